```python
import jax
import jax.numpy as jnp
from jax import lax
import numpy as np

D_MODEL = 2048
BATCH = 4
SEQ = 2048
DEPTH = 2

HEAD_DIM = 64
MIX_HEADS = D_MODEL // HEAD_DIM
MIX_W = MIX_HEADS * HEAD_DIM
FFN_HIDDEN = 5632
PLE_DIM = 256
ROPE_THETA = 10000.0
NORM_EPS = 1e-6
GRID_W = 64

A_HEADS = 12
A_W = A_HEADS * HEAD_DIM
A_BRANCHES = ((128, 1), (512, 4), (2048, 16))
A_BLOCK = 64

B_HEADS = MIX_HEADS - A_HEADS
B_W = B_HEADS * HEAD_DIM
RW_DECAY_LORA = 64
RW_AAA_LORA = 64
RW_GATE_LORA = 192
RW_GN_EPS = 64e-5
RW_COLS = 3 * B_W + 2 * RW_DECAY_LORA + 2 * RW_AAA_LORA + RW_GATE_LORA
RW_SPLITS = (B_W, 2 * B_W, 3 * B_W, 3 * B_W + 2 * RW_DECAY_LORA,
             3 * B_W + 2 * RW_DECAY_LORA + 2 * RW_AAA_LORA)
AB_COLS = 3 * A_W + RW_COLS

C_HEADS = 16
C_KV_HEADS = 2
C_GROUP = C_HEADS // C_KV_HEADS
C_W = C_HEADS * HEAD_DIM
C_KV_W = C_KV_HEADS * HEAD_DIM
C_HALF = 128
C_BLOCK = 128

D_HEADS = MIX_HEADS - C_HEADS
D_W = D_HEADS * HEAD_DIM
NA_KH = 8
NA_KW = 16
NA_QCOLS = 16
CD_SPLITS = (C_W, C_W + C_KV_W, C_W + 2 * C_KV_W, C_W + 2 * C_KV_W + D_W,
             C_W + 2 * C_KV_W + 2 * D_W)
CD_COLS = C_W + 2 * C_KV_W + 3 * D_W

kernel_name = 'hybrid_bidir_encoder_block'


def rms_norm(x, g):
    xf = x.astype(jnp.float32)
    y = xf * lax.rsqrt(jnp.mean(xf * xf, axis=-1, keepdims=True) + NORM_EPS)
    return (y * g.astype(jnp.float32)).astype(x.dtype)


def swiglu(x, w1, w3, w2):
    return (jax.nn.silu(x @ w1) * (x @ w3)) @ w2


def rope(x, pos):
    half = x.shape[-1] // 2
    inv_freq = ROPE_THETA ** (-jnp.arange(half, dtype=jnp.float32) / half)
    ang = pos.astype(jnp.float32)[:, None] * inv_freq[None, :]
    cos = jnp.cos(ang)[None, :, None, :]
    sin = jnp.sin(ang)[None, :, None, :]
    xf = x.astype(jnp.float32)
    x1, x2 = xf[..., :half], xf[..., half:]
    return jnp.concatenate([x1 * cos - x2 * sin, x2 * cos + x1 * sin], axis=-1).astype(x.dtype)


def banded_attention(q, k, v, half, block, sink=None):
    b, hk, g, n, hd = q.shape
    nb = -(-n // block)
    npad = nb * block
    span = block + 2 * half
    qb = jnp.pad(q, ((0, 0), (0, 0), (0, 0), (0, npad - n), (0, 0))).reshape(b, hk, g, nb, block, hd)
    kv_pad = ((0, 0), (0, 0), (half, npad - n + half), (0, 0))
    kidx = jnp.arange(nb)[:, None] * block + jnp.arange(span)[None, :]
    kb = jnp.pad(k, kv_pad)[:, :, kidx]
    vb = jnp.pad(v, kv_pad)[:, :, kidx]
    qpos = jnp.arange(npad).reshape(nb, block)
    kpos = kidx - half
    in_range = (kpos >= 0) & (kpos < n)
    same = qpos[:, :, None] == kpos[:, None, :]
    mask = (jnp.abs(qpos[:, :, None] - kpos[:, None, :]) <= half) & (in_range[:, None, :] | same)
    s = jnp.einsum('bhgnqd,bhnkd->bhgnqk', qb, kb, preferred_element_type=jnp.float32) * (hd ** -0.5)
    s = jnp.where(mask, s, -jnp.inf)
    m = jnp.max(s, axis=-1)
    if sink is not None:
        sk = sink.astype(jnp.float32)[None, :, :, None, None]
        m = jnp.maximum(m, sk)
    p = jnp.exp(s - m[..., None])
    den = jnp.sum(p, axis=-1)
    if sink is not None:
        den = den + jnp.exp(sk - m)
    o = jnp.einsum('bhgnqk,bhnkd->bhgnqd', p, vb.astype(jnp.float32)) / den[..., None]
    lse = m + jnp.log(den)
    o = o.reshape(b, hk, g, npad, hd)[:, :, :, :n]
    lse = lse.reshape(b, hk, g, npad)[..., :n]
    return o, lse


def dilated_attention(q, k, v):
    b, n, h, hd = q.shape
    outs, lses = [], []
    for window, dil in A_BRANCHES:
        half = window // (2 * dil)
        m = n // dil

        def strided(t):
            return t.reshape(b, m, dil, h, hd).transpose(0, 3, 2, 1, 4).reshape(b, h * dil, m, hd)

        o, lse = banded_attention(strided(q)[:, :, None], strided(k), strided(v), half, A_BLOCK)
        outs.append(o[:, :, 0].reshape(b, h, dil, m, hd).transpose(0, 3, 2, 1, 4).reshape(b, n, h, hd))
        lses.append(lse[:, :, 0].reshape(b, h, dil, m).transpose(0, 3, 2, 1).reshape(b, n, h))
    wts = jax.nn.softmax(jnp.stack(lses, axis=0), axis=0)
    return jnp.einsum('kbsh,kbshd->bshd', wts, jnp.stack(outs, axis=0))


def wkv7_scan(r, w, k, v, a, bb):
    n_, h_, d_ = r.shape[1:]

    def step(state, inp):
        r_t, w_t, k_t, v_t, a_t, b_t = inp
        sa = jnp.einsum('nhvk,nhk->nhv', state, a_t)
        state = (state * w_t[:, :, None, :] + sa[..., None] * b_t[:, :, None, :]
                 + v_t[..., None] * k_t[:, :, None, :])
        return state, jnp.einsum('nhvk,nhk->nhv', state, r_t)

    s0 = jnp.zeros((n_, h_, d_, d_), jnp.float32)
    _, y = lax.scan(step, s0, (r, w, k, v, a, bb))
    return y


def rwkv7_bidir(z, mu_prev, mu_next, w0, w2, a0, a2, g2, k_k, k_a, r_k, ln_w, ln_b):
    b, n, _ = z.shape
    z = z.astype(jnp.float32)
    z_prev = jnp.pad(z, ((0, 0), (1, 0), (0, 0)))[:, :-1]
    z_next = jnp.pad(z, ((0, 0), (0, 1), (0, 0)))[:, 1:]
    z = z + mu_prev * (z_prev - z) + mu_next * (z_next - z)
    r, k, v, wl, al, gl = jnp.split(z, RW_SPLITS, axis=-1)
    wl = wl.reshape(b, n, 2, RW_DECAY_LORA)
    al = al.reshape(b, n, 2, RW_AAA_LORA)
    w = -jax.nn.softplus(-(w0 + jnp.einsum('bsel,elc->bsec', jnp.tanh(wl), w2))) - 0.5
    decay = jnp.exp(-jnp.exp(w))
    a = jax.nn.sigmoid(a0 + jnp.einsum('bsel,elc->bsec', al, a2))
    g = jax.nn.sigmoid(gl) @ g2

    def hv(t):
        return t.reshape(t.shape[:-1] + (B_HEADS, HEAD_DIM))

    kk = hv(k * k_k)
    kk = kk / jnp.maximum(jnp.sqrt(jnp.sum(kk * kk, axis=-1, keepdims=True)), 1e-12)
    k_dir = hv(k[:, :, None] * (1.0 + (a - 1.0) * k_a))
    a = hv(a)
    decay = hv(decay)
    r = hv(r)
    v = hv(v)

    def dirs(t_f, t_b):
        t = jnp.stack([t_f, jnp.flip(t_b, axis=1)], axis=0)
        return t.reshape(2 * b, n, B_HEADS, HEAD_DIM).swapaxes(0, 1)

    y = wkv7_scan(dirs(r, r), dirs(decay[:, :, 0], decay[:, :, 1]),
                  dirs(k_dir[:, :, 0], k_dir[:, :, 1]), dirs(v, v), dirs(-kk, -kk),
                  dirs(kk * a[:, :, 0], kk * a[:, :, 1]))
    y = y.swapaxes(0, 1).reshape(2, b, n, B_HEADS, HEAD_DIM)
    y = y[0] + jnp.flip(y[1], axis=1)
    mean = jnp.mean(y, axis=-1, keepdims=True)
    var = jnp.mean(jnp.square(y - mean), axis=-1, keepdims=True)
    yn = (y - mean) * lax.rsqrt(var + RW_GN_EPS) * hv(ln_w) + hv(ln_b)
    bonus = jnp.sum(r[:, :, None] * k_dir * r_k, axis=-1, keepdims=True) * v[:, :, None]
    return (yn + jnp.sum(bonus, axis=2)).reshape(b, n, B_W) * g


def neighborhood_attention(q, k, v, rpb):
    b, n, h, hd = q.shape
    rows = n // GRID_W
    kh = min(NA_KH, rows)
    ncb = GRID_W // NA_QCOLS
    span_c = NA_QCOLS + NA_KW

    def grid(t):
        return t.reshape(b, rows, GRID_W, h, hd).transpose(0, 3, 1, 2, 4)

    qg = grid(q).reshape(b, h, rows, ncb, NA_QCOLS, hd)
    qrow = jnp.arange(rows)
    krow = jnp.clip(qrow - NA_KH // 2, 0, rows - kh)[:, None] + jnp.arange(kh)[None, :]
    kcol = (jnp.clip(jnp.arange(ncb) * NA_QCOLS - NA_KW // 2, 0, GRID_W - span_c)[:, None]
            + jnp.arange(span_c)[None, :])
    ridx = krow[:, None, :, None]
    cidx = kcol[None, :, None, :]
    kb = grid(k)[:, :, ridx, cidx]
    vb = grid(v)[:, :, ridx, cidx]
    qcol = jnp.arange(ncb)[:, None] * NA_QCOLS + jnp.arange(NA_QCOLS)[None, :]
    qc0 = jnp.clip(qcol - NA_KW // 2, 0, GRID_W - NA_KW)
    cmask = (kcol[:, None, :] >= qc0[..., None]) & (kcol[:, None, :] < qc0[..., None] + NA_KW)
    ri = (krow - qrow[:, None]) + NA_KH - 1
    ci = jnp.clip(kcol[:, None, :] - qcol[..., None], -(NA_KW - 1), NA_KW - 1) + NA_KW - 1
    bias = rpb.astype(jnp.float32)[:, ri[:, None, None, :, None], ci[None, :, :, None, :]]
    s = jnp.einsum('bhrcqd,bhrckwd->bhrcqkw', qg, kb, preferred_element_type=jnp.float32) * (hd ** -0.5)
    s = jnp.where(cmask[:, :, None, :], s + bias[None], -jnp.inf)
    p = jax.nn.softmax(s.reshape(s.shape[:-2] + (kh * span_c,)), axis=-1).reshape(s.shape)
    o = jnp.einsum('bhrcqkw,bhrckwd->bhrcqd', p, vb.astype(jnp.float32))
    return o.reshape(b, h, rows, GRID_W, hd).transpose(0, 2, 3, 1, 4).reshape(b, n, h, hd)


def mix_ab(h, pos, w_in, w_out, mu_prev, mu_next, w0, w2, a0, a2, g2, k_k, k_a, r_k, ln_w, ln_b):
    b, n, _ = h.shape
    z = h @ w_in
    qa, ka, va, zb = jnp.split(z, (A_W, 2 * A_W, 3 * A_W), axis=-1)
    heads = lambda t: t.reshape(b, n, -1, HEAD_DIM)
    oa = dilated_attention(rope(heads(qa), pos), rope(heads(ka), pos), heads(va)).reshape(b, n, A_W)
    ob = rwkv7_bidir(zb, mu_prev, mu_next, w0, w2, a0, a2, g2, k_k, k_a, r_k, ln_w, ln_b)
    return jnp.concatenate([oa, ob], axis=-1).astype(h.dtype) @ w_out


def mix_cd(h, pos, w_in, w_out, sink, rpb):
    b, n, _ = h.shape
    z = h @ w_in
    qc, kc, vc, qd, kd, vd = jnp.split(z, CD_SPLITS, axis=-1)
    heads = lambda t: t.reshape(b, n, -1, HEAD_DIM)
    qc = rope(heads(qc), pos).reshape(b, n, C_KV_HEADS, C_GROUP, HEAD_DIM).transpose(0, 2, 3, 1, 4)
    kc = rope(heads(kc), pos).transpose(0, 2, 1, 3)
    vc = heads(vc).transpose(0, 2, 1, 3)
    oc, _ = banded_attention(qc, kc, vc, C_HALF, C_BLOCK, sink)
    oc = oc.transpose(0, 3, 1, 2, 4).reshape(b, n, C_W)
    od = neighborhood_attention(heads(qd), heads(kd), heads(vd), rpb).reshape(b, n, D_W)
    return jnp.concatenate([oc, od], axis=-1).astype(h.dtype) @ w_out


def setup_inputs(seed: int = 0) -> dict:
    key = jax.random.key(seed)
    ks = iter(jax.random.split(key, 40))
    n_ab = (DEPTH + 1) // 2
    n_cd = DEPTH // 2
    D, F = D_MODEL, FFN_HIDDEN

    def nrm(shape, scale):
        return scale * jax.random.normal(next(ks), shape, jnp.float32)

    def gain(shape):
        return 1.0 + nrm(shape, 0.02)

    def unif(shape, lo, hi):
        return jax.random.uniform(next(ks), shape, jnp.float32, minval=lo, maxval=hi)

    return {
        'x': nrm((BATCH, SEQ, D), 1.0),
        'p': nrm((DEPTH, BATCH, SEQ, PLE_DIM), 1.0),
        'ffn1_norm': gain((DEPTH, D)),
        'ffn1_w1': nrm((DEPTH, D, F), D ** -0.5),
        'ffn1_w3': nrm((DEPTH, D, F), D ** -0.5),
        'ffn1_w2': nrm((DEPTH, F, D), F ** -0.5),
        'mix_norm': gain((DEPTH, D)),
        'ffn2_norm': gain((DEPTH, D)),
        'ffn2_w1': nrm((DEPTH, D, F), D ** -0.5),
        'ffn2_w3': nrm((DEPTH, D, F), D ** -0.5),
        'ffn2_w2': nrm((DEPTH, F, D), F ** -0.5),
        'ple_norm': gain((DEPTH, D)),
        'ple_w_gate': nrm((DEPTH, D, D), D ** -0.5),
        'ple_w_proj': nrm((DEPTH, PLE_DIM, D), PLE_DIM ** -0.5),
        'ab_w_in': nrm((n_ab, D, AB_COLS), D ** -0.5),
        'ab_w_out': nrm((n_ab, MIX_W, D), MIX_W ** -0.5),
        'rw_mu_prev': unif((n_ab, RW_COLS), 0.0, 0.5),
        'rw_mu_next': unif((n_ab, RW_COLS), 0.0, 0.5),
        'rw_w0': unif((n_ab, 2, B_W), -4.0, 0.0),
        'rw_w2': nrm((n_ab, 2, RW_DECAY_LORA, B_W), RW_DECAY_LORA ** -0.5),
        'rw_a0': nrm((n_ab, 2, B_W), 0.1),
        'rw_a2': nrm((n_ab, 2, RW_AAA_LORA, B_W), RW_AAA_LORA ** -0.5),
        'rw_g2': nrm((n_ab, RW_GATE_LORA, B_W), RW_GATE_LORA ** -0.5),
        'rw_k_k': 0.85 + nrm((n_ab, B_W), 0.02),
        'rw_k_a': gain((n_ab, B_W)),
        'rw_r_k': nrm((n_ab, B_HEADS, HEAD_DIM), 0.1),
        'rw_ln_w': gain((n_ab, B_W)),
        'rw_ln_b': nrm((n_ab, B_W), 0.02),
        'cd_w_in': nrm((n_cd, D, CD_COLS), D ** -0.5),
        'cd_w_out': nrm((n_cd, MIX_W, D), MIX_W ** -0.5),
        'c_sink': nrm((n_cd, C_KV_HEADS, C_GROUP), 0.5),
        'd_rpb': nrm((n_cd, D_HEADS, 2 * NA_KH - 1, 2 * NA_KW - 1), 0.1),
        'final_norm': gain((D,)),
    }


def reference(x, p, ffn1_norm, ffn1_w1, ffn1_w3, ffn1_w2, mix_norm, ffn2_norm, ffn2_w1, ffn2_w3,
              ffn2_w2, ple_norm, ple_w_gate, ple_w_proj, ab_w_in, ab_w_out, rw_mu_prev, rw_mu_next,
              rw_w0, rw_w2, rw_a0, rw_a2, rw_g2, rw_k_k, rw_k_a, rw_r_k, rw_ln_w, rw_ln_b,
              cd_w_in, cd_w_out, c_sink, d_rpb, final_norm):
    pos = jnp.arange(x.shape[1])
    for i in range(DEPTH):
        j = i // 2
        x = x + 0.5 * swiglu(rms_norm(x, ffn1_norm[i]), ffn1_w1[i], ffn1_w3[i], ffn1_w2[i])
        h = rms_norm(x, mix_norm[i])
        if i % 2 == 0:
            x = x + mix_ab(h, pos, ab_w_in[j], ab_w_out[j], rw_mu_prev[j], rw_mu_next[j], rw_w0[j],
                           rw_w2[j], rw_a0[j], rw_a2[j], rw_g2[j], rw_k_k[j], rw_k_a[j], rw_r_k[j],
                           rw_ln_w[j], rw_ln_b[j])
        else:
            x = x + mix_cd(h, pos, cd_w_in[j], cd_w_out[j], c_sink[j], d_rpb[j])
        x = x + 0.5 * swiglu(rms_norm(x, ffn2_norm[i]), ffn2_w1[i], ffn2_w3[i], ffn2_w2[i])
        gate = jax.nn.sigmoid(rms_norm(x, ple_norm[i]) @ ple_w_gate[i])
        x = x + (p[i] @ ple_w_proj[i]) * gate
    return rms_norm(x, final_norm)
```

```python
import functools

import jax
import jax.numpy as jnp
from jax import lax
from jax.experimental import pallas as pl
from jax.experimental.pallas import tpu as pltpu

F32 = jnp.float32
BF16 = jnp.bfloat16

HEAD_DIM = 64
LANES = 128
NORM_EPS = 1e-6
ROPE_THETA = 10000.0
GRID_W = 64
A_HEADS = 12
A_W = A_HEADS * HEAD_DIM
A_HALF = 64
A_DILATIONS = (1, 4, 16)
B_HEADS = 20
B_W = B_HEADS * HEAD_DIM
RW_LORA = 64
RW_GATE_LORA = 192
RW_GN_EPS = 64e-5
RW_PAD_COLS = 3 * B_W + 4 * LANES
C_HEADS = 16
C_KV_HEADS = 2
C_GROUP = C_HEADS // C_KV_HEADS
C_W = C_HEADS * HEAD_DIM
C_HALF = 128
D_HEADS = 16
D_W = D_HEADS * HEAD_DIM
NA_KH = 8
NA_KW = 16
MASKED = -1e30
WKV_CHUNK = 64
VMEM_LIMIT = 56 * 1024 * 1024


def _params(*sem):
    return pltpu.CompilerParams(dimension_semantics=sem, vmem_limit_bytes=VMEM_LIMIT)


def _dot(a, b):
    return jnp.dot(a.astype(BF16), b.astype(BF16), preferred_element_type=F32)


def _dot_nt(a, b):
    return lax.dot_general(a.astype(BF16), b.astype(BF16), (((1,), (1,)), ((), ())),
                           preferred_element_type=F32)


def _dot_tn(a, b):
    return lax.dot_general(a.astype(BF16), b.astype(BF16), (((0,), (0,)), ((), ())),
                           preferred_element_type=F32)


def _rms_rows(x, g):
    ms = jnp.mean(x * x, axis=-1, keepdims=True)
    return x * lax.rsqrt(ms + NORM_EPS) * g


def _norm_to_scratch(x_ref, g_ref, xn_ref):
    rows = 64

    def body(i, carry):
        sl = pl.ds(pl.multiple_of(i * rows, rows), rows)
        xn_ref[sl, :] = _rms_rows(x_ref[sl, :], g_ref[...]).astype(BF16)
        return carry

    lax.fori_loop(0, x_ref.shape[0] // rows, body, 0)


def _ffn_up_kernel(x_ref, g_ref, w1_ref, w3_ref, o_ref, xn_ref):
    @pl.when(pl.program_id(1) == 0)
    def _():
        _norm_to_scratch(x_ref, g_ref, xn_ref)

    xn = xn_ref[...]
    a = jnp.dot(xn, w1_ref[...], preferred_element_type=F32)
    b = jnp.dot(xn, w3_ref[...], preferred_element_type=F32)
    o_ref[...] = (a * jax.nn.sigmoid(a) * b).astype(o_ref.dtype)


def _ffn_up(x, g, w1, w3, tm=512, tn=512):
    m, d = x.shape
    f = w1.shape[1]
    return pl.pallas_call(
        _ffn_up_kernel,
        grid=(m // tm, f // tn),
        in_specs=[pl.BlockSpec((tm, d), lambda i, j: (i, 0)),
                  pl.BlockSpec((1, d), lambda i, j: (0, 0)),
                  pl.BlockSpec((d, tn), lambda i, j: (0, j)),
                  pl.BlockSpec((d, tn), lambda i, j: (0, j))],
        out_specs=pl.BlockSpec((tm, tn), lambda i, j: (i, j)),
        out_shape=jax.ShapeDtypeStruct((m, f), BF16),
        scratch_shapes=[pltpu.VMEM((tm, d), BF16)],
        compiler_params=_params("parallel", "arbitrary"),
        name="ffn_up",
    )(x, g.reshape(1, d), w1, w3)


def _norm_proj_kernel(x_ref, g_ref, w_ref, o_ref, xn_ref):
    @pl.when(pl.program_id(1) == 0)
    def _():
        _norm_to_scratch(x_ref, g_ref, xn_ref)

    o_ref[...] = jnp.dot(xn_ref[...], w_ref[...], preferred_element_type=F32).astype(o_ref.dtype)


def _norm_proj(x, g, w, tn, tm=512):
    m, d = x.shape
    n = w.shape[1]
    return pl.pallas_call(
        _norm_proj_kernel,
        grid=(m // tm, n // tn),
        in_specs=[pl.BlockSpec((tm, d), lambda i, j: (i, 0)),
                  pl.BlockSpec((1, d), lambda i, j: (0, 0)),
                  pl.BlockSpec((d, tn), lambda i, j: (0, j))],
        out_specs=pl.BlockSpec((tm, tn), lambda i, j: (i, j)),
        out_shape=jax.ShapeDtypeStruct((m, n), F32),
        scratch_shapes=[pltpu.VMEM((tm, d), BF16)],
        compiler_params=_params("parallel", "arbitrary"),
        name="norm_proj",
    )(x, g.reshape(1, d), w)


def _ple_kernel(x_ref, g_ref, w_ref, xres_ref, p_ref, wp_ref, o_ref, xn_ref):
    @pl.when(pl.program_id(1) == 0)
    def _():
        _norm_to_scratch(x_ref, g_ref, xn_ref)

    gate = jax.nn.sigmoid(jnp.dot(xn_ref[...], w_ref[...], preferred_element_type=F32))
    emb = jnp.dot(p_ref[...].astype(BF16), wp_ref[...], preferred_element_type=F32)
    o_ref[...] = xres_ref[...] + emb * gate


def _ple(x, g, w_gate, p, w_proj, tm=512, tn=512):
    m, d = x.shape
    pd = p.shape[1]
    return pl.pallas_call(
        _ple_kernel,
        grid=(m // tm, d // tn),
        in_specs=[pl.BlockSpec((tm, d), lambda i, j: (i, 0)),
                  pl.BlockSpec((1, d), lambda i, j: (0, 0)),
                  pl.BlockSpec((d, tn), lambda i, j: (0, j)),
                  pl.BlockSpec((tm, tn), lambda i, j: (i, j)),
                  pl.BlockSpec((tm, pd), lambda i, j: (i, 0)),
                  pl.BlockSpec((pd, tn), lambda i, j: (0, j))],
        out_specs=pl.BlockSpec((tm, tn), lambda i, j: (i, j)),
        out_shape=jax.ShapeDtypeStruct((m, d), F32),
        scratch_shapes=[pltpu.VMEM((tm, d), BF16)],
        compiler_params=_params("parallel", "arbitrary"),
        name="ple",
    )(x, g.reshape(1, d), w_gate, x, p, w_proj)


def _mm_res_kernel(*refs, n_pairs, scale):
    res_ref, o_ref = refs[2 * n_pairs], refs[2 * n_pairs + 1]
    acc = jnp.dot(refs[0][...], refs[1][...], preferred_element_type=F32)
    for t in range(1, n_pairs):
        acc = acc + jnp.dot(refs[2 * t][...], refs[2 * t + 1][...], preferred_element_type=F32)
    o_ref[...] = res_ref[...] + scale * acc


def _mm_res(pairs, res, scale, tm=512, tn=512):
    m, n = res.shape
    in_specs, args = [], []
    for a, w in pairs:
        k = a.shape[1]
        in_specs += [pl.BlockSpec((tm, k), lambda i, j: (i, 0)),
                     pl.BlockSpec((k, tn), lambda i, j: (0, j))]
        args += [a, w]
    in_specs.append(pl.BlockSpec((tm, tn), lambda i, j: (i, j)))
    return pl.pallas_call(
        functools.partial(_mm_res_kernel, n_pairs=len(pairs), scale=scale),
        grid=(m // tm, n // tn),
        in_specs=in_specs,
        out_specs=pl.BlockSpec((tm, tn), lambda i, j: (i, j)),
        out_shape=jax.ShapeDtypeStruct((m, n), F32),
        compiler_params=_params("parallel", "arbitrary"),
        name="mm_res",
    )(*args, res)


def _final_norm_kernel(x_ref, g_ref, o_ref):
    o_ref[...] = _rms_rows(x_ref[...], g_ref[...])


def _final_norm(x, g, tm=256):
    m, d = x.shape
    return pl.pallas_call(
        _final_norm_kernel,
        grid=(m // tm,),
        in_specs=[pl.BlockSpec((tm, d), lambda i: (i, 0)),
                  pl.BlockSpec((1, d), lambda i: (0, 0))],
        out_specs=pl.BlockSpec((tm, d), lambda i: (i, 0)),
        out_shape=jax.ShapeDtypeStruct((m, d), F32),
        compiler_params=_params("parallel"),
        name="final_norm",
    )(x, g.reshape(1, d))


def _lane_lo(shape=(1, LANES)):
    return lax.broadcasted_iota(jnp.int32, shape, len(shape) - 1) < HEAD_DIM


def _rope_tables(n):
    half = HEAD_DIM // 2
    inv_freq = ROPE_THETA ** (-jnp.arange(half, dtype=F32) / half)
    ang = jnp.arange(n, dtype=F32)[:, None] * inv_freq[None, :]
    cos, sin = jnp.cos(ang), jnp.sin(ang)
    return (jnp.concatenate([cos, cos, cos, cos], axis=-1),
            jnp.concatenate([-sin, sin, -sin, sin], axis=-1))


def _rope_tile(x, cos, sin):
    lane = lax.broadcasted_iota(jnp.int32, (1, LANES), 1)
    first_half = (lane % HEAD_DIM) < (HEAD_DIM // 2)
    partner = jnp.where(first_half, pltpu.roll(x, LANES - HEAD_DIM // 2, 1),
                        pltpu.roll(x, HEAD_DIM // 2, 1))
    return x * cos + partner * sin


def _attn_a_kernel(q_ref, k_ref, v_ref, cos_ref, sin_ref, o_ref, qs, ks, m_s, l_s, acc_s):
    n = q_ref.shape[1]
    lo = _lane_lo()
    rb = 256

    def rope_body(i, carry):
        sl = pl.ds(pl.multiple_of(i * rb, rb), rb)
        c, s = cos_ref[sl, :], sin_ref[sl, :]
        qs[sl, :] = _rope_tile(q_ref[0, sl, :], c, s) * (HEAD_DIM ** -0.5)
        ks[sl, :] = _rope_tile(k_ref[0, sl, :], c, s)
        return carry

    lax.fori_loop(0, n // rb, rope_body, 0)
    m_s[...] = jnp.full(m_s.shape, MASKED, F32)
    l_s[...] = jnp.zeros(l_s.shape, F32)
    acc_s[...] = jnp.zeros(acc_s.shape, F32)

    for dil in A_DILATIONS:
        m = n // dil
        qb_rows = min(128, m)
        span = min(qb_rows + 2 * A_HALF, m)

        def block(r, qb, dil=dil, m=m, qb_rows=qb_rows, span=span):
            q0 = qb * qb_rows
            k0 = jnp.clip(q0 - A_HALF, 0, m - span)
            if dil == 1:
                rows_q = pl.ds(pl.multiple_of(q0, qb_rows), qb_rows)
                rows_k = pl.ds(pl.multiple_of(k0, A_HALF), span)
            else:
                rows_q = pl.ds(r + q0 * dil, qb_rows, stride=dil)
                rows_k = pl.ds(r + k0 * dil, span, stride=dil)
            q = qs[rows_q, :]
            kb = ks[rows_k, :].astype(BF16)
            vb = v_ref[0, rows_k, :].astype(BF16)
            qpos = q0 + lax.broadcasted_iota(jnp.int32, (qb_rows, 1), 0)
            kpos = k0 + lax.broadcasted_iota(jnp.int32, (1, span), 1)
            mask = jnp.abs(qpos - kpos) <= A_HALF
            s0 = jnp.where(mask, _dot_nt(jnp.where(lo, q, 0.0), kb), MASKED)
            s1 = jnp.where(mask, _dot_nt(jnp.where(lo, 0.0, q), kb), MASKED)
            m_old = m_s[rows_q, :]
            mn0 = jnp.maximum(m_old[:, :1], jnp.max(s0, axis=1, keepdims=True))
            mn1 = jnp.maximum(m_old[:, LANES - 1:], jnp.max(s1, axis=1, keepdims=True))
            p0 = jnp.exp(s0 - mn0)
            p1 = jnp.exp(s1 - mn1)
            m_new = jnp.where(lo, mn0, mn1)
            alpha = jnp.exp(m_old - m_new)
            psum = jnp.where(lo, jnp.sum(p0, axis=1, keepdims=True), jnp.sum(p1, axis=1, keepdims=True))
            pv = jnp.where(lo, _dot(p0, vb), _dot(p1, vb))
            m_s[rows_q, :] = m_new
            l_s[rows_q, :] = alpha * l_s[rows_q, :] + psum
            acc_s[rows_q, :] = alpha * acc_s[rows_q, :] + pv

        def res_body(r, carry, block=block, nqb=m // qb_rows):
            def qb_body(qb, c2):
                block(r, qb)
                return c2

            return lax.fori_loop(0, nqb, qb_body, carry)

        lax.fori_loop(0, dil, res_body, 0)

    def out_body(i, carry):
        sl = pl.ds(pl.multiple_of(i * rb, rb), rb)
        o_ref[0, sl, :] = (acc_s[sl, :] / l_s[sl, :]).astype(o_ref.dtype)
        return carry

    lax.fori_loop(0, n // rb, out_body, 0)


def _attn_a(za, cos, sin):
    b, n, _ = za.shape
    npair = A_W // LANES
    blk = lambda off: pl.BlockSpec((1, n, LANES), lambda i, j, off=off: (i, 0, off + j))
    tab = pl.BlockSpec((n, LANES), lambda i, j: (0, 0))
    return pl.pallas_call(
        _attn_a_kernel,
        grid=(b, npair),
        in_specs=[blk(0), blk(npair), blk(2 * npair), tab, tab],
        out_specs=pl.BlockSpec((1, n, LANES), lambda i, j: (i, 0, j)),
        out_shape=jax.ShapeDtypeStruct((b, n, A_W), BF16),
        scratch_shapes=[pltpu.VMEM((n, LANES), F32)] * 5,
        compiler_params=_params("parallel", "parallel"),
        name="attn_dilated",
    )(za, za, za, cos, sin)


def _attn_c_kernel(sink_ref, q_ref, k_ref, v_ref, cosq_ref, sinq_ref, cos_ref, sin_ref, o_ref,
                   klo, khi, vlo, vhi):
    n = k_ref.shape[1]
    qb_rows = q_ref.shape[1]
    span = qb_rows + 2 * C_HALF
    hk = pl.program_id(1)
    qb = pl.program_id(2)
    lo = _lane_lo()

    @pl.when(qb == 0)
    def _():
        rb = 256

        def body(i, carry):
            sl = pl.ds(pl.multiple_of(i * rb, rb), rb)
            kx = _rope_tile(k_ref[0, sl, :], cos_ref[sl, :], sin_ref[sl, :])
            kw = pltpu.roll(kx, HEAD_DIM, 1)
            vx = v_ref[0, sl, :]
            vw = pltpu.roll(vx, HEAD_DIM, 1)
            first = hk == 0
            klo[sl, :] = jnp.where(lo, jnp.where(first, kx, kw), 0.0).astype(BF16)
            khi[sl, :] = jnp.where(lo, 0.0, jnp.where(first, kw, kx)).astype(BF16)
            vlo[sl, :] = jnp.where(lo, jnp.where(first, vx, vw), 0.0).astype(BF16)
            vhi[sl, :] = jnp.where(lo, 0.0, jnp.where(first, vw, vx)).astype(BF16)
            return carry

        lax.fori_loop(0, n // rb, body, 0)

    q0 = qb * qb_rows
    k0 = pl.multiple_of(jnp.clip(q0 - C_HALF, 0, n - span), C_HALF)
    rows_k = pl.ds(k0, span)
    kl, kh = klo[rows_k, :], khi[rows_k, :]
    vl, vh = vlo[rows_k, :], vhi[rows_k, :]
    qpos = q0 + lax.broadcasted_iota(jnp.int32, (qb_rows, 1), 0)
    kpos = k0 + lax.broadcasted_iota(jnp.int32, (1, span), 1)
    mask = jnp.abs(qpos - kpos) <= C_HALF
    cq, sq = cosq_ref[...], sinq_ref[...]

    def probs(qt, kk, sk):
        s = jnp.where(mask, _dot_nt(qt, kk), MASKED)
        mx = jnp.maximum(jnp.max(s, axis=1, keepdims=True), sk)
        p = jnp.exp(s - mx)
        den = jnp.sum(p, axis=1, keepdims=True) + jnp.exp(sk - mx)
        return p, den

    for j in range(q_ref.shape[2] // LANES):
        qt = _rope_tile(q_ref[0, :, j * LANES:(j + 1) * LANES], cq, sq) * (HEAD_DIM ** -0.5)
        p_e, den_e = probs(qt, kl, sink_ref[hk * C_GROUP + 2 * j])
        p_o, den_o = probs(qt, kh, sink_ref[hk * C_GROUP + 2 * j + 1])
        pv = _dot(p_e, vl) + _dot(p_o, vh)
        o_ref[0, :, j * LANES:(j + 1) * LANES] = (pv / jnp.where(lo, den_e, den_o)).astype(o_ref.dtype)


def _attn_c(zc, sink, cos, sin, qb_rows=128):
    b, n, _ = zc.shape
    gw = C_GROUP * HEAD_DIM
    kcol = C_W // LANES
    full = lambda off: pl.BlockSpec((1, n, LANES), lambda i, h, t, off=off: (i, 0, off))
    tabq = pl.BlockSpec((qb_rows, LANES), lambda i, h, t: (t, 0))
    tab = pl.BlockSpec((n, LANES), lambda i, h, t: (0, 0))
    return pl.pallas_call(
        _attn_c_kernel,
        grid=(b, C_KV_HEADS, n // qb_rows),
        in_specs=[pl.BlockSpec(memory_space=pltpu.SMEM),
                  pl.BlockSpec((1, qb_rows, gw), lambda i, h, t: (i, t, h)),
                  full(kcol), full(kcol + 1), tabq, tabq, tab, tab],
        out_specs=pl.BlockSpec((1, qb_rows, gw), lambda i, h, t: (i, t, h)),
        out_shape=jax.ShapeDtypeStruct((b, n, C_W), BF16),
        scratch_shapes=[pltpu.VMEM((n, LANES), BF16)] * 4,
        compiler_params=_params("parallel", "parallel", "arbitrary"),
        name="attn_gqa_sink",
    )(sink.reshape(-1), zc, zc, zc, cos, sin, cos, sin)


def _na_bias_table(rpb, rows):
    kh = min(NA_KH, rows)
    qc = jnp.arange(GRID_W)[:, None]
    kc = jnp.arange(GRID_W)[None, :]
    qc0 = jnp.clip(qc - NA_KW // 2, 0, GRID_W - NA_KW)
    valid = (kc >= qc0) & (kc < qc0 + NA_KW)
    ci = jnp.clip(kc - qc, -(NA_KW - 1), NA_KW - 1) + NA_KW - 1
    off = jnp.arange(NA_KH)[:, None]
    kr = jnp.arange(kh)[None, :]
    ri = jnp.clip(off + kr, 0, 2 * NA_KH - 2)
    bias = rpb.astype(F32)[:, ri[:, None, :, None], ci[None, :, None, :]]
    bias = jnp.where(valid[None, None, :, None, :], bias, MASKED)
    bias = bias.reshape(D_HEADS // 2, 2, NA_KH, GRID_W, kh * GRID_W)
    return bias.transpose(0, 2, 1, 3, 4).reshape(D_HEADS // 2, NA_KH, 2 * GRID_W, kh * GRID_W)


def _attn_d_kernel(q_ref, k_ref, v_ref, bias_ref, o_ref):
    n = q_ref.shape[1]
    rows = n // GRID_W
    kh = min(NA_KH, rows)
    lo = _lane_lo()

    def body(r, carry):
        r0 = jnp.clip(r - NA_KH // 2, 0, rows - kh)
        rows_q = pl.ds(pl.multiple_of(r * GRID_W, GRID_W), GRID_W)
        rows_k = pl.ds(pl.multiple_of(r0 * GRID_W, GRID_W), kh * GRID_W)
        q = q_ref[0, rows_q, :] * (HEAD_DIM ** -0.5)
        q2 = jnp.concatenate([jnp.where(lo, q, 0.0), jnp.where(lo, 0.0, q)], axis=0)
        s = _dot_nt(q2, k_ref[0, rows_k, :]) + bias_ref[0, r0 - r + NA_KH - 1]
        p = jnp.exp(s - jnp.max(s, axis=1, keepdims=True))
        pv = _dot(p, v_ref[0, rows_k, :]) / jnp.sum(p, axis=1, keepdims=True)
        o_ref[0, rows_q, :] = jnp.where(lo, pv[:GRID_W], pv[GRID_W:]).astype(o_ref.dtype)
        return carry

    lax.fori_loop(0, rows, body, 0)


def _attn_d(zc, bias):
    b, n, _ = zc.shape
    npair = D_W // LANES
    q_off = (C_W + 2 * C_KV_HEADS * HEAD_DIM) // LANES
    blk = lambda off: pl.BlockSpec((1, n, LANES), lambda i, j, off=off: (i, 0, off + j))
    return pl.pallas_call(
        _attn_d_kernel,
        grid=(b, npair),
        in_specs=[blk(q_off), blk(q_off + npair), blk(q_off + 2 * npair),
                  pl.BlockSpec((1,) + bias.shape[1:], lambda i, j: (j, 0, 0, 0))],
        out_specs=pl.BlockSpec((1, n, LANES), lambda i, j: (i, 0, j)),
        out_shape=jax.ShapeDtypeStruct((b, n, D_W), BF16),
        compiler_params=_params("parallel", "parallel"),
        name="attn_neighborhood",
    )(zc, zc, zc, bias)


def _rw_prep_kernel(z_ref, hp_ref, hn_ref, mup_ref, mun_ref, w0_ref, w2f_ref, w2b_ref, a0_ref,
                    a2f_ref, a2b_ref, g2_ref, r_o, k_o, v_o, lwf_o, lwb_o, af_o, ab_o, g_o):
    tm = z_ref.shape[1]
    z = z_ref[0]
    row = lax.broadcasted_iota(jnp.int32, (tm, 1), 0)
    z_prev = jnp.where(row == 0, hp_ref[0, 0], pltpu.roll(z, 1, 0))
    z_next = jnp.where(row == tm - 1, hn_ref[0, 0], pltpu.roll(z, tm - 1, 0))
    z = z + mup_ref[...] * (z_prev - z) + mun_ref[...] * (z_next - z)
    base = 3 * B_W
    wl = jnp.tanh(z[:, base:base + LANES])
    al = z[:, base + LANES:base + 2 * LANES]
    gl = jax.nn.sigmoid(z[:, base + 2 * LANES:base + 4 * LANES])

    def log_decay(e, w2_ref):
        x = w0_ref[e:e + 1, :] + _dot(wl, w2_ref[...])
        w = -(jnp.maximum(-x, 0.0) + jnp.log1p(jnp.exp(-jnp.abs(x)))) - 0.5
        return -jnp.exp(w)

    def rate(e, a2_ref):
        return jax.nn.sigmoid(a0_ref[e:e + 1, :] + _dot(al, a2_ref[...]))

    outs = ((r_o, z[:, :B_W]), (k_o, z[:, B_W:2 * B_W]), (v_o, z[:, 2 * B_W:3 * B_W]),
            (lwf_o, log_decay(0, w2f_ref)), (lwb_o, log_decay(1, w2b_ref)),
            (af_o, rate(0, a2f_ref)), (ab_o, rate(1, a2b_ref)), (g_o, _dot(gl, g2_ref[...])))
    for ref, val in outs:
        for h in range(B_HEADS):
            ref[0, h] = val[:, h * HEAD_DIM:(h + 1) * HEAD_DIM]


def _rw_prep(zb, mu_prev, mu_next, w0, w2, a0, a2, g2, tm=128):
    b, n, cols = zb.shape
    nblk = n // tm
    zero = jnp.zeros((b, 1, cols), F32)
    halo_prev = jnp.concatenate([zero, zb[:, tm - 1:n - 1:tm]], axis=1).reshape(b, nblk, 1, cols)
    halo_next = jnp.concatenate([zb[:, tm::tm], zero], axis=1).reshape(b, nblk, 1, cols)
    pad_cols = lambda t: jnp.pad(t, (0, cols - t.shape[0])).reshape(1, cols)
    rows_f = lambda t: jnp.pad(t, ((0, LANES - RW_LORA), (0, 0))).astype(BF16)
    rows_b = lambda t: jnp.pad(t, ((RW_LORA, LANES - 2 * RW_LORA), (0, 0))).astype(BF16)
    g2p = jnp.pad(g2, ((0, 2 * LANES - RW_GATE_LORA), (0, 0))).astype(BF16)
    const = lambda shape: pl.BlockSpec(shape, lambda i, t: (0,) * len(shape))
    halo = pl.BlockSpec((1, 1, 1, cols), lambda i, t: (i, t, 0, 0))
    hm = jax.ShapeDtypeStruct((b, B_HEADS, n, HEAD_DIM), F32)
    return pl.pallas_call(
        _rw_prep_kernel,
        grid=(b, nblk),
        in_specs=[pl.BlockSpec((1, tm, cols), lambda i, t: (i, t, 0)), halo, halo,
                  const((1, cols)), const((1, cols)), const((2, B_W)), const((LANES, B_W)),
                  const((LANES, B_W)), const((2, B_W)), const((LANES, B_W)), const((LANES, B_W)),
                  const((2 * LANES, B_W))],
        out_specs=[pl.BlockSpec((1, B_HEADS, tm, HEAD_DIM), lambda i, t: (i, 0, t, 0))] * 8,
        out_shape=[hm] * 8,
        compiler_params=_params("parallel", "parallel"),
        name="rwkv_prep",
    )(zb, halo_prev, halo_next, pad_cols(mu_prev), pad_cols(mu_next), w0, rows_f(w2[0]), rows_b(w2[1]),
      a0, rows_f(a2[0]), rows_b(a2[1]), g2p)


def _wkv_chunk(r, k, v, lw, rate, st, k_k, k_a, rev):
    c = r.shape[0]
    ii = lax.broadcasted_iota(jnp.int32, (c, c), 0)
    jj = lax.broadcasted_iota(jnp.int32, (c, c), 1)
    incl = (jj >= ii) if rev else (jj <= ii)
    strict = (jj > ii) if rev else (jj < ii)
    eye = ii == jj

    kk = k * k_k
    kk = kk / jnp.maximum(jnp.sqrt(jnp.sum(kk * kk, axis=1, keepdims=True)), 1e-12)
    kd = k * (1.0 + (rate - 1.0) * k_a)
    a = -kk
    b = kk * rate

    cum = jnp.dot(incl.astype(F32), lw, precision=lax.Precision.HIGHEST, preferred_element_type=F32)
    tot = cum[0:1] if rev else cum[c - 1:c]
    at = a * jnp.exp(cum - lw)
    rt = r * jnp.exp(cum)
    g_inv = jnp.exp(-cum)
    g_end = jnp.exp(tot - cum)
    ar = jnp.concatenate([at, rt], axis=0)
    x1 = _dot_nt(ar, b * g_inv)
    x2 = _dot_nt(ar, kd * g_inv)
    a_ab = jnp.where(strict, x1[:c], 0.0)
    a_rb = jnp.where(incl, x1[c:], 0.0)
    a_k = jnp.concatenate([jnp.where(strict, x2[:c], 0.0), jnp.where(incl, x2[c:], 0.0)], axis=0)

    inv = eye.astype(F32) + a_ab
    pw = _dot(a_ab, a_ab)
    steps = c.bit_length() - 2
    for _ in range(steps - 1):
        both = _dot(jnp.concatenate([inv, pw], axis=0), pw)
        inv = inv + both[:c]
        pw = both[c:]
    inv = inv + _dot(inv, pw)

    akv = _dot(a_k, v)
    ars = _dot(ar, st)
    p = _dot(inv, ars[:c] + akv[:c])
    y = ars[c:] + _dot(a_rb, p) + akv[c:]
    g_col = jnp.sum(jnp.where(eye, jnp.exp(tot), 0.0), axis=1, keepdims=True)
    bk = jnp.concatenate([b * g_end, kd * g_end], axis=0)
    pv = jnp.concatenate([p, v], axis=0)
    return y, g_col * st + _dot_tn(bk, pv)


def _rw_scan_kernel(rf, kf, vf, lwf, af, rb, kb, vb, lwb, ab, kk_ref, ka_ref, yf, yb, sf, sb):
    hb, tb = rf.shape[1], rf.shape[2]
    c = min(WKV_CHUNK, tb)
    nch = tb // c

    @pl.when(pl.program_id(2) == 0)
    def _():
        sf[...] = jnp.zeros(sf.shape, F32)
        sb[...] = jnp.zeros(sb.shape, F32)

    def body(ci, carry):
        rows_f = pl.ds(pl.multiple_of(ci * c, c), c)
        rows_b = pl.ds(pl.multiple_of((nch - 1 - ci) * c, c), c)
        for h in range(hb):
            k_k, k_a = kk_ref[h], ka_ref[h]
            y, s = _wkv_chunk(rf[0, h, rows_f, :], kf[0, h, rows_f, :], vf[0, h, rows_f, :],
                              lwf[0, h, rows_f, :], af[0, h, rows_f, :], sf[h], k_k, k_a, False)
            yf[0, h, rows_f, :] = y
            sf[h] = s
            y, s = _wkv_chunk(rb[0, h, rows_b, :], kb[0, h, rows_b, :], vb[0, h, rows_b, :],
                              lwb[0, h, rows_b, :], ab[0, h, rows_b, :], sb[h], k_k, k_a, True)
            yb[0, h, rows_b, :] = y
            sb[h] = s
        return carry

    lax.fori_loop(0, nch, body, 0)


def _rw_scan(r, k, v, lwf, lwb, af, ab, k_k, k_a, hb=4, tb=256):
    b, nh, n, hd = r.shape
    tb = min(tb, n)
    nt = n // tb
    fwd = pl.BlockSpec((1, hb, tb, hd), lambda i, g, t: (i, g, t, 0))
    bwd = pl.BlockSpec((1, hb, tb, hd), lambda i, g, t: (i, g, nt - 1 - t, 0))
    par = pl.BlockSpec((hb, 1, hd), lambda i, g, t: (g, 0, 0))
    out = jax.ShapeDtypeStruct((b, nh, n, hd), F32)
    return pl.pallas_call(
        _rw_scan_kernel,
        grid=(b, nh // hb, nt),
        in_specs=[fwd] * 5 + [bwd] * 5 + [par, par],
        out_specs=[fwd, bwd],
        out_shape=[out, out],
        scratch_shapes=[pltpu.VMEM((hb, hd, hd), F32)] * 2,
        compiler_params=_params("parallel", "parallel", "arbitrary"),
        name="rwkv_scan",
    )(r, k, v, lwf, af, r, k, v, lwb, ab, k_k.reshape(nh, 1, hd), k_a.reshape(nh, 1, hd))


def _rw_post_kernel(yf, yb, r, k, v, af, ab, g, ka_ref, rk_ref, lnw_ref, lnb_ref, o_ref):
    outs = []
    for h in range(B_HEADS):
        y = yf[0, h] + yb[0, h]
        mean = jnp.mean(y, axis=1, keepdims=True)
        var = jnp.mean(jnp.square(y - mean), axis=1, keepdims=True)
        yn = (y - mean) * lax.rsqrt(var + RW_GN_EPS) * lnw_ref[h] + lnb_ref[h]
        kh, k_a = k[0, h], ka_ref[h]
        kd = kh * (1.0 + (af[0, h] - 1.0) * k_a) + kh * (1.0 + (ab[0, h] - 1.0) * k_a)
        bonus = jnp.sum(r[0, h] * kd * rk_ref[h], axis=1, keepdims=True) * v[0, h]
        outs.append((yn + bonus) * g[0, h])
    o_ref[0] = jnp.concatenate(outs, axis=1).astype(o_ref.dtype)


def _rw_post(yf, yb, r, k, v, af, ab, g, k_a, r_k, ln_w, ln_b, tm=128):
    b, nh, n, hd = r.shape
    tm = min(tm, n)
    hm = pl.BlockSpec((1, nh, tm, hd), lambda i, t: (i, 0, t, 0))
    par = pl.BlockSpec((nh, 1, hd), lambda i, t: (0, 0, 0))
    return pl.pallas_call(
        _rw_post_kernel,
        grid=(b, n // tm),
        in_specs=[hm] * 8 + [par] * 4,
        out_specs=pl.BlockSpec((1, tm, nh * hd), lambda i, t: (i, t, 0)),
        out_shape=jax.ShapeDtypeStruct((b, n, nh * hd), BF16),
        compiler_params=_params("parallel", "parallel"),
        name="rwkv_post",
    )(yf, yb, r, k, v, af, ab, g, k_a.reshape(nh, 1, hd), r_k.reshape(nh, 1, hd),
      ln_w.reshape(nh, 1, hd), ln_b.reshape(nh, 1, hd))


def _rwkv7_bidir(zb, mu_prev, mu_next, w0, w2, a0, a2, g2, k_k, k_a, r_k, ln_w, ln_b):
    r, k, v, lwf, lwb, af, ab, g = _rw_prep(zb, mu_prev, mu_next, w0, w2, a0, a2, g2)
    yf, yb = _rw_scan(r, k, v, lwf, lwb, af, ab, k_k, k_a)
    return _rw_post(yf, yb, r, k, v, af, ab, g, k_a, r_k, ln_w, ln_b)


def _ffn_half_step(x, g, w1, w3, w2):
    h = _ffn_up(x, g, w1.astype(BF16), w3.astype(BF16))
    return _mm_res([(h, w2.astype(BF16))], x, 0.5)


def _mix_ab(x, g, b, n, cos, sin, w_in, w_out, mu_prev, mu_next, w0, w2, a0, a2, g2, k_k, k_a, r_k,
            ln_w, ln_b):
    qkv = 3 * A_W
    w_a = w_in[:, :qkv].astype(BF16)
    w_b = jnp.pad(w_in[:, qkv:], ((0, 0), (0, RW_PAD_COLS - (w_in.shape[1] - qkv)))).astype(BF16)
    za = _norm_proj(x, g, w_a, tn=qkv // 3).reshape(b, n, qkv)
    zb = _norm_proj(x, g, w_b, tn=RW_PAD_COLS // 2).reshape(b, n, RW_PAD_COLS)
    oa = _attn_a(za, cos, sin).reshape(b * n, A_W)
    ob = _rwkv7_bidir(zb, mu_prev, mu_next, w0, w2, a0, a2, g2, k_k, k_a, r_k, ln_w, ln_b)
    w_out = w_out.astype(BF16)
    return _mm_res([(oa, w_out[:A_W]), (ob.reshape(b * n, B_W), w_out[A_W:])], x, 1.0)


def _mix_cd(x, g, b, n, cos, sin, w_in, w_out, sink, rpb):
    cols = w_in.shape[1]
    zc = _norm_proj(x, g, w_in.astype(BF16), tn=cols // 2).reshape(b, n, cols)
    oc = _attn_c(zc, sink, cos, sin).reshape(b * n, C_W)
    od = _attn_d(zc, _na_bias_table(rpb, n // GRID_W)).reshape(b * n, D_W)
    w_out = w_out.astype(BF16)
    return _mm_res([(oc, w_out[:C_W]), (od, w_out[C_W:])], x, 1.0)


def kernel(x, p, ffn1_norm, ffn1_w1, ffn1_w3, ffn1_w2, mix_norm, ffn2_norm, ffn2_w1, ffn2_w3, ffn2_w2, ple_norm, ple_w_gate, ple_w_proj, ab_w_in, ab_w_out, rw_mu_prev, rw_mu_next, rw_w0, rw_w2, rw_a0, rw_a2, rw_g2, rw_k_k, rw_k_a, rw_r_k, rw_ln_w, rw_ln_b, cd_w_in, cd_w_out, c_sink, d_rpb, final_norm):
    b, n, d = x.shape
    depth = p.shape[0]
    cos, sin = _rope_tables(n)
    x = x.reshape(b * n, d)
    for i in range(depth):
        j = i // 2
        x = _ffn_half_step(x, ffn1_norm[i], ffn1_w1[i], ffn1_w3[i], ffn1_w2[i])
        if i % 2 == 0:
            x = _mix_ab(x, mix_norm[i], b, n, cos, sin, ab_w_in[j], ab_w_out[j], rw_mu_prev[j],
                        rw_mu_next[j], rw_w0[j], rw_w2[j], rw_a0[j], rw_a2[j], rw_g2[j], rw_k_k[j],
                        rw_k_a[j], rw_r_k[j], rw_ln_w[j], rw_ln_b[j])
        else:
            x = _mix_cd(x, mix_norm[i], b, n, cos, sin, cd_w_in[j], cd_w_out[j], c_sink[j], d_rpb[j])
        x = _ffn_half_step(x, ffn2_norm[i], ffn2_w1[i], ffn2_w3[i], ffn2_w2[i])
        x = _ple(x, ple_norm[i], ple_w_gate[i].astype(BF16), p[i].reshape(b * n, -1),
                 ple_w_proj[i].astype(BF16))
    return _final_norm(x, final_norm).reshape(b, n, d)
```

```python
import functools

import jax
import jax.numpy as jnp
from jax import lax
from jax.experimental import pallas as pl
from jax.experimental.pallas import tpu as pltpu

F32 = jnp.float32
BF16 = jnp.bfloat16

HEAD_DIM = 64
LANES = 128
NORM_EPS = 1e-6
ROPE_THETA = 10000.0
GRID_W = 64
A_HEADS = 12
A_W = A_HEADS * HEAD_DIM
A_HALF = 64
A_DILATIONS = (1, 4, 16)
B_HEADS = 20
B_W = B_HEADS * HEAD_DIM
RW_LORA = 64
RW_GATE_LORA = 192
RW_GN_EPS = 64e-5
RW_PAD_COLS = 3 * B_W + 4 * LANES
C_HEADS = 16
C_KV_HEADS = 2
C_GROUP = C_HEADS // C_KV_HEADS
C_W = C_HEADS * HEAD_DIM
C_HALF = 128
D_HEADS = 16
D_W = D_HEADS * HEAD_DIM
NA_KH = 8
NA_KW = 16
MASKED = -1e30
WKV_CHUNK = 64
VMEM_LIMIT = 56 * 1024 * 1024


def _params(*sem):
    return pltpu.CompilerParams(dimension_semantics=sem, vmem_limit_bytes=VMEM_LIMIT)


def _dot(a, b):
    return jnp.dot(a.astype(BF16), b.astype(BF16), preferred_element_type=F32)


def _dot_nt(a, b):
    return lax.dot_general(a.astype(BF16), b.astype(BF16), (((1,), (1,)), ((), ())),
                           preferred_element_type=F32)


def _dot_tn(a, b):
    return lax.dot_general(a.astype(BF16), b.astype(BF16), (((0,), (0,)), ((), ())),
                           preferred_element_type=F32)


def _rms_rows(x, g):
    ms = jnp.mean(x * x, axis=-1, keepdims=True)
    return x * lax.rsqrt(ms + NORM_EPS) * g


def _norm_to_scratch(x_ref, g_ref, xn_ref):
    rows = 64

    def body(i, carry):
        sl = pl.ds(pl.multiple_of(i * rows, rows), rows)
        xn_ref[sl, :] = _rms_rows(x_ref[sl, :], g_ref[...]).astype(BF16)
        return carry

    lax.fori_loop(0, x_ref.shape[0] // rows, body, 0)


def _ffn_up_kernel(x_ref, g_ref, w1_ref, w3_ref, o_ref, xn_ref):
    @pl.when(pl.program_id(1) == 0)
    def _():
        _norm_to_scratch(x_ref, g_ref, xn_ref)

    xn = xn_ref[...]
    a = jnp.dot(xn, w1_ref[...], preferred_element_type=F32)
    b = jnp.dot(xn, w3_ref[...], preferred_element_type=F32)
    o_ref[...] = (a * jax.nn.sigmoid(a) * b).astype(o_ref.dtype)


def _ffn_up(x, g, w1, w3, tm=512, tn=512):
    m, d = x.shape
    f = w1.shape[1]
    return pl.pallas_call(
        _ffn_up_kernel,
        grid=(m // tm, f // tn),
        in_specs=[pl.BlockSpec((tm, d), lambda i, j: (i, 0)),
                  pl.BlockSpec((1, d), lambda i, j: (0, 0)),
                  pl.BlockSpec((d, tn), lambda i, j: (0, j)),
                  pl.BlockSpec((d, tn), lambda i, j: (0, j))],
        out_specs=pl.BlockSpec((tm, tn), lambda i, j: (i, j)),
        out_shape=jax.ShapeDtypeStruct((m, f), BF16),
        scratch_shapes=[pltpu.VMEM((tm, d), BF16)],
        compiler_params=_params("parallel", "arbitrary"),
        name="ffn_up",
    )(x, g.reshape(1, d), w1, w3)


def _norm_proj_kernel(x_ref, g_ref, w_ref, o_ref, xn_ref):
    @pl.when(pl.program_id(1) == 0)
    def _():
        _norm_to_scratch(x_ref, g_ref, xn_ref)

    o_ref[...] = jnp.dot(xn_ref[...], w_ref[...], preferred_element_type=F32).astype(o_ref.dtype)


def _norm_proj(x, g, w, tn, tm=512):
    m, d = x.shape
    n = w.shape[1]
    return pl.pallas_call(
        _norm_proj_kernel,
        grid=(m // tm, n // tn),
        in_specs=[pl.BlockSpec((tm, d), lambda i, j: (i, 0)),
                  pl.BlockSpec((1, d), lambda i, j: (0, 0)),
                  pl.BlockSpec((d, tn), lambda i, j: (0, j))],
        out_specs=pl.BlockSpec((tm, tn), lambda i, j: (i, j)),
        out_shape=jax.ShapeDtypeStruct((m, n), F32),
        scratch_shapes=[pltpu.VMEM((tm, d), BF16)],
        compiler_params=_params("parallel", "arbitrary"),
        name="norm_proj",
    )(x, g.reshape(1, d), w)


def _ple_kernel(x_ref, g_ref, w_ref, xres_ref, p_ref, wp_ref, o_ref, xn_ref):
    @pl.when(pl.program_id(1) == 0)
    def _():
        _norm_to_scratch(x_ref, g_ref, xn_ref)

    gate = jax.nn.sigmoid(jnp.dot(xn_ref[...], w_ref[...], preferred_element_type=F32))
    emb = jnp.dot(p_ref[...].astype(BF16), wp_ref[...], preferred_element_type=F32)
    o_ref[...] = xres_ref[...] + emb * gate


def _ple(x, g, w_gate, p, w_proj, tm=512, tn=512):
    m, d = x.shape
    pd = p.shape[1]
    return pl.pallas_call(
        _ple_kernel,
        grid=(m // tm, d // tn),
        in_specs=[pl.BlockSpec((tm, d), lambda i, j: (i, 0)),
                  pl.BlockSpec((1, d), lambda i, j: (0, 0)),
                  pl.BlockSpec((d, tn), lambda i, j: (0, j)),
                  pl.BlockSpec((tm, tn), lambda i, j: (i, j)),
                  pl.BlockSpec((tm, pd), lambda i, j: (i, 0)),
                  pl.BlockSpec((pd, tn), lambda i, j: (0, j))],
        out_specs=pl.BlockSpec((tm, tn), lambda i, j: (i, j)),
        out_shape=jax.ShapeDtypeStruct((m, d), F32),
        scratch_shapes=[pltpu.VMEM((tm, d), BF16)],
        compiler_params=_params("parallel", "arbitrary"),
        name="ple",
    )(x, g.reshape(1, d), w_gate, x, p, w_proj)


def _mm_res_kernel(*refs, n_pairs, scale):
    res_ref, o_ref = refs[2 * n_pairs], refs[2 * n_pairs + 1]
    acc = jnp.dot(refs[0][...], refs[1][...], preferred_element_type=F32)
    for t in range(1, n_pairs):
        acc = acc + jnp.dot(refs[2 * t][...], refs[2 * t + 1][...], preferred_element_type=F32)
    o_ref[...] = res_ref[...] + scale * acc


def _mm_res(pairs, res, scale, tm=512, tn=512):
    m, n = res.shape
    in_specs, args = [], []
    for a, w in pairs:
        k = a.shape[1]
        in_specs += [pl.BlockSpec((tm, k), lambda i, j: (i, 0)),
                     pl.BlockSpec((k, tn), lambda i, j: (0, j))]
        args += [a, w]
    in_specs.append(pl.BlockSpec((tm, tn), lambda i, j: (i, j)))
    return pl.pallas_call(
        functools.partial(_mm_res_kernel, n_pairs=len(pairs), scale=scale),
        grid=(m // tm, n // tn),
        in_specs=in_specs,
        out_specs=pl.BlockSpec((tm, tn), lambda i, j: (i, j)),
        out_shape=jax.ShapeDtypeStruct((m, n), F32),
        compiler_params=_params("parallel", "arbitrary"),
        name="mm_res",
    )(*args, res)


def _final_norm_kernel(x_ref, g_ref, o_ref):
    o_ref[...] = _rms_rows(x_ref[...], g_ref[...])


def _final_norm(x, g, tm=256):
    m, d = x.shape
    return pl.pallas_call(
        _final_norm_kernel,
        grid=(m // tm,),
        in_specs=[pl.BlockSpec((tm, d), lambda i: (i, 0)),
                  pl.BlockSpec((1, d), lambda i: (0, 0))],
        out_specs=pl.BlockSpec((tm, d), lambda i: (i, 0)),
        out_shape=jax.ShapeDtypeStruct((m, d), F32),
        compiler_params=_params("parallel"),
        name="final_norm",
    )(x, g.reshape(1, d))


def _lane_lo(shape=(1, LANES)):
    return lax.broadcasted_iota(jnp.int32, shape, len(shape) - 1) < HEAD_DIM


def _rope_tables(n):
    half = HEAD_DIM // 2
    inv_freq = ROPE_THETA ** (-jnp.arange(half, dtype=F32) / half)
    ang = jnp.arange(n, dtype=F32)[:, None] * inv_freq[None, :]
    cos, sin = jnp.cos(ang), jnp.sin(ang)
    return (jnp.concatenate([cos, cos, cos, cos], axis=-1),
            jnp.concatenate([-sin, sin, -sin, sin], axis=-1))


def _rope_tile(x, cos, sin):
    lane = lax.broadcasted_iota(jnp.int32, (1, LANES), 1)
    first_half = (lane % HEAD_DIM) < (HEAD_DIM // 2)
    partner = jnp.where(first_half, pltpu.roll(x, LANES - HEAD_DIM // 2, 1),
                        pltpu.roll(x, HEAD_DIM // 2, 1))
    return x * cos + partner * sin


def _attn_a_kernel(q_ref, k_ref, v_ref, cos_ref, sin_ref, o_ref, qs, ks, m_s, l_s, acc_s):
    n = q_ref.shape[1]
    lo = _lane_lo()
    rb = 256

    def rope_body(i, carry):
        sl = pl.ds(pl.multiple_of(i * rb, rb), rb)
        c, s = cos_ref[sl, :], sin_ref[sl, :]
        qs[sl, :] = _rope_tile(q_ref[0, sl, :], c, s) * (HEAD_DIM ** -0.5)
        ks[sl, :] = _rope_tile(k_ref[0, sl, :], c, s)
        return carry

    lax.fori_loop(0, n // rb, rope_body, 0)
    m_s[...] = jnp.full(m_s.shape, MASKED, F32)
    l_s[...] = jnp.zeros(l_s.shape, F32)
    acc_s[...] = jnp.zeros(acc_s.shape, F32)

    for dil in A_DILATIONS:
        m = n // dil
        qb_rows = min(128, m)
        span = min(qb_rows + 2 * A_HALF, m)

        def block(r, qb, dil=dil, m=m, qb_rows=qb_rows, span=span):
            q0 = qb * qb_rows
            k0 = jnp.clip(q0 - A_HALF, 0, m - span)
            if dil == 1:
                rows_q = pl.ds(pl.multiple_of(q0, qb_rows), qb_rows)
                rows_k = pl.ds(pl.multiple_of(k0, A_HALF), span)
            else:
                rows_q = pl.ds(r + q0 * dil, qb_rows, stride=dil)
                rows_k = pl.ds(r + k0 * dil, span, stride=dil)
            q = qs[rows_q, :]
            kb = ks[rows_k, :].astype(BF16)
            vb = v_ref[0, rows_k, :].astype(BF16)
            qpos = q0 + lax.broadcasted_iota(jnp.int32, (qb_rows, 1), 0)
            kpos = k0 + lax.broadcasted_iota(jnp.int32, (1, span), 1)
            mask = jnp.abs(qpos - kpos) <= A_HALF
            s0 = jnp.where(mask, _dot_nt(jnp.where(lo, q, 0.0), kb), MASKED)
            s1 = jnp.where(mask, _dot_nt(jnp.where(lo, 0.0, q), kb), MASKED)
            m_old = m_s[rows_q, :]
            mn0 = jnp.maximum(m_old[:, :1], jnp.max(s0, axis=1, keepdims=True))
            mn1 = jnp.maximum(m_old[:, LANES - 1:], jnp.max(s1, axis=1, keepdims=True))
            p0 = jnp.exp(s0 - mn0)
            p1 = jnp.exp(s1 - mn1)
            m_new = jnp.where(lo, mn0, mn1)
            alpha = jnp.exp(m_old - m_new)
            psum = jnp.where(lo, jnp.sum(p0, axis=1, keepdims=True), jnp.sum(p1, axis=1, keepdims=True))
            pv = jnp.where(lo, _dot(p0, vb), _dot(p1, vb))
            m_s[rows_q, :] = m_new
            l_s[rows_q, :] = alpha * l_s[rows_q, :] + psum
            acc_s[rows_q, :] = alpha * acc_s[rows_q, :] + pv

        def res_body(r, carry, block=block, nqb=m // qb_rows):
            def qb_body(qb, c2):
                block(r, qb)
                return c2

            return lax.fori_loop(0, nqb, qb_body, carry)

        lax.fori_loop(0, dil, res_body, 0)

    def out_body(i, carry):
        sl = pl.ds(pl.multiple_of(i * rb, rb), rb)
        o_ref[0, sl, :] = (acc_s[sl, :] / l_s[sl, :]).astype(o_ref.dtype)
        return carry

    lax.fori_loop(0, n // rb, out_body, 0)


def _attn_a(za, cos, sin):
    b, n, _ = za.shape
    npair = A_W // LANES
    blk = lambda off: pl.BlockSpec((1, n, LANES), lambda i, j, off=off: (i, 0, off + j))
    tab = pl.BlockSpec((n, LANES), lambda i, j: (0, 0))
    return pl.pallas_call(
        _attn_a_kernel,
        grid=(b, npair),
        in_specs=[blk(0), blk(npair), blk(2 * npair), tab, tab],
        out_specs=pl.BlockSpec((1, n, LANES), lambda i, j: (i, 0, j)),
        out_shape=jax.ShapeDtypeStruct((b, n, A_W), BF16),
        scratch_shapes=[pltpu.VMEM((n, LANES), F32)] * 5,
        compiler_params=_params("parallel", "parallel"),
        name="attn_dilated",
    )(za, za, za, cos, sin)


def _attn_c_kernel(sink_ref, q_ref, k_ref, v_ref, cosq_ref, sinq_ref, cos_ref, sin_ref, o_ref,
                   klo, khi, vlo, vhi):
    n = k_ref.shape[1]
    qb_rows = q_ref.shape[1]
    span = qb_rows + 2 * C_HALF
    hk = pl.program_id(1)
    qb = pl.program_id(2)
    lo = _lane_lo()

    @pl.when(qb == 0)
    def _():
        rb = 256

        def body(i, carry):
            sl = pl.ds(pl.multiple_of(i * rb, rb), rb)
            kx = _rope_tile(k_ref[0, sl, :], cos_ref[sl, :], sin_ref[sl, :])
            kw = pltpu.roll(kx, HEAD_DIM, 1)
            vx = v_ref[0, sl, :]
            vw = pltpu.roll(vx, HEAD_DIM, 1)
            first = hk == 0
            klo[sl, :] = jnp.where(lo, jnp.where(first, kx, kw), 0.0).astype(BF16)
            khi[sl, :] = jnp.where(lo, 0.0, jnp.where(first, kw, kx)).astype(BF16)
            vlo[sl, :] = jnp.where(lo, jnp.where(first, vx, vw), 0.0).astype(BF16)
            vhi[sl, :] = jnp.where(lo, 0.0, jnp.where(first, vw, vx)).astype(BF16)
            return carry

        lax.fori_loop(0, n // rb, body, 0)

    q0 = qb * qb_rows
    k0 = pl.multiple_of(jnp.clip(q0 - C_HALF, 0, n - span), C_HALF)
    rows_k = pl.ds(k0, span)
    kl, kh = klo[rows_k, :], khi[rows_k, :]
    vl, vh = vlo[rows_k, :], vhi[rows_k, :]
    qpos = q0 + lax.broadcasted_iota(jnp.int32, (qb_rows, 1), 0)
    kpos = k0 + lax.broadcasted_iota(jnp.int32, (1, span), 1)
    mask = jnp.abs(qpos - kpos) <= C_HALF
    cq, sq = cosq_ref[...], sinq_ref[...]

    def probs(qt, kk, sk):
        s = jnp.where(mask, _dot_nt(qt, kk), MASKED)
        mx = jnp.maximum(jnp.max(s, axis=1, keepdims=True), sk)
        p = jnp.exp(s - mx)
        den = jnp.sum(p, axis=1, keepdims=True) + jnp.exp(sk - mx)
        return p, den

    for j in range(q_ref.shape[2] // LANES):
        qt = _rope_tile(q_ref[0, :, j * LANES:(j + 1) * LANES], cq, sq) * (HEAD_DIM ** -0.5)
        p_e, den_e = probs(qt, kl, sink_ref[hk * C_GROUP + 2 * j])
        p_o, den_o = probs(qt, kh, sink_ref[hk * C_GROUP + 2 * j + 1])
        pv = _dot(p_e, vl) + _dot(p_o, vh)
        o_ref[0, :, j * LANES:(j + 1) * LANES] = (pv / jnp.where(lo, den_e, den_o)).astype(o_ref.dtype)


def _attn_c(zc, sink, cos, sin, qb_rows=128):
    b, n, _ = zc.shape
    gw = C_GROUP * HEAD_DIM
    kcol = C_W // LANES
    full = lambda off: pl.BlockSpec((1, n, LANES), lambda i, h, t, off=off: (i, 0, off))
    tabq = pl.BlockSpec((qb_rows, LANES), lambda i, h, t: (t, 0))
    tab = pl.BlockSpec((n, LANES), lambda i, h, t: (0, 0))
    return pl.pallas_call(
        _attn_c_kernel,
        grid=(b, C_KV_HEADS, n // qb_rows),
        in_specs=[pl.BlockSpec(memory_space=pltpu.SMEM),
                  pl.BlockSpec((1, qb_rows, gw), lambda i, h, t: (i, t, h)),
                  full(kcol), full(kcol + 1), tabq, tabq, tab, tab],
        out_specs=pl.BlockSpec((1, qb_rows, gw), lambda i, h, t: (i, t, h)),
        out_shape=jax.ShapeDtypeStruct((b, n, C_W), BF16),
        scratch_shapes=[pltpu.VMEM((n, LANES), BF16)] * 4,
        compiler_params=_params("parallel", "parallel", "arbitrary"),
        name="attn_gqa_sink",
    )(sink.reshape(-1), zc, zc, zc, cos, sin, cos, sin)


def _na_bias_table(rpb, rows):
    kh = min(NA_KH, rows)
    qc = jnp.arange(GRID_W)[:, None]
    kc = jnp.arange(GRID_W)[None, :]
    qc0 = jnp.clip(qc - NA_KW // 2, 0, GRID_W - NA_KW)
    valid = (kc >= qc0) & (kc < qc0 + NA_KW)
    ci = jnp.clip(kc - qc, -(NA_KW - 1), NA_KW - 1) + NA_KW - 1
    by_row = jnp.stack([rpb.astype(F32)[:, off:off + kh] for off in range(NA_KH)], axis=1)
    onehot = (ci[None] == jnp.arange(2 * NA_KW - 1)[:, None, None]).astype(F32)
    bias = jnp.einsum('hokc,cqj->hoqkj', by_row, onehot, precision=lax.Precision.HIGHEST)
    bias = jnp.where(valid[None, None, :, None, :], bias, MASKED)
    bias = bias.reshape(D_HEADS // 2, 2, NA_KH, GRID_W, kh * GRID_W)
    return bias.transpose(0, 2, 1, 3, 4).reshape(D_HEADS // 2, NA_KH, 2 * GRID_W, kh * GRID_W)


def _attn_d_kernel(q_ref, k_ref, v_ref, bias_ref, o_ref):
    n = q_ref.shape[1]
    rows = n // GRID_W
    kh = min(NA_KH, rows)
    lo = _lane_lo()

    def body(r, carry):
        r0 = jnp.clip(r - NA_KH // 2, 0, rows - kh)
        rows_q = pl.ds(pl.multiple_of(r * GRID_W, GRID_W), GRID_W)
        rows_k = pl.ds(pl.multiple_of(r0 * GRID_W, GRID_W), kh * GRID_W)
        q = q_ref[0, rows_q, :] * (HEAD_DIM ** -0.5)
        q2 = jnp.concatenate([jnp.where(lo, q, 0.0), jnp.where(lo, 0.0, q)], axis=0)
        s = _dot_nt(q2, k_ref[0, rows_k, :]) + bias_ref[0, r0 - r + NA_KH - 1]
        p = jnp.exp(s - jnp.max(s, axis=1, keepdims=True))
        pv = _dot(p, v_ref[0, rows_k, :]) / jnp.sum(p, axis=1, keepdims=True)
        o_ref[0, rows_q, :] = jnp.where(lo, pv[:GRID_W], pv[GRID_W:]).astype(o_ref.dtype)
        return carry

    lax.fori_loop(0, rows, body, 0)


def _attn_d(zc, bias):
    b, n, _ = zc.shape
    npair = D_W // LANES
    q_off = (C_W + 2 * C_KV_HEADS * HEAD_DIM) // LANES
    blk = lambda off: pl.BlockSpec((1, n, LANES), lambda i, j, off=off: (i, 0, off + j))
    return pl.pallas_call(
        _attn_d_kernel,
        grid=(b, npair),
        in_specs=[blk(q_off), blk(q_off + npair), blk(q_off + 2 * npair),
                  pl.BlockSpec((1,) + bias.shape[1:], lambda i, j: (j, 0, 0, 0))],
        out_specs=pl.BlockSpec((1, n, LANES), lambda i, j: (i, 0, j)),
        out_shape=jax.ShapeDtypeStruct((b, n, D_W), BF16),
        compiler_params=_params("parallel", "parallel"),
        name="attn_neighborhood",
    )(zc, zc, zc, bias)


def _rw_prep_kernel(z_ref, hp_ref, hn_ref, mup_ref, mun_ref, w0_ref, w2f_ref, w2b_ref, a0_ref,
                    a2f_ref, a2b_ref, g2_ref, r_o, k_o, v_o, lwf_o, lwb_o, af_o, ab_o, g_o):
    tm = z_ref.shape[1]
    z = z_ref[0]
    row = lax.broadcasted_iota(jnp.int32, (tm, 1), 0)
    z_prev = jnp.where(row == 0, hp_ref[0, 0], pltpu.roll(z, 1, 0))
    z_next = jnp.where(row == tm - 1, hn_ref[0, 0], pltpu.roll(z, tm - 1, 0))
    z = z + mup_ref[...] * (z_prev - z) + mun_ref[...] * (z_next - z)
    base = 3 * B_W
    wl = jnp.tanh(z[:, base:base + LANES])
    al = z[:, base + LANES:base + 2 * LANES]
    gl = jax.nn.sigmoid(z[:, base + 2 * LANES:base + 4 * LANES])

    def log_decay(e, w2_ref):
        x = w0_ref[e:e + 1, :] + _dot(wl, w2_ref[...])
        w = -(jnp.maximum(-x, 0.0) + jnp.log1p(jnp.exp(-jnp.abs(x)))) - 0.5
        return -jnp.exp(w)

    def rate(e, a2_ref):
        return jax.nn.sigmoid(a0_ref[e:e + 1, :] + _dot(al, a2_ref[...]))

    outs = ((r_o, z[:, :B_W]), (k_o, z[:, B_W:2 * B_W]), (v_o, z[:, 2 * B_W:3 * B_W]),
            (lwf_o, log_decay(0, w2f_ref)), (lwb_o, log_decay(1, w2b_ref)),
            (af_o, rate(0, a2f_ref)), (ab_o, rate(1, a2b_ref)), (g_o, _dot(gl, g2_ref[...])))
    for ref, val in outs:
        for h in range(B_HEADS):
            ref[0, h] = val[:, h * HEAD_DIM:(h + 1) * HEAD_DIM]


def _rw_prep(zb, mu_prev, mu_next, w0, w2, a0, a2, g2, tm=128):
    b, n, cols = zb.shape
    nblk = n // tm
    zero = jnp.zeros((b, 1, cols), F32)
    halo_prev = jnp.concatenate([zero, zb[:, tm - 1:n - 1:tm]], axis=1).reshape(b, nblk, 1, cols)
    halo_next = jnp.concatenate([zb[:, tm::tm], zero], axis=1).reshape(b, nblk, 1, cols)
    pad_cols = lambda t: jnp.pad(t, (0, cols - t.shape[0])).reshape(1, cols)
    rows_f = lambda t: jnp.pad(t, ((0, LANES - RW_LORA), (0, 0))).astype(BF16)
    rows_b = lambda t: jnp.pad(t, ((RW_LORA, LANES - 2 * RW_LORA), (0, 0))).astype(BF16)
    g2p = jnp.pad(g2, ((0, 2 * LANES - RW_GATE_LORA), (0, 0))).astype(BF16)
    const = lambda shape: pl.BlockSpec(shape, lambda i, t: (0,) * len(shape))
    halo = pl.BlockSpec((1, 1, 1, cols), lambda i, t: (i, t, 0, 0))
    hm = jax.ShapeDtypeStruct((b, B_HEADS, n, HEAD_DIM), F32)
    return pl.pallas_call(
        _rw_prep_kernel,
        grid=(b, nblk),
        in_specs=[pl.BlockSpec((1, tm, cols), lambda i, t: (i, t, 0)), halo, halo,
                  const((1, cols)), const((1, cols)), const((2, B_W)), const((LANES, B_W)),
                  const((LANES, B_W)), const((2, B_W)), const((LANES, B_W)), const((LANES, B_W)),
                  const((2 * LANES, B_W))],
        out_specs=[pl.BlockSpec((1, B_HEADS, tm, HEAD_DIM), lambda i, t: (i, 0, t, 0))] * 8,
        out_shape=[hm] * 8,
        compiler_params=_params("parallel", "parallel"),
        name="rwkv_prep",
    )(zb, halo_prev, halo_next, pad_cols(mu_prev), pad_cols(mu_next), w0, rows_f(w2[0]), rows_b(w2[1]),
      a0, rows_f(a2[0]), rows_b(a2[1]), g2p)


def _wkv_chunks(chains):
    c = chains[0][0].shape[0]
    ii = lax.broadcasted_iota(jnp.int32, (c, c), 0)
    jj = lax.broadcasted_iota(jnp.int32, (c, c), 1)
    eye = ii == jj
    eye_f = eye.astype(F32)
    masks = {False: (jj <= ii, jj < ii), True: (jj >= ii, jj > ii)}
    each = lambda f, *cols: [f(*xs) for xs in zip(*cols)]
    r, k, v, lw, rate, st, k_k, k_a, rev = (list(col) for col in zip(*chains))
    incl = [masks[x][0] for x in rev]
    strict = [masks[x][1] for x in rev]

    kk = each(lambda k, k_k: k * k_k, k, k_k)
    kk = each(lambda t: t / jnp.maximum(jnp.sqrt(jnp.sum(t * t, axis=1, keepdims=True)), 1e-12), kk)
    kd = each(lambda k, rate, k_a: k * (1.0 + (rate - 1.0) * k_a), k, rate, k_a)
    b = each(lambda kk, rate: kk * rate, kk, rate)

    cum = each(lambda m, lw: jnp.dot(m.astype(F32), lw, precision=lax.Precision.HIGHEST,
                                     preferred_element_type=F32), incl, lw)
    tot = each(lambda cum, x: cum[0:1] if x else cum[c - 1:c], cum, rev)
    ar = each(lambda kk, r, cum, lw: jnp.concatenate([-kk * jnp.exp(cum - lw), r * jnp.exp(cum)], axis=0),
              kk, r, cum, lw)
    g_inv = each(lambda cum: jnp.exp(-cum), cum)
    x1 = each(lambda ar, b, g: _dot_nt(ar, b * g), ar, b, g_inv)
    x2 = each(lambda ar, kd, g: _dot_nt(ar, kd * g), ar, kd, g_inv)
    a_ab = each(lambda x, m: jnp.where(m, x[:c], 0.0), x1, strict)
    a_rb = each(lambda x, m: jnp.where(m, x[c:], 0.0), x1, incl)
    a_k = each(lambda x, ms, mi: jnp.concatenate([jnp.where(ms, x[:c], 0.0), jnp.where(mi, x[c:], 0.0)], axis=0),
               x2, strict, incl)

    inv = each(lambda t: eye_f + t, a_ab)
    pw = each(lambda t: _dot(t, t), a_ab)
    for _ in range(c.bit_length() - 3):
        both = each(lambda inv, pw: _dot(jnp.concatenate([inv, pw], axis=0), pw), inv, pw)
        inv = each(lambda inv, t: inv + t[:c], inv, both)
        pw = each(lambda t: t[c:], both)
    inv = each(lambda inv, pw: inv + _dot(inv, pw), inv, pw)

    akv = each(_dot, a_k, v)
    ars = each(_dot, ar, st)
    p = each(lambda inv, ars, akv: _dot(inv, ars[:c] + akv[:c]), inv, ars, akv)
    y = each(lambda ars, a_rb, p, akv: ars[c:] + _dot(a_rb, p) + akv[c:], ars, a_rb, p, akv)
    g_end = each(lambda tot, cum: jnp.exp(tot - cum), tot, cum)
    upd = each(lambda b, kd, g, p, v: _dot_tn(jnp.concatenate([b * g, kd * g], axis=0),
                                              jnp.concatenate([p, v], axis=0)), b, kd, g_end, p, v)
    g_col = each(lambda tot: jnp.sum(jnp.where(eye, jnp.exp(tot), 0.0), axis=1, keepdims=True), tot)
    new_st = each(lambda g, st, upd: g * st + upd, g_col, st, upd)
    return list(zip(y, new_st))


def _rw_scan_kernel(rf, kf, vf, lwf, af, rb, kb, vb, lwb, ab, kk_ref, ka_ref, yf, yb, sf, sb):
    hb, tb = rf.shape[1], rf.shape[2]
    c = min(WKV_CHUNK, tb)
    nch = tb // c

    @pl.when(pl.program_id(2) == 0)
    def _():
        sf[...] = jnp.zeros(sf.shape, F32)
        sb[...] = jnp.zeros(sb.shape, F32)

    def body(ci, carry):
        rows_f = pl.ds(pl.multiple_of(ci * c, c), c)
        rows_b = pl.ds(pl.multiple_of((nch - 1 - ci) * c, c), c)
        chains = []
        for h in range(hb):
            chains.append((rf[0, h, rows_f, :], kf[0, h, rows_f, :], vf[0, h, rows_f, :], lwf[0, h, rows_f, :],
                           af[0, h, rows_f, :], sf[h], kk_ref[h], ka_ref[h], False))
            chains.append((rb[0, h, rows_b, :], kb[0, h, rows_b, :], vb[0, h, rows_b, :], lwb[0, h, rows_b, :],
                           ab[0, h, rows_b, :], sb[h], kk_ref[h], ka_ref[h], True))
        outs = _wkv_chunks(chains)
        for h in range(hb):
            yf[0, h, rows_f, :], sf[h] = outs[2 * h]
            yb[0, h, rows_b, :], sb[h] = outs[2 * h + 1]
        return carry

    lax.fori_loop(0, nch, body, 0)


def _rw_scan(r, k, v, lwf, lwb, af, ab, k_k, k_a, hb=5, tb=256):
    b, nh, n, hd = r.shape
    tb = min(tb, n)
    nt = n // tb
    fwd = pl.BlockSpec((1, hb, tb, hd), lambda i, g, t: (i, g, t, 0))
    bwd = pl.BlockSpec((1, hb, tb, hd), lambda i, g, t: (i, g, nt - 1 - t, 0))
    par = pl.BlockSpec((hb, 1, hd), lambda i, g, t: (g, 0, 0))
    out = jax.ShapeDtypeStruct((b, nh, n, hd), F32)
    return pl.pallas_call(
        _rw_scan_kernel,
        grid=(b, nh // hb, nt),
        in_specs=[fwd] * 5 + [bwd] * 5 + [par, par],
        out_specs=[fwd, bwd],
        out_shape=[out, out],
        scratch_shapes=[pltpu.VMEM((hb, hd, hd), F32)] * 2,
        compiler_params=_params("parallel", "parallel", "arbitrary"),
        name="rwkv_scan",
    )(r, k, v, lwf, af, r, k, v, lwb, ab, k_k.reshape(nh, 1, hd), k_a.reshape(nh, 1, hd))


def _rw_post_kernel(yf, yb, r, k, v, af, ab, g, ka_ref, rk_ref, lnw_ref, lnb_ref, o_ref):
    outs = []
    for h in range(B_HEADS):
        y = yf[0, h] + yb[0, h]
        mean = jnp.mean(y, axis=1, keepdims=True)
        var = jnp.mean(jnp.square(y - mean), axis=1, keepdims=True)
        yn = (y - mean) * lax.rsqrt(var + RW_GN_EPS) * lnw_ref[h] + lnb_ref[h]
        kh, k_a = k[0, h], ka_ref[h]
        kd = kh * (1.0 + (af[0, h] - 1.0) * k_a) + kh * (1.0 + (ab[0, h] - 1.0) * k_a)
        bonus = jnp.sum(r[0, h] * kd * rk_ref[h], axis=1, keepdims=True) * v[0, h]
        outs.append((yn + bonus) * g[0, h])
    o_ref[0] = jnp.concatenate(outs, axis=1).astype(o_ref.dtype)


def _rw_post(yf, yb, r, k, v, af, ab, g, k_a, r_k, ln_w, ln_b, tm=128):
    b, nh, n, hd = r.shape
    tm = min(tm, n)
    hm = pl.BlockSpec((1, nh, tm, hd), lambda i, t: (i, 0, t, 0))
    par = pl.BlockSpec((nh, 1, hd), lambda i, t: (0, 0, 0))
    return pl.pallas_call(
        _rw_post_kernel,
        grid=(b, n // tm),
        in_specs=[hm] * 8 + [par] * 4,
        out_specs=pl.BlockSpec((1, tm, nh * hd), lambda i, t: (i, t, 0)),
        out_shape=jax.ShapeDtypeStruct((b, n, nh * hd), BF16),
        compiler_params=_params("parallel", "parallel"),
        name="rwkv_post",
    )(yf, yb, r, k, v, af, ab, g, k_a.reshape(nh, 1, hd), r_k.reshape(nh, 1, hd),
      ln_w.reshape(nh, 1, hd), ln_b.reshape(nh, 1, hd))


def _rwkv7_bidir(zb, mu_prev, mu_next, w0, w2, a0, a2, g2, k_k, k_a, r_k, ln_w, ln_b):
    r, k, v, lwf, lwb, af, ab, g = _rw_prep(zb, mu_prev, mu_next, w0, w2, a0, a2, g2)
    yf, yb = _rw_scan(r, k, v, lwf, lwb, af, ab, k_k, k_a)
    return _rw_post(yf, yb, r, k, v, af, ab, g, k_a, r_k, ln_w, ln_b)


def _ffn_half_step(x, g, w1, w3, w2):
    h = _ffn_up(x, g, w1.astype(BF16), w3.astype(BF16))
    return _mm_res([(h, w2.astype(BF16))], x, 0.5)


def _mix_ab(x, g, b, n, cos, sin, w_in, w_out, mu_prev, mu_next, w0, w2, a0, a2, g2, k_k, k_a, r_k,
            ln_w, ln_b):
    qkv = 3 * A_W
    w_a = w_in[:, :qkv].astype(BF16)
    w_b = jnp.pad(w_in[:, qkv:], ((0, 0), (0, RW_PAD_COLS - (w_in.shape[1] - qkv)))).astype(BF16)
    za = _norm_proj(x, g, w_a, tn=qkv // 3).reshape(b, n, qkv)
    zb = _norm_proj(x, g, w_b, tn=RW_PAD_COLS // 2).reshape(b, n, RW_PAD_COLS)
    oa = _attn_a(za, cos, sin).reshape(b * n, A_W)
    ob = _rwkv7_bidir(zb, mu_prev, mu_next, w0, w2, a0, a2, g2, k_k, k_a, r_k, ln_w, ln_b)
    w_out = w_out.astype(BF16)
    return _mm_res([(oa, w_out[:A_W]), (ob.reshape(b * n, B_W), w_out[A_W:])], x, 1.0)


def _mix_cd(x, g, b, n, cos, sin, w_in, w_out, sink, rpb):
    cols = w_in.shape[1]
    zc = _norm_proj(x, g, w_in.astype(BF16), tn=cols // 2).reshape(b, n, cols)
    oc = _attn_c(zc, sink, cos, sin).reshape(b * n, C_W)
    od = _attn_d(zc, _na_bias_table(rpb, n // GRID_W)).reshape(b * n, D_W)
    w_out = w_out.astype(BF16)
    return _mm_res([(oc, w_out[:C_W]), (od, w_out[C_W:])], x, 1.0)


def kernel(x, p, ffn1_norm, ffn1_w1, ffn1_w3, ffn1_w2, mix_norm, ffn2_norm, ffn2_w1, ffn2_w3, ffn2_w2, ple_norm, ple_w_gate, ple_w_proj, ab_w_in, ab_w_out, rw_mu_prev, rw_mu_next, rw_w0, rw_w2, rw_a0, rw_a2, rw_g2, rw_k_k, rw_k_a, rw_r_k, rw_ln_w, rw_ln_b, cd_w_in, cd_w_out, c_sink, d_rpb, final_norm):
    b, n, d = x.shape
    depth = p.shape[0]
    cos, sin = _rope_tables(n)
    x = x.reshape(b * n, d)
    for i in range(depth):
        j = i // 2
        x = _ffn_half_step(x, ffn1_norm[i], ffn1_w1[i], ffn1_w3[i], ffn1_w2[i])
        if i % 2 == 0:
            x = _mix_ab(x, mix_norm[i], b, n, cos, sin, ab_w_in[j], ab_w_out[j], rw_mu_prev[j],
                        rw_mu_next[j], rw_w0[j], rw_w2[j], rw_a0[j], rw_a2[j], rw_g2[j], rw_k_k[j],
                        rw_k_a[j], rw_r_k[j], rw_ln_w[j], rw_ln_b[j])
        else:
            x = _mix_cd(x, mix_norm[i], b, n, cos, sin, cd_w_in[j], cd_w_out[j], c_sink[j], d_rpb[j])
        x = _ffn_half_step(x, ffn2_norm[i], ffn2_w1[i], ffn2_w3[i], ffn2_w2[i])
        x = _ple(x, ple_norm[i], ple_w_gate[i].astype(BF16), p[i].reshape(b * n, -1),
                 ple_w_proj[i].astype(BF16))
    return _final_norm(x, final_norm).reshape(b, n, d)
```

```python
import functools

import jax
import jax.numpy as jnp
from jax import lax
from jax.experimental import pallas as pl
from jax.experimental.pallas import tpu as pltpu

F32 = jnp.float32
BF16 = jnp.bfloat16

HEAD_DIM = 64
LANES = 128
NORM_EPS = 1e-6
ROPE_THETA = 10000.0
GRID_W = 64
A_HEADS = 12
A_W = A_HEADS * HEAD_DIM
A_HALF = 64
A_DILATIONS = (1, 4, 16)
B_HEADS = 20
B_W = B_HEADS * HEAD_DIM
RW_LORA = 64
RW_GATE_LORA = 192
RW_GN_EPS = 64e-5
RW_PAD_COLS = 3 * B_W + 4 * LANES
C_HEADS = 16
C_KV_HEADS = 2
C_GROUP = C_HEADS // C_KV_HEADS
C_W = C_HEADS * HEAD_DIM
C_HALF = 128
D_HEADS = 16
D_W = D_HEADS * HEAD_DIM
NA_KH = 8
NA_KW = 16
MASKED = -1e30
WKV_CHUNK = 64
VMEM_LIMIT = 56 * 1024 * 1024


def _params(*sem):
    return pltpu.CompilerParams(dimension_semantics=sem, vmem_limit_bytes=VMEM_LIMIT)


def _dot(a, b):
    return jnp.dot(a.astype(BF16), b.astype(BF16), preferred_element_type=F32)


def _dot_nt(a, b):
    return lax.dot_general(a.astype(BF16), b.astype(BF16), (((1,), (1,)), ((), ())),
                           preferred_element_type=F32)


def _dot_tn(a, b):
    return lax.dot_general(a.astype(BF16), b.astype(BF16), (((0,), (0,)), ((), ())),
                           preferred_element_type=F32)


def _each(f, *cols):
    return [f(*xs) for xs in zip(*cols)]


def _rmsnorm_kernel(x_ref, g_ref, o_ref):
    x = x_ref[...]
    ms = jnp.mean(x * x, axis=-1, keepdims=True)
    o_ref[...] = (x * lax.rsqrt(ms + NORM_EPS) * g_ref[...]).astype(o_ref.dtype)


def _rmsnorm(x, g, dtype, tm=256):
    m, d = x.shape
    return pl.pallas_call(
        _rmsnorm_kernel,
        grid=(m // tm,),
        in_specs=[pl.BlockSpec((tm, d), lambda i: (i, 0)),
                  pl.BlockSpec((1, d), lambda i: (0, 0))],
        out_specs=pl.BlockSpec((tm, d), lambda i: (i, 0)),
        out_shape=jax.ShapeDtypeStruct((m, d), dtype),
        compiler_params=_params("parallel"),
        name="rmsnorm",
    )(x, g.reshape(1, d))


def _cast_weight(w_ref, wb_ref):
    rows = 256

    @pl.when(pl.program_id(1) == 0)
    def _():
        def body(i, carry):
            sl = pl.ds(pl.multiple_of(i * rows, rows), rows)
            wb_ref[sl, :] = w_ref[sl, :].astype(BF16)
            return carry

        lax.fori_loop(0, w_ref.shape[0] // rows, body, 0)


def _row_spec(tm, k):
    return pl.BlockSpec((tm, k), lambda j, i: (i, 0))


def _col_spec(k, tn, **kw):
    return pl.BlockSpec((k, tn), lambda j, i: (0, j), **kw)


def _tile_spec(tm, tn):
    return pl.BlockSpec((tm, tn), lambda j, i: (i, j))


def _ffn_up_kernel(xn_ref, w1_ref, w3_ref, o_ref, w1b, w3b):
    _cast_weight(w1_ref, w1b)
    _cast_weight(w3_ref, w3b)
    xn = xn_ref[...]
    a = jnp.dot(xn, w1b[...], preferred_element_type=F32)
    b = jnp.dot(xn, w3b[...], preferred_element_type=F32)
    o_ref[...] = (a * jax.nn.sigmoid(a) * b).astype(o_ref.dtype)


def _ffn_up(xn, w1, w3, tm=1024, tn=512):
    m, d = xn.shape
    f = w1.shape[1]
    return pl.pallas_call(
        _ffn_up_kernel,
        grid=(f // tn, m // tm),
        in_specs=[_row_spec(tm, d), _col_spec(d, tn), _col_spec(d, tn)],
        out_specs=_tile_spec(tm, tn),
        out_shape=jax.ShapeDtypeStruct((m, f), BF16),
        scratch_shapes=[pltpu.VMEM((d, tn), BF16)] * 2,
        compiler_params=_params("parallel", "arbitrary"),
        name="ffn_up",
    )(xn, w1, w3)


def _proj_kernel(xn_ref, w_ref, o_ref, wb):
    _cast_weight(w_ref, wb)
    o_ref[...] = jnp.dot(xn_ref[...], wb[...], preferred_element_type=F32)


def _proj(xn, w, ncols, tn, tm=512):
    m, d = xn.shape
    return pl.pallas_call(
        _proj_kernel,
        grid=(ncols // tn, m // tm),
        in_specs=[_row_spec(tm, d), _col_spec(d, tn, pipeline_mode=pl.Buffered(1))],
        out_specs=_tile_spec(tm, tn),
        out_shape=jax.ShapeDtypeStruct((m, ncols), F32),
        scratch_shapes=[pltpu.VMEM((d, tn), BF16)],
        compiler_params=_params("parallel", "arbitrary"),
        name="proj",
    )(xn, w)


def _ple_kernel(xn_ref, w_ref, x_ref, p_ref, wp_ref, o_ref, wb, wpb):
    _cast_weight(w_ref, wb)
    _cast_weight(wp_ref, wpb)
    gate = jax.nn.sigmoid(jnp.dot(xn_ref[...], wb[...], preferred_element_type=F32))
    emb = jnp.dot(p_ref[...].astype(BF16), wpb[...], preferred_element_type=F32)
    o_ref[...] = x_ref[...] + emb * gate


def _ple(x, xn, w_gate, p, w_proj, tm=512, tn=512):
    m, d = x.shape
    pd = p.shape[1]
    return pl.pallas_call(
        _ple_kernel,
        grid=(d // tn, m // tm),
        in_specs=[_row_spec(tm, d), _col_spec(d, tn), _tile_spec(tm, tn), _row_spec(tm, pd),
                  _col_spec(pd, tn)],
        out_specs=_tile_spec(tm, tn),
        out_shape=jax.ShapeDtypeStruct((m, d), F32),
        scratch_shapes=[pltpu.VMEM((d, tn), BF16), pltpu.VMEM((pd, tn), BF16)],
        compiler_params=_params("parallel", "arbitrary"),
        name="ple",
    )(xn, w_gate, x, p, w_proj)


def _mm_res_kernel(*refs, n_pairs, scale):
    res_ref, o_ref = refs[2 * n_pairs], refs[2 * n_pairs + 1]
    scratch = refs[2 * n_pairs + 2:]
    acc = None
    for t in range(n_pairs):
        _cast_weight(refs[2 * t + 1], scratch[t])
        part = jnp.dot(refs[2 * t][...], scratch[t][...], preferred_element_type=F32)
        acc = part if acc is None else acc + part
    o_ref[...] = res_ref[...] + scale * acc


def _mm_res(pairs, res, scale, tm=512, tn=512):
    m, n = res.shape
    in_specs, args, scratch = [], [], []
    for a, w in pairs:
        k = a.shape[1]
        in_specs += [_row_spec(tm, k), _col_spec(k, tn)]
        args += [a, w]
        scratch.append(pltpu.VMEM((k, tn), BF16))
    in_specs.append(_tile_spec(tm, tn))
    return pl.pallas_call(
        functools.partial(_mm_res_kernel, n_pairs=len(pairs), scale=scale),
        grid=(n // tn, m // tm),
        in_specs=in_specs,
        out_specs=_tile_spec(tm, tn),
        out_shape=jax.ShapeDtypeStruct((m, n), F32),
        scratch_shapes=scratch,
        compiler_params=_params("parallel", "arbitrary"),
        name="mm_res",
    )(*args, res)


def _lane_lo(shape=(1, LANES)):
    return lax.broadcasted_iota(jnp.int32, shape, len(shape) - 1) < HEAD_DIM


def _rope_tables(n):
    half = HEAD_DIM // 2
    inv_freq = ROPE_THETA ** (-jnp.arange(half, dtype=F32) / half)
    ang = jnp.arange(n, dtype=F32)[:, None] * inv_freq[None, :]
    cos, sin = jnp.cos(ang), jnp.sin(ang)
    return (jnp.concatenate([cos, cos, cos, cos], axis=-1),
            jnp.concatenate([-sin, sin, -sin, sin], axis=-1))


def _rope_tile(x, cos, sin):
    lane = lax.broadcasted_iota(jnp.int32, (1, LANES), 1)
    first_half = (lane % HEAD_DIM) < (HEAD_DIM // 2)
    partner = jnp.where(first_half, pltpu.roll(x, LANES - HEAD_DIM // 2, 1),
                        pltpu.roll(x, HEAD_DIM // 2, 1))
    return x * cos + partner * sin


def _attn_a_kernel(q_ref, k_ref, v_ref, cos_ref, sin_ref, o_ref, qs, ks, m_s, l_s, acc_s):
    n = q_ref.shape[1]
    lo = _lane_lo()
    rb = 256

    def rope_body(i, carry):
        sl = pl.ds(pl.multiple_of(i * rb, rb), rb)
        c, s = cos_ref[sl, :], sin_ref[sl, :]
        qs[sl, :] = _rope_tile(q_ref[0, sl, :], c, s) * (HEAD_DIM ** -0.5)
        ks[sl, :] = _rope_tile(k_ref[0, sl, :], c, s)
        return carry

    lax.fori_loop(0, n // rb, rope_body, 0)
    m_s[...] = jnp.full(m_s.shape, MASKED, F32)
    l_s[...] = jnp.zeros(l_s.shape, F32)
    acc_s[...] = jnp.zeros(acc_s.shape, F32)

    group = 4

    for dil in A_DILATIONS:
        m = n // dil
        qb_rows = min(128, m)
        span = min(qb_rows + 2 * A_HALF, m)
        nqb = m // qb_rows

        def blocks(items, dil=dil, m=m, qb_rows=qb_rows, span=span):
            q0 = [qb * qb_rows for _, qb in items]
            k0 = [jnp.clip(t - A_HALF, 0, m - span) for t in q0]
            if dil == 1:
                rows_q = [pl.ds(pl.multiple_of(t, qb_rows), qb_rows) for t in q0]
                rows_k = [pl.ds(pl.multiple_of(t, A_HALF), span) for t in k0]
            else:
                rows_q = [pl.ds(r + t * dil, qb_rows, stride=dil) for (r, _), t in zip(items, q0)]
                rows_k = [pl.ds(r + t * dil, span, stride=dil) for (r, _), t in zip(items, k0)]
            q = [qs[rq, :] for rq in rows_q]
            kb = [ks[rk, :].astype(BF16) for rk in rows_k]
            vb = [v_ref[0, rk, :].astype(BF16) for rk in rows_k]
            m_old = [m_s[rq, :] for rq in rows_q]
            l_old = [l_s[rq, :] for rq in rows_q]
            acc_old = [acc_s[rq, :] for rq in rows_q]
            mask = _each(lambda a, c: jnp.abs(a + lax.broadcasted_iota(jnp.int32, (qb_rows, 1), 0)
                                              - c - lax.broadcasted_iota(jnp.int32, (1, span), 1)) <= A_HALF,
                         q0, k0)
            s0 = _each(lambda q, kb, mk: jnp.where(mk, _dot_nt(jnp.where(lo, q, 0.0), kb), MASKED), q, kb, mask)
            s1 = _each(lambda q, kb, mk: jnp.where(mk, _dot_nt(jnp.where(lo, 0.0, q), kb), MASKED), q, kb, mask)
            mn0 = _each(lambda mo, s: jnp.maximum(mo[:, :1], jnp.max(s, axis=1, keepdims=True)), m_old, s0)
            mn1 = _each(lambda mo, s: jnp.maximum(mo[:, LANES - 1:], jnp.max(s, axis=1, keepdims=True)), m_old, s1)
            p0 = _each(lambda s, mn: jnp.exp(s - mn), s0, mn0)
            p1 = _each(lambda s, mn: jnp.exp(s - mn), s1, mn1)
            m_new = _each(lambda a, c: jnp.where(lo, a, c), mn0, mn1)
            alpha = _each(lambda mo, mn: jnp.exp(mo - mn), m_old, m_new)
            psum = _each(lambda a, c: jnp.where(lo, jnp.sum(a, axis=1, keepdims=True),
                                                jnp.sum(c, axis=1, keepdims=True)), p0, p1)
            pv = _each(lambda a, c, vb: jnp.where(lo, _dot(a, vb), _dot(c, vb)), p0, p1, vb)
            for rq, mn, al, lold, ps, ao, pvi in zip(rows_q, m_new, alpha, l_old, psum, acc_old, pv):
                m_s[rq, :] = mn
                l_s[rq, :] = al * lold + ps
                acc_s[rq, :] = al * ao + pvi

        if dil >= group:
            def res_body(rg, carry, blocks=blocks, nqb=nqb):
                def qb_body(qb, c2):
                    blocks([(rg * group + u, qb) for u in range(group)])
                    return c2

                return lax.fori_loop(0, nqb, qb_body, carry)

            lax.fori_loop(0, dil // group, res_body, 0)
        else:
            def qb_body(qg, carry, blocks=blocks):
                blocks([(0, qg * group + u) for u in range(group)])
                return carry

            lax.fori_loop(0, nqb // group, qb_body, 0)

    def out_body(i, carry):
        sl = pl.ds(pl.multiple_of(i * rb, rb), rb)
        o_ref[0, sl, :] = (acc_s[sl, :] / l_s[sl, :]).astype(o_ref.dtype)
        return carry

    lax.fori_loop(0, n // rb, out_body, 0)


def _attn_a(za, cos, sin):
    b, n, _ = za.shape
    npair = A_W // LANES
    blk = lambda off: pl.BlockSpec((1, n, LANES), lambda i, j, off=off: (i, 0, off + j))
    tab = pl.BlockSpec((n, LANES), lambda i, j: (0, 0))
    return pl.pallas_call(
        _attn_a_kernel,
        grid=(b, npair),
        in_specs=[blk(0), blk(npair), blk(2 * npair), tab, tab],
        out_specs=pl.BlockSpec((1, n, LANES), lambda i, j: (i, 0, j)),
        out_shape=jax.ShapeDtypeStruct((b, n, A_W), BF16),
        scratch_shapes=[pltpu.VMEM((n, LANES), F32)] * 5,
        compiler_params=_params("parallel", "parallel"),
        name="attn_dilated",
    )(za, za, za, cos, sin)


def _attn_c_kernel(sink_ref, q_ref, k_ref, v_ref, cosq_ref, sinq_ref, cos_ref, sin_ref, o_ref,
                   klo, khi, vlo, vhi):
    n = k_ref.shape[1]
    qb_rows = q_ref.shape[1]
    span = qb_rows + 2 * C_HALF
    hk = pl.program_id(1)
    qb = pl.program_id(2)
    lo = _lane_lo()

    @pl.when(qb == 0)
    def _():
        rb = 256

        def body(i, carry):
            sl = pl.ds(pl.multiple_of(i * rb, rb), rb)
            kx = _rope_tile(k_ref[0, sl, :], cos_ref[sl, :], sin_ref[sl, :])
            kw = pltpu.roll(kx, HEAD_DIM, 1)
            vx = v_ref[0, sl, :]
            vw = pltpu.roll(vx, HEAD_DIM, 1)
            first = hk == 0
            klo[sl, :] = jnp.where(lo, jnp.where(first, kx, kw), 0.0).astype(BF16)
            khi[sl, :] = jnp.where(lo, 0.0, jnp.where(first, kw, kx)).astype(BF16)
            vlo[sl, :] = jnp.where(lo, jnp.where(first, vx, vw), 0.0).astype(BF16)
            vhi[sl, :] = jnp.where(lo, 0.0, jnp.where(first, vw, vx)).astype(BF16)
            return carry

        lax.fori_loop(0, n // rb, body, 0)

    q0 = qb * qb_rows
    k0 = pl.multiple_of(jnp.clip(q0 - C_HALF, 0, n - span), C_HALF)
    rows_k = pl.ds(k0, span)
    kl, kh = klo[rows_k, :], khi[rows_k, :]
    vl, vh = vlo[rows_k, :], vhi[rows_k, :]
    qpos = q0 + lax.broadcasted_iota(jnp.int32, (qb_rows, 1), 0)
    kpos = k0 + lax.broadcasted_iota(jnp.int32, (1, span), 1)
    mask = jnp.abs(qpos - kpos) <= C_HALF
    cq, sq = cosq_ref[...], sinq_ref[...]

    def probs(qt, kk, sk):
        s = jnp.where(mask, _dot_nt(qt, kk), MASKED)
        mx = jnp.maximum(jnp.max(s, axis=1, keepdims=True), sk)
        p = jnp.exp(s - mx)
        den = jnp.sum(p, axis=1, keepdims=True) + jnp.exp(sk - mx)
        return p, den

    for j in range(q_ref.shape[2] // LANES):
        qt = _rope_tile(q_ref[0, :, j * LANES:(j + 1) * LANES], cq, sq) * (HEAD_DIM ** -0.5)
        p_e, den_e = probs(qt, kl, sink_ref[hk * C_GROUP + 2 * j])
        p_o, den_o = probs(qt, kh, sink_ref[hk * C_GROUP + 2 * j + 1])
        pv = _dot(p_e, vl) + _dot(p_o, vh)
        o_ref[0, :, j * LANES:(j + 1) * LANES] = (pv / jnp.where(lo, den_e, den_o)).astype(o_ref.dtype)


def _attn_c(zc, sink, cos, sin, qb_rows=128):
    b, n, _ = zc.shape
    gw = C_GROUP * HEAD_DIM
    kcol = C_W // LANES
    full = lambda off: pl.BlockSpec((1, n, LANES), lambda i, h, t, off=off: (i, 0, off))
    tabq = pl.BlockSpec((qb_rows, LANES), lambda i, h, t: (t, 0))
    tab = pl.BlockSpec((n, LANES), lambda i, h, t: (0, 0))
    return pl.pallas_call(
        _attn_c_kernel,
        grid=(b, C_KV_HEADS, n // qb_rows),
        in_specs=[pl.BlockSpec(memory_space=pltpu.SMEM),
                  pl.BlockSpec((1, qb_rows, gw), lambda i, h, t: (i, t, h)),
                  full(kcol), full(kcol + 1), tabq, tabq, tab, tab],
        out_specs=pl.BlockSpec((1, qb_rows, gw), lambda i, h, t: (i, t, h)),
        out_shape=jax.ShapeDtypeStruct((b, n, C_W), BF16),
        scratch_shapes=[pltpu.VMEM((n, LANES), BF16)] * 4,
        compiler_params=_params("parallel", "parallel", "arbitrary"),
        name="attn_gqa_sink",
    )(sink.reshape(-1), zc, zc, zc, cos, sin, cos, sin)


def _na_bias_table(rpb, rows):
    kh = min(NA_KH, rows)
    qc = jnp.arange(GRID_W)[:, None]
    kc = jnp.arange(GRID_W)[None, :]
    qc0 = jnp.clip(qc - NA_KW // 2, 0, GRID_W - NA_KW)
    valid = (kc >= qc0) & (kc < qc0 + NA_KW)
    ci = jnp.clip(kc - qc, -(NA_KW - 1), NA_KW - 1) + NA_KW - 1
    by_row = jnp.stack([rpb.astype(F32)[:, off:off + kh] for off in range(NA_KH)], axis=1)
    onehot = (ci[None] == jnp.arange(2 * NA_KW - 1)[:, None, None]).astype(F32)
    bias = jnp.einsum('hokc,cqj->hoqkj', by_row, onehot, precision=lax.Precision.HIGHEST)
    bias = jnp.where(valid[None, None, :, None, :], bias, MASKED)
    bias = bias.reshape(D_HEADS // 2, 2, NA_KH, GRID_W, kh * GRID_W)
    return bias.transpose(0, 2, 1, 3, 4).reshape(D_HEADS // 2, NA_KH, 2 * GRID_W, kh * GRID_W)


def _attn_d_kernel(q_ref, k_ref, v_ref, bias_ref, o_ref):
    n = q_ref.shape[1]
    rows = n // GRID_W
    kh = min(NA_KH, rows)
    lo = _lane_lo()
    group = 4 if rows % 4 == 0 else 1

    def body(rg, carry):
        r = [rg * group + u for u in range(group)]
        r0 = [jnp.clip(t - NA_KH // 2, 0, rows - kh) for t in r]
        rows_q = [pl.ds(pl.multiple_of(t * GRID_W, GRID_W), GRID_W) for t in r]
        rows_k = [pl.ds(pl.multiple_of(t * GRID_W, GRID_W), kh * GRID_W) for t in r0]
        q = [q_ref[0, rq, :] * (HEAD_DIM ** -0.5) for rq in rows_q]
        q2 = _each(lambda q: jnp.concatenate([jnp.where(lo, q, 0.0), jnp.where(lo, 0.0, q)], axis=0), q)
        s = [_dot_nt(q2i, k_ref[0, rk, :]) + bias_ref[0, t0 - t + NA_KH - 1]
             for q2i, rk, t0, t in zip(q2, rows_k, r0, r)]
        p = _each(lambda s: jnp.exp(s - jnp.max(s, axis=1, keepdims=True)), s)
        pv = [_dot(pi, v_ref[0, rk, :]) / jnp.sum(pi, axis=1, keepdims=True) for pi, rk in zip(p, rows_k)]
        for rq, pvi in zip(rows_q, pv):
            o_ref[0, rq, :] = jnp.where(lo, pvi[:GRID_W], pvi[GRID_W:]).astype(o_ref.dtype)
        return carry

    lax.fori_loop(0, rows // group, body, 0)


def _attn_d(zc, bias):
    b, n, _ = zc.shape
    npair = D_W // LANES
    q_off = (C_W + 2 * C_KV_HEADS * HEAD_DIM) // LANES
    blk = lambda off: pl.BlockSpec((1, n, LANES), lambda i, j, off=off: (i, 0, off + j))
    return pl.pallas_call(
        _attn_d_kernel,
        grid=(b, npair),
        in_specs=[blk(q_off), blk(q_off + npair), blk(q_off + 2 * npair),
                  pl.BlockSpec((1,) + bias.shape[1:], lambda i, j: (j, 0, 0, 0))],
        out_specs=pl.BlockSpec((1, n, LANES), lambda i, j: (i, 0, j)),
        out_shape=jax.ShapeDtypeStruct((b, n, D_W), BF16),
        compiler_params=_params("parallel", "parallel"),
        name="attn_neighborhood",
    )(zc, zc, zc, bias)


def _rw_prep_kernel(z_ref, hp_ref, hn_ref, mup_ref, mun_ref, w0_ref, w2f_ref, w2b_ref, a0_ref,
                    a2f_ref, a2b_ref, g2_ref, r_o, k_o, v_o, cumf_o, cumb_o, af_o, ab_o, g_o):
    tm = z_ref.shape[1]
    z = z_ref[0]
    row = lax.broadcasted_iota(jnp.int32, (tm, 1), 0)
    z_prev = jnp.where(row == 0, hp_ref[0, 0], pltpu.roll(z, 1, 0))
    z_next = jnp.where(row == tm - 1, hn_ref[0, 0], pltpu.roll(z, tm - 1, 0))
    z = z + mup_ref[...] * (z_prev - z) + mun_ref[...] * (z_next - z)
    base = 3 * B_W
    wl = jnp.tanh(z[:, base:base + LANES])
    al = z[:, base + LANES:base + 2 * LANES]
    gl = jax.nn.sigmoid(z[:, base + 2 * LANES:base + 4 * LANES])

    def log_decay(e, w2_ref):
        x = w0_ref[e:e + 1, :] + _dot(wl, w2_ref[...])
        w = -(jnp.maximum(-x, 0.0) + jnp.log1p(jnp.exp(-jnp.abs(x)))) - 0.5
        return -jnp.exp(w)

    def rate(e, a2_ref):
        return jax.nn.sigmoid(a0_ref[e:e + 1, :] + _dot(al, a2_ref[...]))

    crow = row % WKV_CHUNK

    def chunk_cumsum(x, rev):
        s = 1
        while s < WKV_CHUNK:
            if rev:
                x = x + jnp.where(crow < WKV_CHUNK - s, pltpu.roll(x, tm - s, 0), 0.0)
            else:
                x = x + jnp.where(crow >= s, pltpu.roll(x, s, 0), 0.0)
            s *= 2
        return x

    outs = ((r_o, z[:, :B_W]), (k_o, z[:, B_W:2 * B_W]), (v_o, z[:, 2 * B_W:3 * B_W]),
            (cumf_o, chunk_cumsum(log_decay(0, w2f_ref), False)),
            (cumb_o, chunk_cumsum(log_decay(1, w2b_ref), True)),
            (af_o, rate(0, a2f_ref)), (ab_o, rate(1, a2b_ref)), (g_o, _dot(gl, g2_ref[...])))
    for ref, val in outs:
        for h in range(B_HEADS):
            ref[0, h] = val[:, h * HEAD_DIM:(h + 1) * HEAD_DIM]


def _rw_prep(zb, mu_prev, mu_next, w0, w2, a0, a2, g2, tm=128):
    b, n, cols = zb.shape
    nblk = n // tm
    zero = jnp.zeros((b, 1, cols), F32)
    halo_prev = jnp.concatenate([zero, zb[:, tm - 1:n - 1:tm]], axis=1).reshape(b, nblk, 1, cols)
    halo_next = jnp.concatenate([zb[:, tm::tm], zero], axis=1).reshape(b, nblk, 1, cols)
    pad_cols = lambda t: jnp.pad(t, (0, cols - t.shape[0])).reshape(1, cols)
    rows_f = lambda t: jnp.pad(t, ((0, LANES - RW_LORA), (0, 0))).astype(BF16)
    rows_b = lambda t: jnp.pad(t, ((RW_LORA, LANES - 2 * RW_LORA), (0, 0))).astype(BF16)
    g2p = jnp.pad(g2, ((0, 2 * LANES - RW_GATE_LORA), (0, 0))).astype(BF16)
    const = lambda shape: pl.BlockSpec(shape, lambda i, t: (0,) * len(shape))
    halo = pl.BlockSpec((1, 1, 1, cols), lambda i, t: (i, t, 0, 0))
    hm = jax.ShapeDtypeStruct((b, B_HEADS, n, HEAD_DIM), F32)
    return pl.pallas_call(
        _rw_prep_kernel,
        grid=(b, nblk),
        in_specs=[pl.BlockSpec((1, tm, cols), lambda i, t: (i, t, 0)), halo, halo,
                  const((1, cols)), const((1, cols)), const((2, B_W)), const((LANES, B_W)),
                  const((LANES, B_W)), const((2, B_W)), const((LANES, B_W)), const((LANES, B_W)),
                  const((2 * LANES, B_W))],
        out_specs=[pl.BlockSpec((1, B_HEADS, tm, HEAD_DIM), lambda i, t: (i, 0, t, 0))] * 8,
        out_shape=[hm] * 8,
        compiler_params=_params("parallel", "parallel"),
        name="rwkv_prep",
    )(zb, halo_prev, halo_next, pad_cols(mu_prev), pad_cols(mu_next), w0, rows_f(w2[0]), rows_b(w2[1]),
      a0, rows_f(a2[0]), rows_b(a2[1]), g2p)


def _wkv_chunks(chains):
    c = chains[0][0].shape[0]
    ii = lax.broadcasted_iota(jnp.int32, (c, c), 0)
    jj = lax.broadcasted_iota(jnp.int32, (c, c), 1)
    row = lax.broadcasted_iota(jnp.int32, (c, 1), 0)
    eye = ii == jj
    eye_f = eye.astype(F32)
    masks = {False: (jj <= ii, jj < ii), True: (jj >= ii, jj > ii)}
    r, k, v, cum, rate, st, k_k, k_a, rev = (list(col) for col in zip(*chains))
    incl = [masks[x][0] for x in rev]
    strict = [masks[x][1] for x in rev]

    kk = _each(lambda k, k_k: k * k_k, k, k_k)
    kk = _each(lambda t: t / jnp.maximum(jnp.sqrt(jnp.sum(t * t, axis=1, keepdims=True)), 1e-12), kk)
    kd = _each(lambda k, rate, k_a: k * (1.0 + (rate - 1.0) * k_a), k, rate, k_a)
    b = _each(lambda kk, rate: kk * rate, kk, rate)

    tot = _each(lambda cum, x: cum[0:1] if x else cum[c - 1:c], cum, rev)
    excl = _each(lambda cum, x: jnp.where(row == c - 1, 0.0, pltpu.roll(cum, c - 1, 0)) if x
                 else jnp.where(row == 0, 0.0, pltpu.roll(cum, 1, 0)), cum, rev)
    ar = _each(lambda kk, r, cum, ex: jnp.concatenate([-kk * jnp.exp(ex), r * jnp.exp(cum)], axis=0),
               kk, r, cum, excl)
    g_inv = _each(lambda cum: jnp.exp(-cum), cum)
    x1 = _each(lambda ar, b, g: _dot_nt(ar, b * g), ar, b, g_inv)
    x2 = _each(lambda ar, kd, g: _dot_nt(ar, kd * g), ar, kd, g_inv)
    a_ab = _each(lambda x, m: jnp.where(m, x[:c], 0.0), x1, strict)
    a_rb = _each(lambda x, m: jnp.where(m, x[c:], 0.0), x1, incl)
    a_k = _each(lambda x, ms, mi: jnp.concatenate([jnp.where(ms, x[:c], 0.0), jnp.where(mi, x[c:], 0.0)], axis=0),
                x2, strict, incl)

    inv = _each(lambda t: eye_f + t, a_ab)
    pw = _each(lambda t: _dot(t, t), a_ab)
    for _ in range(c.bit_length() - 3):
        both = _each(lambda inv, pw: _dot(jnp.concatenate([inv, pw], axis=0), pw), inv, pw)
        inv = _each(lambda inv, t: inv + t[:c], inv, both)
        pw = _each(lambda t: t[c:], both)
    inv = _each(lambda inv, pw: inv + _dot(inv, pw), inv, pw)

    akv = _each(_dot, a_k, v)
    ars = _each(_dot, ar, st)
    p = _each(lambda inv, ars, akv: _dot(inv, ars[:c] + akv[:c]), inv, ars, akv)
    y = _each(lambda ars, a_rb, p, akv: ars[c:] + _dot(a_rb, p) + akv[c:], ars, a_rb, p, akv)
    g_end = _each(lambda tot, cum: jnp.exp(tot - cum), tot, cum)
    upd = _each(lambda b, kd, g, p, v: _dot_tn(jnp.concatenate([b * g, kd * g], axis=0),
                                               jnp.concatenate([p, v], axis=0)), b, kd, g_end, p, v)
    g_col = _each(lambda tot: jnp.sum(jnp.where(eye, jnp.exp(tot), 0.0), axis=1, keepdims=True), tot)
    new_st = _each(lambda g, st, upd: g * st + upd, g_col, st, upd)
    return list(zip(y, new_st))


def _rw_scan_kernel(rf, kf, vf, cumf, af, rb, kb, vb, cumb, ab, kk_ref, ka_ref, yf, yb, sf, sb):
    hb, tb = rf.shape[1], rf.shape[2]
    c = WKV_CHUNK
    nch = tb // c

    @pl.when(pl.program_id(2) == 0)
    def _():
        sf[...] = jnp.zeros(sf.shape, F32)
        sb[...] = jnp.zeros(sb.shape, F32)

    def body(ci, carry):
        rows_f = pl.ds(pl.multiple_of(ci * c, c), c)
        rows_b = pl.ds(pl.multiple_of((nch - 1 - ci) * c, c), c)
        chains = []
        for h in range(hb):
            chains.append((rf[0, h, rows_f, :], kf[0, h, rows_f, :], vf[0, h, rows_f, :], cumf[0, h, rows_f, :],
                           af[0, h, rows_f, :], sf[h], kk_ref[h], ka_ref[h], False))
            chains.append((rb[0, h, rows_b, :], kb[0, h, rows_b, :], vb[0, h, rows_b, :], cumb[0, h, rows_b, :],
                           ab[0, h, rows_b, :], sb[h], kk_ref[h], ka_ref[h], True))
        outs = _wkv_chunks(chains)
        for h in range(hb):
            yf[0, h, rows_f, :], sf[h] = outs[2 * h]
            yb[0, h, rows_b, :], sb[h] = outs[2 * h + 1]
        return carry

    lax.fori_loop(0, nch, body, 0)


def _rw_scan(r, k, v, cumf, cumb, af, ab, k_k, k_a, hb=5, tb=256):
    b, nh, n, hd = r.shape
    tb = min(tb, n)
    nt = n // tb
    fwd = pl.BlockSpec((1, hb, tb, hd), lambda i, g, t: (i, g, t, 0))
    bwd = pl.BlockSpec((1, hb, tb, hd), lambda i, g, t: (i, g, nt - 1 - t, 0))
    par = pl.BlockSpec((hb, 1, hd), lambda i, g, t: (g, 0, 0))
    out = jax.ShapeDtypeStruct((b, nh, n, hd), F32)
    return pl.pallas_call(
        _rw_scan_kernel,
        grid=(b, nh // hb, nt),
        in_specs=[fwd] * 5 + [bwd] * 5 + [par, par],
        out_specs=[fwd, bwd],
        out_shape=[out, out],
        scratch_shapes=[pltpu.VMEM((hb, hd, hd), F32)] * 2,
        compiler_params=_params("parallel", "parallel", "arbitrary"),
        name="rwkv_scan",
    )(r, k, v, cumf, af, r, k, v, cumb, ab, k_k.reshape(nh, 1, hd), k_a.reshape(nh, 1, hd))


def _rw_post_kernel(yf, yb, r, k, v, af, ab, g, ka_ref, rk_ref, lnw_ref, lnb_ref, o_ref):
    outs = []
    for h in range(B_HEADS):
        y = yf[0, h] + yb[0, h]
        mean = jnp.mean(y, axis=1, keepdims=True)
        var = jnp.mean(jnp.square(y - mean), axis=1, keepdims=True)
        yn = (y - mean) * lax.rsqrt(var + RW_GN_EPS) * lnw_ref[h] + lnb_ref[h]
        kh, k_a = k[0, h], ka_ref[h]
        kd = kh * (1.0 + (af[0, h] - 1.0) * k_a) + kh * (1.0 + (ab[0, h] - 1.0) * k_a)
        bonus = jnp.sum(r[0, h] * kd * rk_ref[h], axis=1, keepdims=True) * v[0, h]
        outs.append((yn + bonus) * g[0, h])
    o_ref[0] = jnp.concatenate(outs, axis=1).astype(o_ref.dtype)


def _rw_post(yf, yb, r, k, v, af, ab, g, k_a, r_k, ln_w, ln_b, tm=128):
    b, nh, n, hd = r.shape
    tm = min(tm, n)
    hm = pl.BlockSpec((1, nh, tm, hd), lambda i, t: (i, 0, t, 0))
    par = pl.BlockSpec((nh, 1, hd), lambda i, t: (0, 0, 0))
    return pl.pallas_call(
        _rw_post_kernel,
        grid=(b, n // tm),
        in_specs=[hm] * 8 + [par] * 4,
        out_specs=pl.BlockSpec((1, tm, nh * hd), lambda i, t: (i, t, 0)),
        out_shape=jax.ShapeDtypeStruct((b, n, nh * hd), BF16),
        compiler_params=_params("parallel", "parallel"),
        name="rwkv_post",
    )(yf, yb, r, k, v, af, ab, g, k_a.reshape(nh, 1, hd), r_k.reshape(nh, 1, hd),
      ln_w.reshape(nh, 1, hd), ln_b.reshape(nh, 1, hd))


def _rwkv7_bidir(zb, mu_prev, mu_next, w0, w2, a0, a2, g2, k_k, k_a, r_k, ln_w, ln_b):
    r, k, v, cumf, cumb, af, ab, g = _rw_prep(zb, mu_prev, mu_next, w0, w2, a0, a2, g2)
    yf, yb = _rw_scan(r, k, v, cumf, cumb, af, ab, k_k, k_a)
    return _rw_post(yf, yb, r, k, v, af, ab, g, k_a, r_k, ln_w, ln_b)


def _ffn_half_step(x, g, w1, w3, w2):
    h = _ffn_up(_rmsnorm(x, g, BF16), w1, w3)
    return _mm_res([(h, w2)], x, 0.5)


def _mix_ab(x, g, b, n, cos, sin, w_in, w_out, mu_prev, mu_next, w0, w2, a0, a2, g2, k_k, k_a, r_k,
            ln_w, ln_b):
    qkv = 3 * A_W
    xn = _rmsnorm(x, g, BF16)
    w_b = jnp.pad(w_in[:, qkv:], ((0, 0), (0, RW_PAD_COLS - (w_in.shape[1] - qkv))))
    za = _proj(xn, w_in, qkv, tn=qkv // 3).reshape(b, n, qkv)
    zb = _proj(xn, w_b, RW_PAD_COLS, tn=RW_PAD_COLS // 2).reshape(b, n, RW_PAD_COLS)
    oa = _attn_a(za, cos, sin).reshape(b * n, A_W)
    ob = _rwkv7_bidir(zb, mu_prev, mu_next, w0, w2, a0, a2, g2, k_k, k_a, r_k, ln_w, ln_b)
    return _mm_res([(oa, w_out[:A_W]), (ob.reshape(b * n, B_W), w_out[A_W:])], x, 1.0)


def _mix_cd(x, g, b, n, cos, sin, w_in, w_out, sink, rpb):
    cols = w_in.shape[1]
    zc = _proj(_rmsnorm(x, g, BF16), w_in, cols, tn=cols // 2).reshape(b, n, cols)
    oc = _attn_c(zc, sink, cos, sin).reshape(b * n, C_W)
    od = _attn_d(zc, _na_bias_table(rpb, n // GRID_W)).reshape(b * n, D_W)
    return _mm_res([(oc, w_out[:C_W]), (od, w_out[C_W:])], x, 1.0)


def kernel(x, p, ffn1_norm, ffn1_w1, ffn1_w3, ffn1_w2, mix_norm, ffn2_norm, ffn2_w1, ffn2_w3, ffn2_w2, ple_norm, ple_w_gate, ple_w_proj, ab_w_in, ab_w_out, rw_mu_prev, rw_mu_next, rw_w0, rw_w2, rw_a0, rw_a2, rw_g2, rw_k_k, rw_k_a, rw_r_k, rw_ln_w, rw_ln_b, cd_w_in, cd_w_out, c_sink, d_rpb, final_norm):
    b, n, d = x.shape
    depth = p.shape[0]
    cos, sin = _rope_tables(n)
    x = x.reshape(b * n, d)
    for i in range(depth):
        j = i // 2
        x = _ffn_half_step(x, ffn1_norm[i], ffn1_w1[i], ffn1_w3[i], ffn1_w2[i])
        if i % 2 == 0:
            x = _mix_ab(x, mix_norm[i], b, n, cos, sin, ab_w_in[j], ab_w_out[j], rw_mu_prev[j],
                        rw_mu_next[j], rw_w0[j], rw_w2[j], rw_a0[j], rw_a2[j], rw_g2[j], rw_k_k[j],
                        rw_k_a[j], rw_r_k[j], rw_ln_w[j], rw_ln_b[j])
        else:
            x = _mix_cd(x, mix_norm[i], b, n, cos, sin, cd_w_in[j], cd_w_out[j], c_sink[j], d_rpb[j])
        x = _ffn_half_step(x, ffn2_norm[i], ffn2_w1[i], ffn2_w3[i], ffn2_w2[i])
        x = _ple(x, _rmsnorm(x, ple_norm[i], BF16), ple_w_gate[i], p[i].reshape(b * n, -1), ple_w_proj[i])
    return _rmsnorm(x, final_norm, F32).reshape(b, n, d)
```

```python
import functools

import jax
import jax.numpy as jnp
from jax import lax
from jax.experimental import pallas as pl
from jax.experimental.pallas import tpu as pltpu

F32 = jnp.float32
BF16 = jnp.bfloat16

HEAD_DIM = 64
LANES = 128
NORM_EPS = 1e-6
ROPE_THETA = 10000.0
GRID_W = 64
A_HEADS = 12
A_W = A_HEADS * HEAD_DIM
A_HALF = 64
A_DILATIONS = (1, 4, 16)
B_HEADS = 20
B_W = B_HEADS * HEAD_DIM
RW_LORA = 64
RW_GATE_LORA = 192
RW_GN_EPS = 64e-5
RW_PAD_COLS = 3 * B_W + 4 * LANES
C_HEADS = 16
C_KV_HEADS = 2
C_GROUP = C_HEADS // C_KV_HEADS
C_W = C_HEADS * HEAD_DIM
C_HALF = 128
D_HEADS = 16
D_W = D_HEADS * HEAD_DIM
NA_KH = 8
NA_KW = 16
MASKED = -1e30
WKV_CHUNK = 64
VMEM_LIMIT = 56 * 1024 * 1024


def _params(*sem):
    return pltpu.CompilerParams(dimension_semantics=sem, vmem_limit_bytes=VMEM_LIMIT)


def _dot(a, b):
    return jnp.dot(a.astype(BF16), b.astype(BF16), preferred_element_type=F32)


def _dot_nt(a, b):
    return lax.dot_general(a.astype(BF16), b.astype(BF16), (((1,), (1,)), ((), ())),
                           preferred_element_type=F32)


def _dot_tn(a, b):
    return lax.dot_general(a.astype(BF16), b.astype(BF16), (((0,), (0,)), ((), ())),
                           preferred_element_type=F32)


def _each(f, *cols):
    return [f(*xs) for xs in zip(*cols)]


def _rmsnorm_kernel(x_ref, g_ref, o_ref):
    x = x_ref[...]
    ms = jnp.mean(x * x, axis=-1, keepdims=True)
    o_ref[...] = (x * lax.rsqrt(ms + NORM_EPS) * g_ref[...]).astype(o_ref.dtype)


def _rmsnorm(x, g, dtype, tm=512):
    m, d = x.shape
    return pl.pallas_call(
        _rmsnorm_kernel,
        grid=(m // tm,),
        in_specs=[pl.BlockSpec((tm, d), lambda i: (i, 0)),
                  pl.BlockSpec((1, d), lambda i: (0, 0))],
        out_specs=pl.BlockSpec((tm, d), lambda i: (i, 0)),
        out_shape=jax.ShapeDtypeStruct((m, d), dtype),
        compiler_params=_params("parallel"),
        name="rmsnorm",
    )(x, g.reshape(1, d))


def _cast_weight(w_ref, wb_ref):
    rows = 256

    @pl.when(pl.program_id(1) == 0)
    def _():
        def body(i, carry):
            sl = pl.ds(pl.multiple_of(i * rows, rows), rows)
            wb_ref[sl, :] = w_ref[sl, :].astype(BF16)
            return carry

        lax.fori_loop(0, w_ref.shape[0] // rows, body, 0)


def _row_spec(tm, k):
    return pl.BlockSpec((tm, k), lambda j, i: (i, 0))


def _col_spec(w, k, tn, **kw):
    if isinstance(w, tuple):
        layer = w[1]
        return pl.BlockSpec((None, k, tn), lambda j, i: (layer, 0, j), **kw)
    return pl.BlockSpec((k, tn), lambda j, i: (0, j), **kw)


def _arr(w):
    return w[0] if isinstance(w, tuple) else w


def _tile_spec(tm, tn):
    return pl.BlockSpec((tm, tn), lambda j, i: (i, j))


def _ffn_up_kernel(xn_ref, w1_ref, w3_ref, o_ref, w1b, w3b):
    _cast_weight(w1_ref, w1b)
    _cast_weight(w3_ref, w3b)
    xn = xn_ref[...]
    a = jnp.dot(xn, w1b[...], preferred_element_type=F32)
    b = jnp.dot(xn, w3b[...], preferred_element_type=F32)
    o_ref[...] = (a * jax.nn.sigmoid(a) * b).astype(o_ref.dtype)


def _ffn_up(xn, w1, w3, tm=1024, tn=512):
    m, d = xn.shape
    f = _arr(w1).shape[-1]
    return pl.pallas_call(
        _ffn_up_kernel,
        grid=(f // tn, m // tm),
        in_specs=[_row_spec(tm, d), _col_spec(w1, d, tn), _col_spec(w3, d, tn)],
        out_specs=_tile_spec(tm, tn),
        out_shape=jax.ShapeDtypeStruct((m, f), BF16),
        scratch_shapes=[pltpu.VMEM((d, tn), BF16)] * 2,
        compiler_params=_params("parallel", "arbitrary"),
        name="ffn_up",
    )(xn, _arr(w1), _arr(w3))


def _proj_kernel(xn_ref, w_ref, o_ref, wb):
    _cast_weight(w_ref, wb)
    o_ref[...] = jnp.dot(xn_ref[...], wb[...], preferred_element_type=F32)


def _proj(xn, w, ncols, tn, tm=512):
    m, d = xn.shape
    return pl.pallas_call(
        _proj_kernel,
        grid=(ncols // tn, m // tm),
        in_specs=[_row_spec(tm, d), _col_spec(w, d, tn, pipeline_mode=pl.Buffered(1))],
        out_specs=_tile_spec(tm, tn),
        out_shape=jax.ShapeDtypeStruct((m, ncols), F32),
        scratch_shapes=[pltpu.VMEM((d, tn), BF16)],
        compiler_params=_params("parallel", "arbitrary"),
        name="proj",
    )(xn, _arr(w))


def _ple_kernel(xn_ref, w_ref, x_ref, p_ref, wp_ref, o_ref, wb, wpb):
    _cast_weight(w_ref, wb)
    _cast_weight(wp_ref, wpb)
    gate = jax.nn.sigmoid(jnp.dot(xn_ref[...], wb[...], preferred_element_type=F32))
    emb = jnp.dot(p_ref[...].astype(BF16), wpb[...], preferred_element_type=F32)
    o_ref[...] = x_ref[...] + emb * gate


def _ple(x, xn, w_gate, p, w_proj, tm=512, tn=512):
    m, d = x.shape
    pd = p[0].shape[-1]
    return pl.pallas_call(
        _ple_kernel,
        grid=(d // tn, m // tm),
        in_specs=[_row_spec(tm, d), _col_spec(w_gate, d, tn), _tile_spec(tm, tn),
                  pl.BlockSpec((None, tm, pd), lambda j, i: (p[1], i, 0)), _col_spec(w_proj, pd, tn)],
        out_specs=_tile_spec(tm, tn),
        out_shape=jax.ShapeDtypeStruct((m, d), F32),
        scratch_shapes=[pltpu.VMEM((d, tn), BF16), pltpu.VMEM((pd, tn), BF16)],
        compiler_params=_params("parallel", "arbitrary"),
        name="ple",
    )(xn, _arr(w_gate), x, p[0], _arr(w_proj))


def _mm_res_kernel(*refs, n_pairs, scale):
    res_ref, o_ref = refs[2 * n_pairs], refs[2 * n_pairs + 1]
    scratch = refs[2 * n_pairs + 2:]
    acc = None
    for t in range(n_pairs):
        _cast_weight(refs[2 * t + 1], scratch[t])
        part = jnp.dot(refs[2 * t][...], scratch[t][...], preferred_element_type=F32)
        acc = part if acc is None else acc + part
    o_ref[...] = res_ref[...] + scale * acc


def _mm_res(pairs, res, scale, tm=512, tn=512):
    m, n = res.shape
    in_specs, args, scratch = [], [], []
    for a, w in pairs:
        k = a.shape[1]
        in_specs += [_row_spec(tm, k), _col_spec(w, k, tn)]
        args += [a, _arr(w)]
        scratch.append(pltpu.VMEM((k, tn), BF16))
    in_specs.append(_tile_spec(tm, tn))
    return pl.pallas_call(
        functools.partial(_mm_res_kernel, n_pairs=len(pairs), scale=scale),
        grid=(n // tn, m // tm),
        in_specs=in_specs,
        out_specs=_tile_spec(tm, tn),
        out_shape=jax.ShapeDtypeStruct((m, n), F32),
        scratch_shapes=scratch,
        compiler_params=_params("parallel", "arbitrary"),
        name="mm_res",
    )(*args, res)


def _lane_lo(shape=(1, LANES)):
    return lax.broadcasted_iota(jnp.int32, shape, len(shape) - 1) < HEAD_DIM


def _rope_tables(n):
    half = HEAD_DIM // 2
    inv_freq = ROPE_THETA ** (-jnp.arange(half, dtype=F32) / half)
    ang = jnp.arange(n, dtype=F32)[:, None] * inv_freq[None, :]
    cos, sin = jnp.cos(ang), jnp.sin(ang)
    return (jnp.concatenate([cos, cos, cos, cos], axis=-1),
            jnp.concatenate([-sin, sin, -sin, sin], axis=-1))


def _rope_tile(x, cos, sin):
    lane = lax.broadcasted_iota(jnp.int32, (1, LANES), 1)
    first_half = (lane % HEAD_DIM) < (HEAD_DIM // 2)
    partner = jnp.where(first_half, pltpu.roll(x, LANES - HEAD_DIM // 2, 1),
                        pltpu.roll(x, HEAD_DIM // 2, 1))
    return x * cos + partner * sin


def _attn_a_kernel(q_ref, k_ref, v_ref, cos_ref, sin_ref, o_ref, qs, ks, m_s, l_s, acc_s):
    n = q_ref.shape[1]
    lo = _lane_lo()
    rb = 256

    def rope_body(i, carry):
        sl = pl.ds(pl.multiple_of(i * rb, rb), rb)
        c, s = cos_ref[sl, :], sin_ref[sl, :]
        qs[sl, :] = _rope_tile(q_ref[0, sl, :], c, s) * (HEAD_DIM ** -0.5)
        ks[sl, :] = _rope_tile(k_ref[0, sl, :], c, s)
        return carry

    lax.fori_loop(0, n // rb, rope_body, 0)
    m_s[...] = jnp.full(m_s.shape, MASKED, F32)
    l_s[...] = jnp.zeros(l_s.shape, F32)
    acc_s[...] = jnp.zeros(acc_s.shape, F32)

    group = 4

    for dil in A_DILATIONS:
        m = n // dil
        qb_rows = min(128, m)
        span = min(qb_rows + 2 * A_HALF, m)
        nqb = m // qb_rows

        def blocks(items, dil=dil, m=m, qb_rows=qb_rows, span=span):
            q0 = [qb * qb_rows for _, qb in items]
            k0 = [jnp.clip(t - A_HALF, 0, m - span) for t in q0]
            if dil == 1:
                rows_q = [pl.ds(pl.multiple_of(t, qb_rows), qb_rows) for t in q0]
                rows_k = [pl.ds(pl.multiple_of(t, A_HALF), span) for t in k0]
            else:
                rows_q = [pl.ds(r + t * dil, qb_rows, stride=dil) for (r, _), t in zip(items, q0)]
                rows_k = [pl.ds(r + t * dil, span, stride=dil) for (r, _), t in zip(items, k0)]
            q = [qs[rq, :] for rq in rows_q]
            kb = [ks[rk, :].astype(BF16) for rk in rows_k]
            vb = [v_ref[0, rk, :].astype(BF16) for rk in rows_k]
            m_old = [m_s[rq, :] for rq in rows_q]
            l_old = [l_s[rq, :] for rq in rows_q]
            acc_old = [acc_s[rq, :] for rq in rows_q]
            mask = _each(lambda a, c: jnp.abs(a + lax.broadcasted_iota(jnp.int32, (qb_rows, 1), 0)
                                              - c - lax.broadcasted_iota(jnp.int32, (1, span), 1)) <= A_HALF,
                         q0, k0)
            s0 = _each(lambda q, kb, mk: jnp.where(mk, _dot_nt(jnp.where(lo, q, 0.0), kb), MASKED), q, kb, mask)
            s1 = _each(lambda q, kb, mk: jnp.where(mk, _dot_nt(jnp.where(lo, 0.0, q), kb), MASKED), q, kb, mask)
            mn0 = _each(lambda mo, s: jnp.maximum(mo[:, :1], jnp.max(s, axis=1, keepdims=True)), m_old, s0)
            mn1 = _each(lambda mo, s: jnp.maximum(mo[:, LANES - 1:], jnp.max(s, axis=1, keepdims=True)), m_old, s1)
            p0 = _each(lambda s, mn: jnp.exp(s - mn), s0, mn0)
            p1 = _each(lambda s, mn: jnp.exp(s - mn), s1, mn1)
            m_new = _each(lambda a, c: jnp.where(lo, a, c), mn0, mn1)
            alpha = _each(lambda mo, mn: jnp.exp(mo - mn), m_old, m_new)
            psum = _each(lambda a, c: jnp.where(lo, jnp.sum(a, axis=1, keepdims=True),
                                                jnp.sum(c, axis=1, keepdims=True)), p0, p1)
            pv = _each(lambda a, c, vb: jnp.where(lo, _dot(a, vb), _dot(c, vb)), p0, p1, vb)
            for rq, mn, al, lold, ps, ao, pvi in zip(rows_q, m_new, alpha, l_old, psum, acc_old, pv):
                m_s[rq, :] = mn
                l_s[rq, :] = al * lold + ps
                acc_s[rq, :] = al * ao + pvi

        if dil >= group:
            def res_body(rg, carry, blocks=blocks, nqb=nqb):
                def qb_body(qb, c2):
                    blocks([(rg * group + u, qb) for u in range(group)])
                    return c2

                return lax.fori_loop(0, nqb, qb_body, carry)

            lax.fori_loop(0, dil // group, res_body, 0)
        else:
            def qb_body(qg, carry, blocks=blocks):
                blocks([(0, qg * group + u) for u in range(group)])
                return carry

            lax.fori_loop(0, nqb // group, qb_body, 0)

    def out_body(i, carry):
        sl = pl.ds(pl.multiple_of(i * rb, rb), rb)
        o_ref[0, sl, :] = (acc_s[sl, :] / l_s[sl, :]).astype(o_ref.dtype)
        return carry

    lax.fori_loop(0, n // rb, out_body, 0)


def _attn_a(za, cos, sin):
    b, n, _ = za.shape
    npair = A_W // LANES
    blk = lambda off: pl.BlockSpec((1, n, LANES), lambda i, j, off=off: (i, 0, off + j))
    tab = pl.BlockSpec((n, LANES), lambda i, j: (0, 0))
    return pl.pallas_call(
        _attn_a_kernel,
        grid=(b, npair),
        in_specs=[blk(0), blk(npair), blk(2 * npair), tab, tab],
        out_specs=pl.BlockSpec((1, n, LANES), lambda i, j: (i, 0, j)),
        out_shape=jax.ShapeDtypeStruct((b, n, A_W), BF16),
        scratch_shapes=[pltpu.VMEM((n, LANES), F32)] * 5,
        compiler_params=_params("parallel", "parallel"),
        name="attn_dilated",
    )(za, za, za, cos, sin)


def _attn_c_kernel(sink_ref, q_ref, k_ref, v_ref, cosq_ref, sinq_ref, cos_ref, sin_ref, o_ref,
                   klo, khi, vlo, vhi):
    n = k_ref.shape[1]
    qb_rows = q_ref.shape[1]
    span = qb_rows + 2 * C_HALF
    hk = pl.program_id(1)
    qb = pl.program_id(2)
    lo = _lane_lo()

    @pl.when(qb == 0)
    def _():
        rb = 256

        def body(i, carry):
            sl = pl.ds(pl.multiple_of(i * rb, rb), rb)
            kx = _rope_tile(k_ref[0, sl, :], cos_ref[sl, :], sin_ref[sl, :])
            kw = pltpu.roll(kx, HEAD_DIM, 1)
            vx = v_ref[0, sl, :]
            vw = pltpu.roll(vx, HEAD_DIM, 1)
            first = hk == 0
            klo[sl, :] = jnp.where(lo, jnp.where(first, kx, kw), 0.0).astype(BF16)
            khi[sl, :] = jnp.where(lo, 0.0, jnp.where(first, kw, kx)).astype(BF16)
            vlo[sl, :] = jnp.where(lo, jnp.where(first, vx, vw), 0.0).astype(BF16)
            vhi[sl, :] = jnp.where(lo, 0.0, jnp.where(first, vw, vx)).astype(BF16)
            return carry

        lax.fori_loop(0, n // rb, body, 0)

    q0 = qb * qb_rows
    k0 = pl.multiple_of(jnp.clip(q0 - C_HALF, 0, n - span), C_HALF)
    rows_k = pl.ds(k0, span)
    kl, kh = klo[rows_k, :], khi[rows_k, :]
    vl, vh = vlo[rows_k, :], vhi[rows_k, :]
    qpos = q0 + lax.broadcasted_iota(jnp.int32, (qb_rows, 1), 0)
    kpos = k0 + lax.broadcasted_iota(jnp.int32, (1, span), 1)
    mask = jnp.abs(qpos - kpos) <= C_HALF
    cq, sq = cosq_ref[...], sinq_ref[...]

    ntile = q_ref.shape[2] // LANES
    qt = [_rope_tile(q_ref[0, :, j * LANES:(j + 1) * LANES], cq, sq) * (HEAD_DIM ** -0.5) for j in range(ntile)]
    qt = [t for t in qt for _ in range(2)]
    kk = [kl, kh] * ntile
    sk = [sink_ref[hk * C_GROUP + h] for h in range(2 * ntile)]
    s = _each(lambda q, k: jnp.where(mask, _dot_nt(q, k), MASKED), qt, kk)
    mx = _each(lambda s, sk: jnp.maximum(jnp.max(s, axis=1, keepdims=True), sk), s, sk)
    p = _each(lambda s, mx: jnp.exp(s - mx), s, mx)
    den = _each(lambda p, sk, mx: jnp.sum(p, axis=1, keepdims=True) + jnp.exp(sk - mx), p, sk, mx)
    for j in range(ntile):
        pv = _dot(p[2 * j], vl) + _dot(p[2 * j + 1], vh)
        o_ref[0, :, j * LANES:(j + 1) * LANES] = (pv / jnp.where(lo, den[2 * j], den[2 * j + 1])).astype(o_ref.dtype)


def _attn_c(zc, sink, cos, sin, qb_rows=128):
    b, n, _ = zc.shape
    gw = C_GROUP * HEAD_DIM
    kcol = C_W // LANES
    full = lambda off: pl.BlockSpec((1, n, LANES), lambda i, h, t, off=off: (i, 0, off))
    tabq = pl.BlockSpec((qb_rows, LANES), lambda i, h, t: (t, 0))
    tab = pl.BlockSpec((n, LANES), lambda i, h, t: (0, 0))
    return pl.pallas_call(
        _attn_c_kernel,
        grid=(b, C_KV_HEADS, n // qb_rows),
        in_specs=[pl.BlockSpec(memory_space=pltpu.SMEM),
                  pl.BlockSpec((1, qb_rows, gw), lambda i, h, t: (i, t, h)),
                  full(kcol), full(kcol + 1), tabq, tabq, tab, tab],
        out_specs=pl.BlockSpec((1, qb_rows, gw), lambda i, h, t: (i, t, h)),
        out_shape=jax.ShapeDtypeStruct((b, n, C_W), BF16),
        scratch_shapes=[pltpu.VMEM((n, LANES), BF16)] * 4,
        compiler_params=_params("parallel", "parallel", "arbitrary"),
        name="attn_gqa_sink",
    )(sink.reshape(-1), zc, zc, zc, cos, sin, cos, sin)


def _na_bias_table(rpb, rows):
    kh = min(NA_KH, rows)
    qc = jnp.arange(GRID_W)[:, None]
    kc = jnp.arange(GRID_W)[None, :]
    qc0 = jnp.clip(qc - NA_KW // 2, 0, GRID_W - NA_KW)
    valid = (kc >= qc0) & (kc < qc0 + NA_KW)
    ci = jnp.clip(kc - qc, -(NA_KW - 1), NA_KW - 1) + NA_KW - 1
    by_row = jnp.stack([rpb.astype(F32)[:, off:off + kh] for off in range(NA_KH)], axis=1)
    onehot = (ci[None] == jnp.arange(2 * NA_KW - 1)[:, None, None]).astype(F32)
    bias = jnp.einsum('hokc,cqj->hoqkj', by_row, onehot, precision=lax.Precision.HIGHEST)
    bias = jnp.where(valid[None, None, :, None, :], bias, MASKED)
    bias = bias.reshape(D_HEADS // 2, 2, NA_KH, GRID_W, kh * GRID_W)
    return bias.transpose(0, 2, 1, 3, 4).reshape(D_HEADS // 2, NA_KH, 2 * GRID_W, kh * GRID_W)


def _attn_d_kernel(q_ref, k_ref, v_ref, bias_ref, o_ref):
    n = q_ref.shape[1]
    rows = n // GRID_W
    kh = min(NA_KH, rows)
    lo = _lane_lo()
    group = 4 if rows % 4 == 0 else 1

    def body(rg, carry):
        r = [rg * group + u for u in range(group)]
        r0 = [jnp.clip(t - NA_KH // 2, 0, rows - kh) for t in r]
        rows_q = [pl.ds(pl.multiple_of(t * GRID_W, GRID_W), GRID_W) for t in r]
        rows_k = [pl.ds(pl.multiple_of(t * GRID_W, GRID_W), kh * GRID_W) for t in r0]
        q = [q_ref[0, rq, :] * (HEAD_DIM ** -0.5) for rq in rows_q]
        q2 = _each(lambda q: jnp.concatenate([jnp.where(lo, q, 0.0), jnp.where(lo, 0.0, q)], axis=0), q)
        s = [_dot_nt(q2i, k_ref[0, rk, :]) + bias_ref[0, t0 - t + NA_KH - 1]
             for q2i, rk, t0, t in zip(q2, rows_k, r0, r)]
        p = _each(lambda s: jnp.exp(s - jnp.max(s, axis=1, keepdims=True)), s)
        pv = [_dot(pi, v_ref[0, rk, :]) / jnp.sum(pi, axis=1, keepdims=True) for pi, rk in zip(p, rows_k)]
        for rq, pvi in zip(rows_q, pv):
            o_ref[0, rq, :] = jnp.where(lo, pvi[:GRID_W], pvi[GRID_W:]).astype(o_ref.dtype)
        return carry

    lax.fori_loop(0, rows // group, body, 0)


def _attn_d(zc, bias):
    b, n, _ = zc.shape
    npair = D_W // LANES
    q_off = (C_W + 2 * C_KV_HEADS * HEAD_DIM) // LANES
    blk = lambda off: pl.BlockSpec((1, n, LANES), lambda i, j, off=off: (i, 0, off + j))
    return pl.pallas_call(
        _attn_d_kernel,
        grid=(b, npair),
        in_specs=[blk(q_off), blk(q_off + npair), blk(q_off + 2 * npair),
                  pl.BlockSpec((1,) + bias.shape[1:], lambda i, j: (j, 0, 0, 0))],
        out_specs=pl.BlockSpec((1, n, LANES), lambda i, j: (i, 0, j)),
        out_shape=jax.ShapeDtypeStruct((b, n, D_W), BF16),
        compiler_params=_params("parallel", "parallel"),
        name="attn_neighborhood",
    )(zc, zc, zc, bias)


def _rw_prep_kernel(z_ref, hp_ref, hn_ref, mup_ref, mun_ref, w0_ref, w2f_ref, w2b_ref, a0_ref,
                    a2f_ref, a2b_ref, g2_ref, r_o, k_o, v_o, cumf_o, cumb_o, af_o, ab_o, g_o):
    tm = z_ref.shape[1]
    z = z_ref[0]
    row = lax.broadcasted_iota(jnp.int32, (tm, 1), 0)
    z_prev = jnp.where(row == 0, hp_ref[0, 0], pltpu.roll(z, 1, 0))
    z_next = jnp.where(row == tm - 1, hn_ref[0, 0], pltpu.roll(z, tm - 1, 0))
    z = z + mup_ref[...] * (z_prev - z) + mun_ref[...] * (z_next - z)
    base = 3 * B_W
    wl = jnp.tanh(z[:, base:base + LANES])
    al = z[:, base + LANES:base + 2 * LANES]
    gl = jax.nn.sigmoid(z[:, base + 2 * LANES:base + 4 * LANES])

    def log_decay(e, w2_ref):
        x = w0_ref[e:e + 1, :] + _dot(wl, w2_ref[...])
        w = -(jnp.maximum(-x, 0.0) + jnp.log1p(jnp.exp(-jnp.abs(x)))) - 0.5
        return -jnp.exp(w)

    def rate(e, a2_ref):
        return jax.nn.sigmoid(a0_ref[e:e + 1, :] + _dot(al, a2_ref[...]))

    crow = row % WKV_CHUNK

    def chunk_cumsum(x, rev):
        s = 1
        while s < WKV_CHUNK:
            if rev:
                x = x + jnp.where(crow < WKV_CHUNK - s, pltpu.roll(x, tm - s, 0), 0.0)
            else:
                x = x + jnp.where(crow >= s, pltpu.roll(x, s, 0), 0.0)
            s *= 2
        return x

    outs = ((r_o, z[:, :B_W]), (k_o, z[:, B_W:2 * B_W]), (v_o, z[:, 2 * B_W:3 * B_W]),
            (cumf_o, chunk_cumsum(log_decay(0, w2f_ref), False)),
            (cumb_o, chunk_cumsum(log_decay(1, w2b_ref), True)),
            (af_o, rate(0, a2f_ref)), (ab_o, rate(1, a2b_ref)), (g_o, _dot(gl, g2_ref[...])))
    for ref, val in outs:
        for h in range(B_HEADS):
            ref[0, h] = val[:, h * HEAD_DIM:(h + 1) * HEAD_DIM]


def _rw_prep(zb, mu_prev, mu_next, w0, w2, a0, a2, g2, tm=128):
    b, n, cols = zb.shape
    nblk = n // tm
    zero = jnp.zeros((b, 1, cols), F32)
    halo_prev = jnp.concatenate([zero, zb[:, tm - 1:n - 1:tm]], axis=1).reshape(b, nblk, 1, cols)
    halo_next = jnp.concatenate([zb[:, tm::tm], zero], axis=1).reshape(b, nblk, 1, cols)
    pad_cols = lambda t: jnp.pad(t, (0, cols - t.shape[0])).reshape(1, cols)
    rows_f = lambda t: jnp.pad(t, ((0, LANES - RW_LORA), (0, 0))).astype(BF16)
    rows_b = lambda t: jnp.pad(t, ((RW_LORA, LANES - 2 * RW_LORA), (0, 0))).astype(BF16)
    g2p = jnp.pad(g2, ((0, 2 * LANES - RW_GATE_LORA), (0, 0))).astype(BF16)
    const = lambda shape: pl.BlockSpec(shape, lambda i, t: (0,) * len(shape))
    halo = pl.BlockSpec((1, 1, 1, cols), lambda i, t: (i, t, 0, 0))
    hm = jax.ShapeDtypeStruct((b, B_HEADS, n, HEAD_DIM), F32)
    return pl.pallas_call(
        _rw_prep_kernel,
        grid=(b, nblk),
        in_specs=[pl.BlockSpec((1, tm, cols), lambda i, t: (i, t, 0)), halo, halo,
                  const((1, cols)), const((1, cols)), const((2, B_W)), const((LANES, B_W)),
                  const((LANES, B_W)), const((2, B_W)), const((LANES, B_W)), const((LANES, B_W)),
                  const((2 * LANES, B_W))],
        out_specs=[pl.BlockSpec((1, B_HEADS, tm, HEAD_DIM), lambda i, t: (i, 0, t, 0))] * 8,
        out_shape=[hm] * 8,
        compiler_params=_params("parallel", "parallel"),
        name="rwkv_prep",
    )(zb, halo_prev, halo_next, pad_cols(mu_prev), pad_cols(mu_next), w0, rows_f(w2[0]), rows_b(w2[1]),
      a0, rows_f(a2[0]), rows_b(a2[1]), g2p)


def _wkv_chunks(chains):
    c = chains[0][0].shape[0]
    ii = lax.broadcasted_iota(jnp.int32, (c, c), 0)
    jj = lax.broadcasted_iota(jnp.int32, (c, c), 1)
    row = lax.broadcasted_iota(jnp.int32, (c, 1), 0)
    eye = ii == jj
    eye_f = eye.astype(F32)
    masks = {False: (jj <= ii, jj < ii), True: (jj >= ii, jj > ii)}
    r, k, v, cum, rate, st, k_k, k_a, rev = (list(col) for col in zip(*chains))
    incl = [masks[x][0] for x in rev]
    strict = [masks[x][1] for x in rev]

    kk = _each(lambda k, k_k: k * k_k, k, k_k)
    kk = _each(lambda t: t / jnp.maximum(jnp.sqrt(jnp.sum(t * t, axis=1, keepdims=True)), 1e-12), kk)
    kd = _each(lambda k, rate, k_a: k * (1.0 + (rate - 1.0) * k_a), k, rate, k_a)
    b = _each(lambda kk, rate: kk * rate, kk, rate)

    tot = _each(lambda cum, x: cum[0:1] if x else cum[c - 1:c], cum, rev)
    excl = _each(lambda cum, x: jnp.where(row == c - 1, 0.0, pltpu.roll(cum, c - 1, 0)) if x
                 else jnp.where(row == 0, 0.0, pltpu.roll(cum, 1, 0)), cum, rev)
    ar = _each(lambda kk, r, cum, ex: jnp.concatenate([-kk * jnp.exp(ex), r * jnp.exp(cum)], axis=0),
               kk, r, cum, excl)
    g_inv = _each(lambda cum: jnp.exp(-cum), cum)
    x1 = _each(lambda ar, b, g: _dot_nt(ar, b * g), ar, b, g_inv)
    x2 = _each(lambda ar, kd, g: _dot_nt(ar, kd * g), ar, kd, g_inv)
    a_ab = _each(lambda x, m: jnp.where(m, x[:c], 0.0), x1, strict)
    a_rb = _each(lambda x, m: jnp.where(m, x[c:], 0.0), x1, incl)
    a_k = _each(lambda x, ms, mi: jnp.concatenate([jnp.where(ms, x[:c], 0.0), jnp.where(mi, x[c:], 0.0)], axis=0),
                x2, strict, incl)

    inv = _each(lambda t: eye_f + t, a_ab)
    pw = _each(lambda t: _dot(t, t), a_ab)
    for _ in range(c.bit_length() - 3):
        both = _each(lambda inv, pw: _dot(jnp.concatenate([inv, pw], axis=0), pw), inv, pw)
        inv = _each(lambda inv, t: inv + t[:c], inv, both)
        pw = _each(lambda t: t[c:], both)
    inv = _each(lambda inv, pw: inv + _dot(inv, pw), inv, pw)

    akv = _each(_dot, a_k, v)
    ars = _each(_dot, ar, st)
    p = _each(lambda inv, ars, akv: _dot(inv, ars[:c] + akv[:c]), inv, ars, akv)
    y = _each(lambda ars, a_rb, p, akv: ars[c:] + _dot(a_rb, p) + akv[c:], ars, a_rb, p, akv)
    g_end = _each(lambda tot, cum: jnp.exp(tot - cum), tot, cum)
    upd = _each(lambda b, kd, g, p, v: _dot_tn(jnp.concatenate([b * g, kd * g], axis=0),
                                               jnp.concatenate([p, v], axis=0)), b, kd, g_end, p, v)
    g_col = _each(lambda tot: jnp.sum(jnp.where(eye, jnp.exp(tot), 0.0), axis=1, keepdims=True), tot)
    new_st = _each(lambda g, st, upd: g * st + upd, g_col, st, upd)
    return list(zip(y, new_st))


def _rw_scan_kernel(rf, kf, vf, cumf, af, rb, kb, vb, cumb, ab, kk_ref, ka_ref, yf, yb, sf, sb):
    hb, tb = rf.shape[1], rf.shape[2]
    c = WKV_CHUNK
    nch = tb // c

    @pl.when(pl.program_id(2) == 0)
    def _():
        sf[...] = jnp.zeros(sf.shape, F32)
        sb[...] = jnp.zeros(sb.shape, F32)

    def body(ci, carry):
        rows_f = pl.ds(pl.multiple_of(ci * c, c), c)
        rows_b = pl.ds(pl.multiple_of((nch - 1 - ci) * c, c), c)
        chains = []
        for h in range(hb):
            chains.append((rf[0, h, rows_f, :], kf[0, h, rows_f, :], vf[0, h, rows_f, :], cumf[0, h, rows_f, :],
                           af[0, h, rows_f, :], sf[h], kk_ref[h], ka_ref[h], False))
            chains.append((rb[0, h, rows_b, :], kb[0, h, rows_b, :], vb[0, h, rows_b, :], cumb[0, h, rows_b, :],
                           ab[0, h, rows_b, :], sb[h], kk_ref[h], ka_ref[h], True))
        outs = _wkv_chunks(chains)
        for h in range(hb):
            yf[0, h, rows_f, :], sf[h] = outs[2 * h]
            yb[0, h, rows_b, :], sb[h] = outs[2 * h + 1]
        return carry

    lax.fori_loop(0, nch, body, 0)


def _rw_scan(r, k, v, cumf, cumb, af, ab, k_k, k_a, hb=10, tb=256):
    b, nh, n, hd = r.shape
    tb = min(tb, n)
    nt = n // tb
    fwd = pl.BlockSpec((1, hb, tb, hd), lambda i, g, t: (i, g, t, 0))
    bwd = pl.BlockSpec((1, hb, tb, hd), lambda i, g, t: (i, g, nt - 1 - t, 0))
    par = pl.BlockSpec((hb, 1, hd), lambda i, g, t: (g, 0, 0))
    out = jax.ShapeDtypeStruct((b, nh, n, hd), F32)
    return pl.pallas_call(
        _rw_scan_kernel,
        grid=(b, nh // hb, nt),
        in_specs=[fwd] * 5 + [bwd] * 5 + [par, par],
        out_specs=[fwd, bwd],
        out_shape=[out, out],
        scratch_shapes=[pltpu.VMEM((hb, hd, hd), F32)] * 2,
        compiler_params=_params("parallel", "parallel", "arbitrary"),
        name="rwkv_scan",
    )(r, k, v, cumf, af, r, k, v, cumb, ab, k_k.reshape(nh, 1, hd), k_a.reshape(nh, 1, hd))


def _rw_post_kernel(yf, yb, r, k, v, af, ab, g, ka_ref, rk_ref, lnw_ref, lnb_ref, o_ref):
    outs = []
    for h in range(B_HEADS):
        y = yf[0, h] + yb[0, h]
        mean = jnp.mean(y, axis=1, keepdims=True)
        var = jnp.mean(jnp.square(y - mean), axis=1, keepdims=True)
        yn = (y - mean) * lax.rsqrt(var + RW_GN_EPS) * lnw_ref[h] + lnb_ref[h]
        kh, k_a = k[0, h], ka_ref[h]
        kd = kh * (1.0 + (af[0, h] - 1.0) * k_a) + kh * (1.0 + (ab[0, h] - 1.0) * k_a)
        bonus = jnp.sum(r[0, h] * kd * rk_ref[h], axis=1, keepdims=True) * v[0, h]
        outs.append((yn + bonus) * g[0, h])
    o_ref[0] = jnp.concatenate(outs, axis=1).astype(o_ref.dtype)


def _rw_post(yf, yb, r, k, v, af, ab, g, k_a, r_k, ln_w, ln_b, tm=128):
    b, nh, n, hd = r.shape
    tm = min(tm, n)
    hm = pl.BlockSpec((1, nh, tm, hd), lambda i, t: (i, 0, t, 0))
    par = pl.BlockSpec((nh, 1, hd), lambda i, t: (0, 0, 0))
    return pl.pallas_call(
        _rw_post_kernel,
        grid=(b, n // tm),
        in_specs=[hm] * 8 + [par] * 4,
        out_specs=pl.BlockSpec((1, tm, nh * hd), lambda i, t: (i, t, 0)),
        out_shape=jax.ShapeDtypeStruct((b, n, nh * hd), BF16),
        compiler_params=_params("parallel", "parallel"),
        name="rwkv_post",
    )(yf, yb, r, k, v, af, ab, g, k_a.reshape(nh, 1, hd), r_k.reshape(nh, 1, hd),
      ln_w.reshape(nh, 1, hd), ln_b.reshape(nh, 1, hd))


def _rwkv7_bidir(zb, mu_prev, mu_next, w0, w2, a0, a2, g2, k_k, k_a, r_k, ln_w, ln_b):
    r, k, v, cumf, cumb, af, ab, g = _rw_prep(zb, mu_prev, mu_next, w0, w2, a0, a2, g2)
    yf, yb = _rw_scan(r, k, v, cumf, cumb, af, ab, k_k, k_a)
    return _rw_post(yf, yb, r, k, v, af, ab, g, k_a, r_k, ln_w, ln_b)


def _ffn_half_step(x, g, w1, w3, w2, layer):
    h = _ffn_up(_rmsnorm(x, g, BF16), (w1, layer), (w3, layer))
    return _mm_res([(h, (w2, layer))], x, 0.5)


def _mix_ab(x, g, b, n, cos, sin, w_in, layer, w_out, mu_prev, mu_next, w0, w2, a0, a2, g2, k_k, k_a,
            r_k, ln_w, ln_b):
    qkv = 3 * A_W
    xn = _rmsnorm(x, g, BF16)
    w_b = jnp.pad(w_in[layer, :, qkv:], ((0, 0), (0, RW_PAD_COLS - (w_in.shape[2] - qkv))))
    za = _proj(xn, (w_in, layer), qkv, tn=qkv // 3).reshape(b, n, qkv)
    zb = _proj(xn, w_b, RW_PAD_COLS, tn=RW_PAD_COLS // 2).reshape(b, n, RW_PAD_COLS)
    oa = _attn_a(za, cos, sin).reshape(b * n, A_W)
    ob = _rwkv7_bidir(zb, mu_prev, mu_next, w0, w2, a0, a2, g2, k_k, k_a, r_k, ln_w, ln_b)
    return _mm_res([(oa, w_out[:A_W]), (ob.reshape(b * n, B_W), w_out[A_W:])], x, 1.0)


def _mix_cd(x, g, b, n, cos, sin, w_in, layer, w_out, sink, rpb):
    cols = w_in.shape[2]
    zc = _proj(_rmsnorm(x, g, BF16), (w_in, layer), cols, tn=cols // 2).reshape(b, n, cols)
    oc = _attn_c(zc, sink, cos, sin).reshape(b * n, C_W)
    od = _attn_d(zc, _na_bias_table(rpb, n // GRID_W)).reshape(b * n, D_W)
    return _mm_res([(oc, w_out[:C_W]), (od, w_out[C_W:])], x, 1.0)


def kernel(x, p, ffn1_norm, ffn1_w1, ffn1_w3, ffn1_w2, mix_norm, ffn2_norm, ffn2_w1, ffn2_w3, ffn2_w2, ple_norm, ple_w_gate, ple_w_proj, ab_w_in, ab_w_out, rw_mu_prev, rw_mu_next, rw_w0, rw_w2, rw_a0, rw_a2, rw_g2, rw_k_k, rw_k_a, rw_r_k, rw_ln_w, rw_ln_b, cd_w_in, cd_w_out, c_sink, d_rpb, final_norm):
    b, n, d = x.shape
    depth = p.shape[0]
    cos, sin = _rope_tables(n)
    x = x.reshape(b * n, d)
    p = p.reshape(depth, b * n, -1)
    for i in range(depth):
        j = i // 2
        x = _ffn_half_step(x, ffn1_norm[i], ffn1_w1, ffn1_w3, ffn1_w2, i)
        if i % 2 == 0:
            x = _mix_ab(x, mix_norm[i], b, n, cos, sin, ab_w_in, j, ab_w_out[j], rw_mu_prev[j],
                        rw_mu_next[j], rw_w0[j], rw_w2[j], rw_a0[j], rw_a2[j], rw_g2[j], rw_k_k[j],
                        rw_k_a[j], rw_r_k[j], rw_ln_w[j], rw_ln_b[j])
        else:
            x = _mix_cd(x, mix_norm[i], b, n, cos, sin, cd_w_in, j, cd_w_out[j], c_sink[j], d_rpb[j])
        x = _ffn_half_step(x, ffn2_norm[i], ffn2_w1, ffn2_w3, ffn2_w2, i)
        x = _ple(x, _rmsnorm(x, ple_norm[i], BF16), (ple_w_gate, i), (p, i), (ple_w_proj, i))
    return _rmsnorm(x, final_norm, F32).reshape(b, n, d)
```

```python
import functools
import math

import jax
import jax.numpy as jnp
from jax import lax
from jax.experimental import pallas as pl
from jax.experimental.pallas import tpu as pltpu

F32 = jnp.float32
BF16 = jnp.bfloat16

HEAD_DIM = 64
LANES = 128
SUBLANES = 8
NORM_EPS = 1e-6
ROPE_THETA = 10000.0
GRID_W = 64
A_HEADS = 12
A_W = A_HEADS * HEAD_DIM
A_HALF = 64
A_DILATIONS = (1, 4, 16)
B_HEADS = 20
B_W = B_HEADS * HEAD_DIM
RW_LORA = 64
RW_GATE_LORA = 192
RW_GN_EPS = 64e-5
RW_DECAY_SCALE = math.exp(-0.5)
RW_PAD_COLS = 3 * B_W + 4 * LANES
C_HEADS = 16
C_KV_HEADS = 2
C_GROUP = C_HEADS // C_KV_HEADS
C_W = C_HEADS * HEAD_DIM
C_HALF = 128
D_HEADS = 16
D_W = D_HEADS * HEAD_DIM
NA_KH = 8
NA_KW = 16
MASKED = -1e30
WKV_CHUNK = 64
VMEM_LIMIT = 56 * 1024 * 1024


def _params(*sem):
    return pltpu.CompilerParams(dimension_semantics=sem, vmem_limit_bytes=VMEM_LIMIT)


def _dot(a, b):
    return jnp.dot(a.astype(BF16), b.astype(BF16), preferred_element_type=F32)


def _dot_nt(a, b):
    return lax.dot_general(a.astype(BF16), b.astype(BF16), (((1,), (1,)), ((), ())),
                           preferred_element_type=F32)


def _dot_tn(a, b):
    return lax.dot_general(a.astype(BF16), b.astype(BF16), (((0,), (0,)), ((), ())),
                           preferred_element_type=F32)


def _each(f, *cols):
    return [f(*xs) for xs in zip(*cols)]


def _rmsnorm_kernel(x_ref, g_ref, o_ref):
    x = x_ref[...]
    ms = jnp.mean(x * x, axis=-1, keepdims=True)
    o_ref[...] = (x * lax.rsqrt(ms + NORM_EPS) * g_ref[...]).astype(o_ref.dtype)


def _rmsnorm(x, g, dtype, tm=512):
    m, d = x.shape
    return pl.pallas_call(
        _rmsnorm_kernel,
        grid=(m // tm,),
        in_specs=[pl.BlockSpec((tm, d), lambda i: (i, 0)),
                  pl.BlockSpec((1, d), lambda i: (0, 0))],
        out_specs=pl.BlockSpec((tm, d), lambda i: (i, 0)),
        out_shape=jax.ShapeDtypeStruct((m, d), dtype),
        compiler_params=_params("parallel"),
        name="rmsnorm",
    )(x, g.reshape(1, d))


def _cast_weight(w_ref, wb_ref):
    rows = 256

    @pl.when(pl.program_id(1) == 0)
    def _():
        def body(i, carry):
            sl = pl.ds(pl.multiple_of(i * rows, rows), rows)
            wb_ref[sl, :] = w_ref[sl, :].astype(BF16)
            return carry

        lax.fori_loop(0, w_ref.shape[0] // rows, body, 0)


def _row_spec(tm, k):
    return pl.BlockSpec((tm, k), lambda j, i: (i, 0))


def _col_spec(w, k, tn, **kw):
    if isinstance(w, tuple):
        layer = w[1]
        return pl.BlockSpec((None, k, tn), lambda j, i: (layer, 0, j), **kw)
    return pl.BlockSpec((k, tn), lambda j, i: (0, j), **kw)


def _arr(w):
    return w[0] if isinstance(w, tuple) else w


def _tile_spec(tm, tn):
    return pl.BlockSpec((tm, tn), lambda j, i: (i, j))


def _ffn_up_kernel(xn_ref, w1_ref, w3_ref, o_ref, w1b, w3b):
    _cast_weight(w1_ref, w1b)
    _cast_weight(w3_ref, w3b)
    xn = xn_ref[...]
    a = jnp.dot(xn, w1b[...], preferred_element_type=F32)
    b = jnp.dot(xn, w3b[...], preferred_element_type=F32)
    o_ref[...] = (a * jax.nn.sigmoid(a) * b).astype(o_ref.dtype)


def _ffn_up(xn, w1, w3, tm=1024, tn=512):
    m, d = xn.shape
    f = _arr(w1).shape[-1]
    return pl.pallas_call(
        _ffn_up_kernel,
        grid=(f // tn, m // tm),
        in_specs=[_row_spec(tm, d), _col_spec(w1, d, tn), _col_spec(w3, d, tn)],
        out_specs=_tile_spec(tm, tn),
        out_shape=jax.ShapeDtypeStruct((m, f), BF16),
        scratch_shapes=[pltpu.VMEM((d, tn), BF16)] * 2,
        compiler_params=_params("parallel", "arbitrary"),
        name="ffn_up",
    )(xn, _arr(w1), _arr(w3))


def _proj_kernel(xn_ref, w_ref, o_ref, wb):
    _cast_weight(w_ref, wb)
    o_ref[...] = jnp.dot(xn_ref[...], wb[...], preferred_element_type=F32)


def _proj(xn, w, ncols, tn, tm=512):
    m, d = xn.shape
    return pl.pallas_call(
        _proj_kernel,
        grid=(ncols // tn, m // tm),
        in_specs=[_row_spec(tm, d), _col_spec(w, d, tn, pipeline_mode=pl.Buffered(1))],
        out_specs=_tile_spec(tm, tn),
        out_shape=jax.ShapeDtypeStruct((m, ncols), F32),
        scratch_shapes=[pltpu.VMEM((d, tn), BF16)],
        compiler_params=_params("parallel", "arbitrary"),
        name="proj",
    )(xn, _arr(w))


def _ple_kernel(xn_ref, w_ref, x_ref, p_ref, wp_ref, o_ref, wb, wpb):
    _cast_weight(w_ref, wb)
    _cast_weight(wp_ref, wpb)
    gate = jax.nn.sigmoid(jnp.dot(xn_ref[...], wb[...], preferred_element_type=F32))
    emb = jnp.dot(p_ref[...].astype(BF16), wpb[...], preferred_element_type=F32)
    o_ref[...] = x_ref[...] + emb * gate


def _ple(x, xn, w_gate, p, w_proj, tm=512, tn=512):
    m, d = x.shape
    pd = p[0].shape[-1]
    return pl.pallas_call(
        _ple_kernel,
        grid=(d // tn, m // tm),
        in_specs=[_row_spec(tm, d), _col_spec(w_gate, d, tn), _tile_spec(tm, tn),
                  pl.BlockSpec((None, tm, pd), lambda j, i: (p[1], i, 0)), _col_spec(w_proj, pd, tn)],
        out_specs=_tile_spec(tm, tn),
        out_shape=jax.ShapeDtypeStruct((m, d), F32),
        scratch_shapes=[pltpu.VMEM((d, tn), BF16), pltpu.VMEM((pd, tn), BF16)],
        compiler_params=_params("parallel", "arbitrary"),
        name="ple",
    )(xn, _arr(w_gate), x, p[0], _arr(w_proj))


def _mm_res_kernel(*refs, n_pairs, scale):
    res_ref, o_ref = refs[2 * n_pairs], refs[2 * n_pairs + 1]
    scratch = refs[2 * n_pairs + 2:]
    acc = None
    for t in range(n_pairs):
        _cast_weight(refs[2 * t + 1], scratch[t])
        part = jnp.dot(refs[2 * t][...], scratch[t][...], preferred_element_type=F32)
        acc = part if acc is None else acc + part
    o_ref[...] = res_ref[...] + scale * acc


def _mm_res(pairs, res, scale, tm=512, tn=512):
    m, n = res.shape
    in_specs, args, scratch = [], [], []
    for a, w in pairs:
        k = a.shape[1]
        in_specs += [_row_spec(tm, k), _col_spec(w, k, tn)]
        args += [a, _arr(w)]
        scratch.append(pltpu.VMEM((k, tn), BF16))
    in_specs.append(_tile_spec(tm, tn))
    return pl.pallas_call(
        functools.partial(_mm_res_kernel, n_pairs=len(pairs), scale=scale),
        grid=(n // tn, m // tm),
        in_specs=in_specs,
        out_specs=_tile_spec(tm, tn),
        out_shape=jax.ShapeDtypeStruct((m, n), F32),
        scratch_shapes=scratch,
        compiler_params=_params("parallel", "arbitrary"),
        name="mm_res",
    )(*args, res)


def _lane_lo(shape=(1, LANES)):
    return lax.broadcasted_iota(jnp.int32, shape, len(shape) - 1) < HEAD_DIM


def _rope_tables(n):
    half = HEAD_DIM // 2
    inv_freq = ROPE_THETA ** (-jnp.arange(half, dtype=F32) / half)
    ang = jnp.arange(n, dtype=F32)[:, None] * inv_freq[None, :]
    cos, sin = jnp.cos(ang), jnp.sin(ang)
    return (jnp.concatenate([cos, cos, cos, cos], axis=-1),
            jnp.concatenate([-sin, sin, -sin, sin], axis=-1))


def _rope_tile(x, cos, sin):
    lane = lax.broadcasted_iota(jnp.int32, (1, LANES), 1)
    first_half = (lane % HEAD_DIM) < (HEAD_DIM // 2)
    partner = jnp.where(first_half, pltpu.roll(x, LANES - HEAD_DIM // 2, 1),
                        pltpu.roll(x, HEAD_DIM // 2, 1))
    return x * cos + partner * sin


def _attn_a_kernel(q_ref, k_ref, v_ref, cos_ref, sin_ref, o_ref, qs, ks, m_s, l_s, acc_s):
    n = q_ref.shape[1]
    lo = _lane_lo()
    rb = 256

    def rope_body(i, carry):
        sl = pl.ds(pl.multiple_of(i * rb, rb), rb)
        c, s = cos_ref[sl, :], sin_ref[sl, :]
        qs[sl, :] = _rope_tile(q_ref[0, sl, :], c, s) * (HEAD_DIM ** -0.5)
        ks[sl, :] = _rope_tile(k_ref[0, sl, :], c, s)
        return carry

    lax.fori_loop(0, n // rb, rope_body, 0)
    m_s[...] = jnp.full(m_s.shape, MASKED, F32)
    l_s[...] = jnp.zeros(l_s.shape, F32)
    acc_s[...] = jnp.zeros(acc_s.shape, F32)

    group = 4

    for dil in A_DILATIONS:
        m = n // dil
        qb_rows = min(128, m)
        span = min(qb_rows + 2 * A_HALF, m)
        nqb = m // qb_rows

        def blocks(items, dil=dil, m=m, qb_rows=qb_rows, span=span):
            q0 = [qb * qb_rows for _, qb in items]
            k0 = [jnp.clip(t - A_HALF, 0, m - span) for t in q0]
            if dil == 1:
                rows_q = [pl.ds(pl.multiple_of(t, qb_rows), qb_rows) for t in q0]
                rows_k = [pl.ds(pl.multiple_of(t, A_HALF), span) for t in k0]
            else:
                rows_q = [pl.ds(r + t * dil, qb_rows, stride=dil) for (r, _), t in zip(items, q0)]
                rows_k = [pl.ds(r + t * dil, span, stride=dil) for (r, _), t in zip(items, k0)]
            q = [qs[rq, :] for rq in rows_q]
            kb = [ks[rk, :].astype(BF16) for rk in rows_k]
            vb = [v_ref[0, rk, :].astype(BF16) for rk in rows_k]
            m_old = [m_s[rq, :] for rq in rows_q]
            l_old = [l_s[rq, :] for rq in rows_q]
            acc_old = [acc_s[rq, :] for rq in rows_q]
            mask = _each(lambda a, c: jnp.abs(a + lax.broadcasted_iota(jnp.int32, (qb_rows, 1), 0)
                                              - c - lax.broadcasted_iota(jnp.int32, (1, span), 1)) <= A_HALF,
                         q0, k0)
            s0 = _each(lambda q, kb, mk: jnp.where(mk, _dot_nt(jnp.where(lo, q, 0.0), kb), MASKED), q, kb, mask)
            s1 = _each(lambda q, kb, mk: jnp.where(mk, _dot_nt(jnp.where(lo, 0.0, q), kb), MASKED), q, kb, mask)
            mn0 = _each(lambda mo, s: jnp.maximum(mo[:, :1], jnp.max(s, axis=1, keepdims=True)), m_old, s0)
            mn1 = _each(lambda mo, s: jnp.maximum(mo[:, LANES - 1:], jnp.max(s, axis=1, keepdims=True)), m_old, s1)
            p0 = _each(lambda s, mn: jnp.exp(s - mn), s0, mn0)
            p1 = _each(lambda s, mn: jnp.exp(s - mn), s1, mn1)
            m_new = _each(lambda a, c: jnp.where(lo, a, c), mn0, mn1)
            alpha = _each(lambda mo, mn: jnp.exp(mo - mn), m_old, m_new)
            psum = _each(lambda a, c: jnp.where(lo, jnp.sum(a, axis=1, keepdims=True),
                                                jnp.sum(c, axis=1, keepdims=True)), p0, p1)
            pv = _each(lambda a, c, vb: jnp.where(lo, _dot(a, vb), _dot(c, vb)), p0, p1, vb)
            for rq, mn, al, lold, ps, ao, pvi in zip(rows_q, m_new, alpha, l_old, psum, acc_old, pv):
                m_s[rq, :] = mn
                l_s[rq, :] = al * lold + ps
                acc_s[rq, :] = al * ao + pvi

        if dil >= group:
            def res_body(rg, carry, blocks=blocks, nqb=nqb):
                def qb_body(qb, c2):
                    blocks([(rg * group + u, qb) for u in range(group)])
                    return c2

                return lax.fori_loop(0, nqb, qb_body, carry)

            lax.fori_loop(0, dil // group, res_body, 0)
        else:
            def qb_body(qg, carry, blocks=blocks):
                blocks([(0, qg * group + u) for u in range(group)])
                return carry

            lax.fori_loop(0, nqb // group, qb_body, 0)

    def out_body(i, carry):
        sl = pl.ds(pl.multiple_of(i * rb, rb), rb)
        o_ref[0, sl, :] = (acc_s[sl, :] / l_s[sl, :]).astype(o_ref.dtype)
        return carry

    lax.fori_loop(0, n // rb, out_body, 0)


def _attn_a(za, cos, sin):
    b, n, _ = za.shape
    npair = A_W // LANES
    blk = lambda off: pl.BlockSpec((1, n, LANES), lambda i, j, off=off: (i, 0, off + j))
    tab = pl.BlockSpec((n, LANES), lambda i, j: (0, 0))
    return pl.pallas_call(
        _attn_a_kernel,
        grid=(b, npair),
        in_specs=[blk(0), blk(npair), blk(2 * npair), tab, tab],
        out_specs=pl.BlockSpec((1, n, LANES), lambda i, j: (i, 0, j)),
        out_shape=jax.ShapeDtypeStruct((b, n, A_W), BF16),
        scratch_shapes=[pltpu.VMEM((n, LANES), F32)] * 5,
        compiler_params=_params("parallel", "parallel"),
        name="attn_dilated",
    )(za, za, za, cos, sin)


def _attn_c_kernel(sink_ref, q_ref, k_ref, v_ref, cosq_ref, sinq_ref, cos_ref, sin_ref, o_ref,
                   klo, khi, vlo, vhi):
    n = k_ref.shape[1]
    qb_rows = q_ref.shape[1]
    span = qb_rows + 2 * C_HALF
    hk = pl.program_id(1)
    qb = pl.program_id(2)
    lo = _lane_lo()

    @pl.when(qb == 0)
    def _():
        rb = 256

        def body(i, carry):
            sl = pl.ds(pl.multiple_of(i * rb, rb), rb)
            kx = _rope_tile(k_ref[0, sl, :], cos_ref[sl, :], sin_ref[sl, :])
            kw = pltpu.roll(kx, HEAD_DIM, 1)
            vx = v_ref[0, sl, :]
            vw = pltpu.roll(vx, HEAD_DIM, 1)
            first = hk == 0
            klo[sl, :] = jnp.where(lo, jnp.where(first, kx, kw), 0.0).astype(BF16)
            khi[sl, :] = jnp.where(lo, 0.0, jnp.where(first, kw, kx)).astype(BF16)
            vlo[sl, :] = jnp.where(lo, jnp.where(first, vx, vw), 0.0).astype(BF16)
            vhi[sl, :] = jnp.where(lo, 0.0, jnp.where(first, vw, vx)).astype(BF16)
            return carry

        lax.fori_loop(0, n // rb, body, 0)

    q0 = qb * qb_rows
    k0 = pl.multiple_of(jnp.clip(q0 - C_HALF, 0, n - span), C_HALF)
    rows_k = pl.ds(k0, span)
    kl, kh = klo[rows_k, :], khi[rows_k, :]
    vl, vh = vlo[rows_k, :], vhi[rows_k, :]
    qpos = q0 + lax.broadcasted_iota(jnp.int32, (qb_rows, 1), 0)
    kpos = k0 + lax.broadcasted_iota(jnp.int32, (1, span), 1)
    mask = jnp.abs(qpos - kpos) <= C_HALF
    cq, sq = cosq_ref[...], sinq_ref[...]

    ntile = q_ref.shape[2] // LANES
    qt = [_rope_tile(q_ref[0, :, j * LANES:(j + 1) * LANES], cq, sq) * (HEAD_DIM ** -0.5) for j in range(ntile)]
    qt = [t for t in qt for _ in range(2)]
    kk = [kl, kh] * ntile
    sk = [sink_ref[hk * C_GROUP + h] for h in range(2 * ntile)]
    s = _each(lambda q, k: jnp.where(mask, _dot_nt(q, k), MASKED), qt, kk)
    mx = _each(lambda s, sk: jnp.maximum(jnp.max(s, axis=1, keepdims=True), sk), s, sk)
    p = _each(lambda s, mx: jnp.exp(s - mx), s, mx)
    den = _each(lambda p, sk, mx: jnp.sum(p, axis=1, keepdims=True) + jnp.exp(sk - mx), p, sk, mx)
    for j in range(ntile):
        pv = _dot(p[2 * j], vl) + _dot(p[2 * j + 1], vh)
        o_ref[0, :, j * LANES:(j + 1) * LANES] = (pv / jnp.where(lo, den[2 * j], den[2 * j + 1])).astype(o_ref.dtype)


def _attn_c(zc, sink, cos, sin, qb_rows=128):
    b, n, _ = zc.shape
    gw = C_GROUP * HEAD_DIM
    kcol = C_W // LANES
    full = lambda off: pl.BlockSpec((1, n, LANES), lambda i, h, t, off=off: (i, 0, off))
    tabq = pl.BlockSpec((qb_rows, LANES), lambda i, h, t: (t, 0))
    tab = pl.BlockSpec((n, LANES), lambda i, h, t: (0, 0))
    return pl.pallas_call(
        _attn_c_kernel,
        grid=(b, C_KV_HEADS, n // qb_rows),
        in_specs=[pl.BlockSpec(memory_space=pltpu.SMEM),
                  pl.BlockSpec((1, qb_rows, gw), lambda i, h, t: (i, t, h)),
                  full(kcol), full(kcol + 1), tabq, tabq, tab, tab],
        out_specs=pl.BlockSpec((1, qb_rows, gw), lambda i, h, t: (i, t, h)),
        out_shape=jax.ShapeDtypeStruct((b, n, C_W), BF16),
        scratch_shapes=[pltpu.VMEM((n, LANES), BF16)] * 4,
        compiler_params=_params("parallel", "parallel", "arbitrary"),
        name="attn_gqa_sink",
    )(sink.reshape(-1), zc, zc, zc, cos, sin, cos, sin)


def _na_bias_table(rpb, rows):
    kh = min(NA_KH, rows)
    qc = jnp.arange(GRID_W)[:, None]
    kc = jnp.arange(GRID_W)[None, :]
    qc0 = jnp.clip(qc - NA_KW // 2, 0, GRID_W - NA_KW)
    valid = (kc >= qc0) & (kc < qc0 + NA_KW)
    ci = jnp.clip(kc - qc, -(NA_KW - 1), NA_KW - 1) + NA_KW - 1
    pairs = rpb.astype(F32).reshape(D_HEADS // 2, 2, 2 * NA_KH - 1, 2 * NA_KW - 1)
    by_row = jnp.stack([pairs[:, :, off:off + kh] for off in range(NA_KH)], axis=1)
    onehot = (ci[None] == jnp.arange(2 * NA_KW - 1)[:, None, None]).astype(F32)
    bias = jnp.einsum('poekc,cqj->poeqkj', by_row, onehot, precision=lax.Precision.HIGHEST)
    bias = jnp.where(valid[:, None, :], bias, MASKED)
    return bias.reshape(D_HEADS // 2, NA_KH, 2 * GRID_W, kh * GRID_W)


def _attn_d_kernel(q_ref, k_ref, v_ref, bias_ref, o_ref):
    n = q_ref.shape[1]
    rows = n // GRID_W
    kh = min(NA_KH, rows)
    lo = _lane_lo()
    group = 4 if rows % 4 == 0 else 1

    def body(rg, carry):
        r = [rg * group + u for u in range(group)]
        r0 = [jnp.clip(t - NA_KH // 2, 0, rows - kh) for t in r]
        rows_q = [pl.ds(pl.multiple_of(t * GRID_W, GRID_W), GRID_W) for t in r]
        rows_k = [pl.ds(pl.multiple_of(t * GRID_W, GRID_W), kh * GRID_W) for t in r0]
        q = [q_ref[0, rq, :] * (HEAD_DIM ** -0.5) for rq in rows_q]
        q2 = _each(lambda q: jnp.concatenate([jnp.where(lo, q, 0.0), jnp.where(lo, 0.0, q)], axis=0), q)
        s = [_dot_nt(q2i, k_ref[0, rk, :]) + bias_ref[0, t0 - t + NA_KH - 1]
             for q2i, rk, t0, t in zip(q2, rows_k, r0, r)]
        p = _each(lambda s: jnp.exp(s - jnp.max(s, axis=1, keepdims=True)), s)
        pv = [_dot(pi, v_ref[0, rk, :]) / jnp.sum(pi, axis=1, keepdims=True) for pi, rk in zip(p, rows_k)]
        for rq, pvi in zip(rows_q, pv):
            o_ref[0, rq, :] = jnp.where(lo, pvi[:GRID_W], pvi[GRID_W:]).astype(o_ref.dtype)
        return carry

    lax.fori_loop(0, rows // group, body, 0)


def _attn_d(zc, bias):
    b, n, _ = zc.shape
    npair = D_W // LANES
    q_off = (C_W + 2 * C_KV_HEADS * HEAD_DIM) // LANES
    blk = lambda off: pl.BlockSpec((1, n, LANES), lambda i, j, off=off: (i, 0, off + j))
    return pl.pallas_call(
        _attn_d_kernel,
        grid=(b, npair),
        in_specs=[blk(q_off), blk(q_off + npair), blk(q_off + 2 * npair),
                  pl.BlockSpec((1,) + bias.shape[1:], lambda i, j: (j, 0, 0, 0))],
        out_specs=pl.BlockSpec((1, n, LANES), lambda i, j: (i, 0, j)),
        out_shape=jax.ShapeDtypeStruct((b, n, D_W), BF16),
        compiler_params=_params("parallel", "parallel"),
        name="attn_neighborhood",
    )(zc, zc, zc, bias)


def _rw_prep_kernel(z_ref, hp_ref, hn_ref, mup_ref, mun_ref, w0_ref, w2f_ref, w2b_ref, a0_ref,
                    a2f_ref, a2b_ref, g2_ref, r_o, k_o, v_o, cumf_o, cumb_o, af_o, ab_o, g_o):
    tm = z_ref.shape[1]
    first = pl.program_id(1) == 0
    last = pl.program_id(1) == pl.num_programs(1) - 1
    row = lax.broadcasted_iota(jnp.int32, (tm, 1), 0)
    crow = row % WKV_CHUNK

    def edges(cols):
        return (jnp.where(first, 0.0, hp_ref[0, SUBLANES - 1:SUBLANES, cols]),
                jnp.where(last, 0.0, hn_ref[0, 0:1, cols]))

    def mix(z, z_prev, z_next, cols):
        return z + mup_ref[:, cols] * (z_prev - z) + mun_ref[:, cols] * (z_next - z)

    def shifted(cols):
        z = z_ref[0, :, cols]
        edge_prev, edge_next = edges(cols)
        z_prev = jnp.where(row == 0, edge_prev, pltpu.roll(z, 1, 0))
        z_next = jnp.where(row == tm - 1, edge_next, pltpu.roll(z, tm - 1, 0))
        return mix(z, z_prev, z_next, cols)

    def store_shifted(o_ref, cs, cols):
        z = z_ref[0, :, cols]
        edge_prev, edge_next = edges(cols)
        o_ref[0, :, cs] = mix(z, pltpu.roll(z, 1, 0), pltpu.roll(z, tm - 1, 0), cols)
        o_ref[0, 0:1, cs] = mix(z[0:1], edge_prev, z[1:2], cols)
        o_ref[0, tm - 1:tm, cs] = mix(z[tm - 1:tm], z[tm - 2:tm - 1], edge_next, cols)

    def log_decay(x):
        return -RW_DECAY_SCALE * jax.nn.sigmoid(x)

    def chunk_cumsum(x, rev):
        s = 1
        while s < WKV_CHUNK:
            if rev:
                x = x + jnp.where(crow < WKV_CHUNK - s, pltpu.roll(x, tm - s, 0), 0.0)
            else:
                x = x + jnp.where(crow >= s, pltpu.roll(x, s, 0), 0.0)
            s *= 2
        return x

    base = 3 * B_W
    wl = jnp.tanh(shifted(slice(base, base + LANES))).astype(BF16)
    al = shifted(slice(base + LANES, base + 2 * LANES)).astype(BF16)
    gl = jax.nn.sigmoid(shifted(slice(base + 2 * LANES, base + 4 * LANES))).astype(BF16)

    for s0 in range(0, B_W, LANES):
        cs = slice(s0, s0 + LANES)
        store_shifted(r_o, cs, cs)
        store_shifted(k_o, cs, slice(B_W + s0, B_W + s0 + LANES))
        store_shifted(v_o, cs, slice(2 * B_W + s0, 2 * B_W + s0 + LANES))
        cumf_o[0, :, cs] = chunk_cumsum(log_decay(w0_ref[0:1, cs] + _dot(wl, w2f_ref[:, cs])), False)
        cumb_o[0, :, cs] = chunk_cumsum(log_decay(w0_ref[1:2, cs] + _dot(wl, w2b_ref[:, cs])), True)
        af_o[0, :, cs] = jax.nn.sigmoid(a0_ref[0:1, cs] + _dot(al, a2f_ref[:, cs]))
        ab_o[0, :, cs] = jax.nn.sigmoid(a0_ref[1:2, cs] + _dot(al, a2b_ref[:, cs]))
        g_o[0, :, cs] = _dot(gl, g2_ref[:, cs])


def _rw_prep(zb, mu_prev, mu_next, w0, w2, a0, a2, g2, tm=256):
    b, n, cols = zb.shape
    tm = min(tm, n)
    per = tm // SUBLANES
    pad_cols = lambda t: jnp.pad(t, (0, cols - t.shape[0])).reshape(1, cols)
    rows_f = lambda t: jnp.pad(t, ((0, LANES - RW_LORA), (0, 0))).astype(BF16)
    rows_b = lambda t: jnp.pad(t, ((RW_LORA, LANES - 2 * RW_LORA), (0, 0))).astype(BF16)
    g2p = jnp.pad(g2, ((0, 2 * LANES - RW_GATE_LORA), (0, 0))).astype(BF16)
    const = lambda shape: pl.BlockSpec(shape, lambda i, t: (0,) * len(shape))
    halo_prev = pl.BlockSpec((1, SUBLANES, cols), lambda i, t: (i, jnp.maximum(t * per - 1, 0), 0))
    halo_next = pl.BlockSpec((1, SUBLANES, cols), lambda i, t: (i, jnp.minimum((t + 1) * per, n // SUBLANES - 1), 0))
    tok = jax.ShapeDtypeStruct((b, n, B_W), F32)
    return pl.pallas_call(
        _rw_prep_kernel,
        grid=(b, n // tm),
        in_specs=[pl.BlockSpec((1, tm, cols), lambda i, t: (i, t, 0)), halo_prev, halo_next,
                  const((1, cols)), const((1, cols)), const((2, B_W)), const((LANES, B_W)),
                  const((LANES, B_W)), const((2, B_W)), const((LANES, B_W)), const((LANES, B_W)),
                  const((2 * LANES, B_W))],
        out_specs=[pl.BlockSpec((1, tm, B_W), lambda i, t: (i, t, 0))] * 8,
        out_shape=[tok] * 8,
        compiler_params=_params("parallel", "parallel"),
        name="rwkv_prep",
    )(zb, zb, zb, pad_cols(mu_prev), pad_cols(mu_next), w0, rows_f(w2[0]), rows_b(w2[1]),
      a0, rows_f(a2[0]), rows_b(a2[1]), g2p)


def _head_sum(x, lo):
    return jnp.where(lo, jnp.sum(jnp.where(lo, x, 0.0), axis=1, keepdims=True),
                     jnp.sum(jnp.where(lo, 0.0, x), axis=1, keepdims=True))


def _wkv_chunks(chains):
    c = chains[0][0].shape[0]
    ii = lax.broadcasted_iota(jnp.int32, (c, c), 0)
    jj = lax.broadcasted_iota(jnp.int32, (c, c), 1)
    row = lax.broadcasted_iota(jnp.int32, (c, 1), 0)
    eye_f = (ii == jj).astype(F32)
    masks = {False: (jj <= ii, jj < ii), True: (jj >= ii, jj > ii)}
    lo = _lane_lo()
    ki = lax.broadcasted_iota(jnp.int32, (LANES, LANES), 0)
    vj = lax.broadcasted_iota(jnp.int32, (LANES, LANES), 1)
    same_head = (ki < HEAD_DIM) == (vj < HEAD_DIM)
    diag = ki == vj
    r, k, v, cum, rate, st, k_k, k_a, rev = (list(col) for col in zip(*chains))
    twice = lambda xs: [x for x in xs for _ in range(2)]
    merge = lambda xs: [jnp.where(lo, xs[2 * i], xs[2 * i + 1]) for i in range(len(xs) // 2)]
    incl = twice([masks[x][0] for x in rev])
    strict = twice([masks[x][1] for x in rev])

    kk = _each(lambda k, k_k: k * k_k, k, k_k)
    kk = _each(lambda t: t / jnp.maximum(jnp.sqrt(_head_sum(t * t, lo)), 1e-12), kk)
    kd = _each(lambda k, rate, k_a: k * (1.0 + (rate - 1.0) * k_a), k, rate, k_a)
    b = _each(lambda kk, rate: kk * rate, kk, rate)

    tot = _each(lambda cum, x: cum[0:1] if x else cum[c - 1:c], cum, rev)
    excl = _each(lambda cum, x: jnp.where(row == c - 1, 0.0, pltpu.roll(cum, c - 1, 0)) if x
                 else jnp.where(row == 0, 0.0, pltpu.roll(cum, 1, 0)), cum, rev)
    ar = _each(lambda kk, r, cum, ex: jnp.concatenate([-kk * jnp.exp(ex), r * jnp.exp(cum)], axis=0),
               kk, r, cum, excl)
    ar_h = [jnp.where(m, x, 0.0).astype(BF16) for x in ar for m in (lo, ~lo)]
    ar = _each(lambda t: t.astype(BF16), ar)
    v = _each(lambda t: t.astype(BF16), v)
    g_inv = _each(lambda cum: jnp.exp(-cum), cum)
    bg = twice(_each(lambda b, g: (b * g).astype(BF16), b, g_inv))
    kg = twice(_each(lambda kd, g: (kd * g).astype(BF16), kd, g_inv))
    x1 = _each(_dot_nt, ar_h, bg)
    x2 = _each(_dot_nt, ar_h, kg)
    a_ab = _each(lambda x, m: jnp.where(m, x[:c], 0.0), x1, strict)
    a_rb = _each(lambda x, m: jnp.where(m, x[c:], 0.0).astype(BF16), x1, incl)
    a_k = _each(lambda x, ms, mi: jnp.concatenate([jnp.where(ms, x[:c], 0.0), jnp.where(mi, x[c:], 0.0)],
                                                  axis=0).astype(BF16), x2, strict, incl)

    inv = _each(lambda t: eye_f + t, a_ab)
    pw = _each(lambda t: _dot(t, t), a_ab)
    for _ in range(c.bit_length() - 3):
        pw = _each(lambda t: t.astype(BF16), pw)
        both = _each(lambda inv, pw: _dot(jnp.concatenate([inv.astype(BF16), pw], axis=0), pw), inv, pw)
        inv = _each(lambda inv, t: inv + t[:c], inv, both)
        pw = _each(lambda t: t[c:], both)
    inv = _each(lambda inv, pw: (inv + _dot(inv, pw)).astype(BF16), inv, pw)

    akv = merge(_each(_dot, a_k, twice(v)))
    ars = _each(_dot, ar, st)
    rhs = _each(lambda ars, akv: (ars[:c] + akv[:c]).astype(BF16), ars, akv)
    p = _each(lambda t: t.astype(BF16), merge(_each(_dot, inv, twice(rhs))))
    y = _each(lambda ars, arb, akv: ars[c:] + arb + akv[c:], ars, merge(_each(_dot, a_rb, twice(p))), akv)
    g_end = _each(lambda tot, cum: jnp.exp(tot - cum), tot, cum)
    upd = _each(lambda b, kd, g, p, v: _dot_tn(jnp.concatenate([b * g, kd * g], axis=0),
                                               jnp.concatenate([p, v], axis=0)), b, kd, g_end, p, v)
    g_col = _each(lambda tot: jnp.sum(jnp.where(diag, jnp.exp(tot), 0.0), axis=1, keepdims=True), tot)
    new_st = _each(lambda g, st, upd: jnp.where(same_head, g * st + upd, 0.0), g_col, st, upd)
    return list(zip(y, new_st))


def _rw_scan_kernel(rf, kf, vf, cumf, af, rb, kb, vb, cumb, ab, kk_ref, ka_ref, yf, yb, sf, sb):
    tb = rf.shape[1]
    npair = rf.shape[2] // LANES
    c = WKV_CHUNK
    nch = tb // c
    pair = lambda j: slice(j * LANES, (j + 1) * LANES)

    @pl.when(pl.program_id(2) == 0)
    def _():
        sf[...] = jnp.zeros(sf.shape, F32)
        sb[...] = jnp.zeros(sb.shape, F32)

    def body(ci, carry):
        rows_f = pl.ds(pl.multiple_of(ci * c, c), c)
        rows_b = pl.ds(pl.multiple_of((nch - 1 - ci) * c, c), c)
        chains = []
        for j in range(npair):
            par = (kk_ref[:, pair(j)], ka_ref[:, pair(j)])
            chains.append(tuple(ref[0, rows_f, pair(j)] for ref in (rf, kf, vf, cumf, af)) + (sf[j],) + par + (False,))
            chains.append(tuple(ref[0, rows_b, pair(j)] for ref in (rb, kb, vb, cumb, ab)) + (sb[j],) + par + (True,))
        outs = _wkv_chunks(chains)
        for j in range(npair):
            yf[0, rows_f, pair(j)], sf[j] = outs[2 * j]
            yb[0, rows_b, pair(j)], sb[j] = outs[2 * j + 1]
        return carry

    lax.fori_loop(0, nch, body, 0)


def _rw_scan(r, k, v, cumf, cumb, af, ab, k_k, k_a, hb=10, tb=256):
    b, n, width = r.shape
    tb = min(tb, n)
    nt = n // tb
    hw = hb * HEAD_DIM
    fwd = pl.BlockSpec((1, tb, hw), lambda i, g, t: (i, t, g))
    bwd = pl.BlockSpec((1, tb, hw), lambda i, g, t: (i, nt - 1 - t, g))
    par = pl.BlockSpec((1, hw), lambda i, g, t: (0, g))
    out = jax.ShapeDtypeStruct((b, n, width), F32)
    return pl.pallas_call(
        _rw_scan_kernel,
        grid=(b, width // hw, nt),
        in_specs=[fwd] * 5 + [bwd] * 5 + [par, par],
        out_specs=[fwd, bwd],
        out_shape=[out, out],
        scratch_shapes=[pltpu.VMEM((hw // LANES, LANES, LANES), F32)] * 2,
        compiler_params=_params("parallel", "parallel", "arbitrary"),
        name="rwkv_scan",
    )(r, k, v, cumf, af, r, k, v, cumb, ab, k_k.reshape(1, width), k_a.reshape(1, width))


def _rw_post_kernel(yf, yb, r, k, v, af, ab, g, ka_ref, rk_ref, lnw_ref, lnb_ref, o_ref):
    lo = _lane_lo()
    head_sum = lambda x: _head_sum(x, lo)
    for j in range(o_ref.shape[2] // LANES):
        sl = slice(j * LANES, (j + 1) * LANES)
        y = yf[0, :, sl] + yb[0, :, sl]
        dev = y - head_sum(y) * (1.0 / HEAD_DIM)
        var = head_sum(dev * dev) * (1.0 / HEAD_DIM)
        yn = dev * lax.rsqrt(var + RW_GN_EPS) * lnw_ref[:, sl] + lnb_ref[:, sl]
        kt, k_a = k[0, :, sl], ka_ref[:, sl]
        kd = kt * (1.0 + (af[0, :, sl] - 1.0) * k_a) + kt * (1.0 + (ab[0, :, sl] - 1.0) * k_a)
        bonus = head_sum(r[0, :, sl] * kd * rk_ref[:, sl]) * v[0, :, sl]
        o_ref[0, :, sl] = ((yn + bonus) * g[0, :, sl]).astype(o_ref.dtype)


def _rw_post(yf, yb, r, k, v, af, ab, g, k_a, r_k, ln_w, ln_b, tm=256):
    b, n, width = r.shape
    tm = min(tm, n)
    tok = pl.BlockSpec((1, tm, width), lambda i, t: (i, t, 0))
    par = pl.BlockSpec((1, width), lambda i, t: (0, 0))
    return pl.pallas_call(
        _rw_post_kernel,
        grid=(b, n // tm),
        in_specs=[tok] * 8 + [par] * 4,
        out_specs=tok,
        out_shape=jax.ShapeDtypeStruct((b, n, width), BF16),
        compiler_params=_params("parallel", "parallel"),
        name="rwkv_post",
    )(yf, yb, r, k, v, af, ab, g, k_a.reshape(1, width), r_k.reshape(1, width),
      ln_w.reshape(1, width), ln_b.reshape(1, width))


def _rwkv7_bidir(zb, mu_prev, mu_next, w0, w2, a0, a2, g2, k_k, k_a, r_k, ln_w, ln_b):
    r, k, v, cumf, cumb, af, ab, g = _rw_prep(zb, mu_prev, mu_next, w0, w2, a0, a2, g2)
    yf, yb = _rw_scan(r, k, v, cumf, cumb, af, ab, k_k, k_a)
    return _rw_post(yf, yb, r, k, v, af, ab, g, k_a, r_k, ln_w, ln_b)


def _ffn_half_step(x, g, w1, w3, w2, layer):
    h = _ffn_up(_rmsnorm(x, g, BF16), (w1, layer), (w3, layer))
    return _mm_res([(h, (w2, layer))], x, 0.5)


def _mix_ab(x, g, b, n, cos, sin, w_in, layer, w_out, mu_prev, mu_next, w0, w2, a0, a2, g2, k_k, k_a,
            r_k, ln_w, ln_b):
    qkv = 3 * A_W
    xn = _rmsnorm(x, g, BF16)
    w_b = jnp.pad(w_in[layer, :, qkv:], ((0, 0), (0, RW_PAD_COLS - (w_in.shape[2] - qkv))))
    za = _proj(xn, (w_in, layer), qkv, tn=qkv // 3).reshape(b, n, qkv)
    zb = _proj(xn, w_b, RW_PAD_COLS, tn=RW_PAD_COLS // 2).reshape(b, n, RW_PAD_COLS)
    oa = _attn_a(za, cos, sin).reshape(b * n, A_W)
    ob = _rwkv7_bidir(zb, mu_prev, mu_next, w0, w2, a0, a2, g2, k_k, k_a, r_k, ln_w, ln_b)
    return _mm_res([(oa, w_out[:A_W]), (ob.reshape(b * n, B_W), w_out[A_W:])], x, 1.0)


def _mix_cd(x, g, b, n, cos, sin, w_in, layer, w_out, sink, rpb):
    cols = w_in.shape[2]
    zc = _proj(_rmsnorm(x, g, BF16), (w_in, layer), cols, tn=cols // 2).reshape(b, n, cols)
    oc = _attn_c(zc, sink, cos, sin).reshape(b * n, C_W)
    od = _attn_d(zc, _na_bias_table(rpb, n // GRID_W)).reshape(b * n, D_W)
    return _mm_res([(oc, w_out[:C_W]), (od, w_out[C_W:])], x, 1.0)


def kernel(x, p, ffn1_norm, ffn1_w1, ffn1_w3, ffn1_w2, mix_norm, ffn2_norm, ffn2_w1, ffn2_w3, ffn2_w2, ple_norm, ple_w_gate, ple_w_proj, ab_w_in, ab_w_out, rw_mu_prev, rw_mu_next, rw_w0, rw_w2, rw_a0, rw_a2, rw_g2, rw_k_k, rw_k_a, rw_r_k, rw_ln_w, rw_ln_b, cd_w_in, cd_w_out, c_sink, d_rpb, final_norm):
    b, n, d = x.shape
    depth = p.shape[0]
    cos, sin = _rope_tables(n)
    x = x.reshape(b * n, d)
    p = p.reshape(depth, b * n, -1)
    for i in range(depth):
        j = i // 2
        x = _ffn_half_step(x, ffn1_norm[i], ffn1_w1, ffn1_w3, ffn1_w2, i)
        if i % 2 == 0:
            x = _mix_ab(x, mix_norm[i], b, n, cos, sin, ab_w_in, j, ab_w_out[j], rw_mu_prev[j],
                        rw_mu_next[j], rw_w0[j], rw_w2[j], rw_a0[j], rw_a2[j], rw_g2[j], rw_k_k[j],
                        rw_k_a[j], rw_r_k[j], rw_ln_w[j], rw_ln_b[j])
        else:
            x = _mix_cd(x, mix_norm[i], b, n, cos, sin, cd_w_in, j, cd_w_out[j], c_sink[j], d_rpb[j])
        x = _ffn_half_step(x, ffn2_norm[i], ffn2_w1, ffn2_w3, ffn2_w2, i)
        x = _ple(x, _rmsnorm(x, ple_norm[i], BF16), (ple_w_gate, i), (p, i), (ple_w_proj, i))
    return _rmsnorm(x, final_norm, F32).reshape(b, n, d)
```

```python
import functools
import math

import jax
import jax.numpy as jnp
from jax import lax
from jax.experimental import pallas as pl
from jax.experimental.pallas import tpu as pltpu

F32 = jnp.float32
BF16 = jnp.bfloat16

HEAD_DIM = 64
LANES = 128
SUBLANES = 8
NORM_EPS = 1e-6
ROPE_THETA = 10000.0
GRID_W = 64
A_HEADS = 12
A_W = A_HEADS * HEAD_DIM
A_HALF = 64
A_DILATIONS = (1, 4, 16)
A_QUERY_ROWS = 64
B_HEADS = 20
B_W = B_HEADS * HEAD_DIM
RW_LORA = 64
RW_GATE_LORA = 192
RW_GN_EPS = 64e-5
RW_DECAY_SCALE = math.exp(-0.5)
RW_PAD_COLS = 3 * B_W + 4 * LANES
C_HEADS = 16
C_KV_HEADS = 2
C_GROUP = C_HEADS // C_KV_HEADS
C_W = C_HEADS * HEAD_DIM
C_HALF = 128
D_HEADS = 16
D_W = D_HEADS * HEAD_DIM
NA_KH = 8
NA_KW = 16
MASKED = -1e30
WKV_CHUNK = 64
VMEM_LIMIT = 56 * 1024 * 1024


def _params(*sem):
    return pltpu.CompilerParams(dimension_semantics=sem, vmem_limit_bytes=VMEM_LIMIT)


def _dot(a, b):
    return jnp.dot(a.astype(BF16), b.astype(BF16), preferred_element_type=F32)


def _dot_nt(a, b):
    return lax.dot_general(a.astype(BF16), b.astype(BF16), (((1,), (1,)), ((), ())),
                           preferred_element_type=F32)


def _dot_tn(a, b):
    return lax.dot_general(a.astype(BF16), b.astype(BF16), (((0,), (0,)), ((), ())),
                           preferred_element_type=F32)


def _each(f, *cols):
    return [f(*xs) for xs in zip(*cols)]


def _rmsnorm_kernel(x_ref, g_ref, o_ref):
    x = x_ref[...]
    ms = jnp.mean(x * x, axis=-1, keepdims=True)
    o_ref[...] = (x * lax.rsqrt(ms + NORM_EPS) * g_ref[...]).astype(o_ref.dtype)


def _rmsnorm(x, g, dtype, tm=512):
    m, d = x.shape
    return pl.pallas_call(
        _rmsnorm_kernel,
        grid=(m // tm,),
        in_specs=[pl.BlockSpec((tm, d), lambda i: (i, 0)),
                  pl.BlockSpec((1, d), lambda i: (0, 0))],
        out_specs=pl.BlockSpec((tm, d), lambda i: (i, 0)),
        out_shape=jax.ShapeDtypeStruct((m, d), dtype),
        compiler_params=_params("parallel"),
        name="rmsnorm",
    )(x, g.reshape(1, d))


def _cast_weight(w_ref, wb_ref):
    rows = 256

    @pl.when(pl.program_id(1) == 0)
    def _():
        def body(i, carry):
            sl = pl.ds(pl.multiple_of(i * rows, rows), rows)
            wb_ref[sl, :] = w_ref[sl, :].astype(BF16)
            return carry

        lax.fori_loop(0, w_ref.shape[0] // rows, body, 0)


def _row_spec(tm, k):
    return pl.BlockSpec((tm, k), lambda j, i: (i, 0))


def _col_spec(w, k, tn, **kw):
    if isinstance(w, tuple):
        layer = w[1]
        return pl.BlockSpec((None, k, tn), lambda j, i: (layer, 0, j), **kw)
    return pl.BlockSpec((k, tn), lambda j, i: (0, j), **kw)


def _arr(w):
    return w[0] if isinstance(w, tuple) else w


def _tile_spec(tm, tn):
    return pl.BlockSpec((tm, tn), lambda j, i: (i, j))


def _normed(x_ref, ss_ref, g_ref):
    ss = ss_ref[0, :, :1]
    for t in range(1, ss_ref.shape[0]):
        ss = ss + ss_ref[t, :, :1]
    rs = lax.rsqrt(ss * (1.0 / x_ref.shape[1]) + NORM_EPS)
    return (x_ref[...] * rs * g_ref[...]).astype(BF16)


def _ss_spec(nt, tm):
    return pl.BlockSpec((nt, tm, LANES), lambda j, i: (0, i, 0))


def _gain_spec(d):
    return pl.BlockSpec((1, d), lambda j, i: (0, 0))


def _ffn_up_kernel(x_ref, ss_ref, g_ref, w1_ref, w3_ref, o_ref, w1b, w3b):
    _cast_weight(w1_ref, w1b)
    _cast_weight(w3_ref, w3b)
    xn = _normed(x_ref, ss_ref, g_ref)
    a = jnp.dot(xn, w1b[...], preferred_element_type=F32)
    b = jnp.dot(xn, w3b[...], preferred_element_type=F32)
    o_ref[...] = (a * jax.nn.sigmoid(a) * b).astype(o_ref.dtype)


def _ffn_up(x, ss, g, w1, w3, tm=1024, tn=512):
    m, d = x.shape
    f = _arr(w1).shape[-1]
    return pl.pallas_call(
        _ffn_up_kernel,
        grid=(f // tn, m // tm),
        in_specs=[_row_spec(tm, d), _ss_spec(ss.shape[0], tm), _gain_spec(d), _col_spec(w1, d, tn),
                  _col_spec(w3, d, tn)],
        out_specs=_tile_spec(tm, tn),
        out_shape=jax.ShapeDtypeStruct((m, f), BF16),
        scratch_shapes=[pltpu.VMEM((d, tn), BF16)] * 2,
        compiler_params=_params("parallel", "arbitrary"),
        name="ffn_up",
    )(x, ss, g.reshape(1, d), _arr(w1), _arr(w3))


def _proj_kernel(x_ref, ss_ref, g_ref, w_ref, o_ref, wb):
    _cast_weight(w_ref, wb)
    o_ref[...] = jnp.dot(_normed(x_ref, ss_ref, g_ref), wb[...], preferred_element_type=F32)


def _proj(x, ss, g, w, ncols, tn, tm=512):
    m, d = x.shape
    return pl.pallas_call(
        _proj_kernel,
        grid=(ncols // tn, m // tm),
        in_specs=[_row_spec(tm, d), _ss_spec(ss.shape[0], tm), _gain_spec(d),
                  _col_spec(w, d, tn, pipeline_mode=pl.Buffered(1))],
        out_specs=_tile_spec(tm, tn),
        out_shape=jax.ShapeDtypeStruct((m, ncols), F32),
        scratch_shapes=[pltpu.VMEM((d, tn), BF16)],
        compiler_params=_params("parallel", "arbitrary"),
        name="proj",
    )(x, ss, g.reshape(1, d), _arr(w))


def _tile_ss(x_new, ss_ref):
    ss_ref[0] = jnp.broadcast_to(jnp.sum(x_new * x_new, axis=1, keepdims=True), ss_ref.shape[1:])


def _ss_out(n, m, tm, tn):
    return (pl.BlockSpec((1, tm, LANES), lambda j, i: (j, i, 0)),
            jax.ShapeDtypeStruct((n // tn, m, LANES), F32))


def _ple_kernel(xrow_ref, ss_ref, g_ref, w_ref, x_ref, p_ref, wp_ref, o_ref, sso_ref, wb, wpb):
    _cast_weight(w_ref, wb)
    _cast_weight(wp_ref, wpb)
    gate = jax.nn.sigmoid(jnp.dot(_normed(xrow_ref, ss_ref, g_ref), wb[...], preferred_element_type=F32))
    emb = jnp.dot(p_ref[...].astype(BF16), wpb[...], preferred_element_type=F32)
    x_new = x_ref[...] + emb * gate
    o_ref[...] = x_new
    _tile_ss(x_new, sso_ref)


def _ple(x, ss, g, w_gate, p, w_proj, tm=1024, tn=512):
    m, d = x.shape
    pd = p[0].shape[-1]
    ss_spec, ss_shape = _ss_out(d, m, tm, tn)
    return pl.pallas_call(
        _ple_kernel,
        grid=(d // tn, m // tm),
        in_specs=[_row_spec(tm, d), _ss_spec(ss.shape[0], tm), _gain_spec(d), _col_spec(w_gate, d, tn),
                  _tile_spec(tm, tn), pl.BlockSpec((None, tm, pd), lambda j, i: (p[1], i, 0)),
                  _col_spec(w_proj, pd, tn)],
        out_specs=[_tile_spec(tm, tn), ss_spec],
        out_shape=[jax.ShapeDtypeStruct((m, d), F32), ss_shape],
        scratch_shapes=[pltpu.VMEM((d, tn), BF16), pltpu.VMEM((pd, tn), BF16)],
        compiler_params=_params("parallel", "arbitrary"),
        name="ple",
    )(x, ss, g.reshape(1, d), _arr(w_gate), x, p[0], _arr(w_proj))


def _mm_res_kernel(*refs, n_pairs, scale):
    res_ref, o_ref, sso_ref = refs[2 * n_pairs:2 * n_pairs + 3]
    scratch = refs[2 * n_pairs + 3:]
    acc = None
    for t in range(n_pairs):
        _cast_weight(refs[2 * t + 1], scratch[t])
        part = jnp.dot(refs[2 * t][...], scratch[t][...], preferred_element_type=F32)
        acc = part if acc is None else acc + part
    x_new = res_ref[...] + scale * acc
    o_ref[...] = x_new
    _tile_ss(x_new, sso_ref)


def _mm_res(pairs, res, scale, tm=512, tn=512):
    m, n = res.shape
    in_specs, args, scratch = [], [], []
    for a, w in pairs:
        k = a.shape[1]
        in_specs += [_row_spec(tm, k), _col_spec(w, k, tn)]
        args += [a, _arr(w)]
        scratch.append(pltpu.VMEM((k, tn), BF16))
    in_specs.append(_tile_spec(tm, tn))
    ss_spec, ss_shape = _ss_out(n, m, tm, tn)
    return pl.pallas_call(
        functools.partial(_mm_res_kernel, n_pairs=len(pairs), scale=scale),
        grid=(n // tn, m // tm),
        in_specs=in_specs,
        out_specs=[_tile_spec(tm, tn), ss_spec],
        out_shape=[jax.ShapeDtypeStruct((m, n), F32), ss_shape],
        scratch_shapes=scratch,
        compiler_params=_params("parallel", "arbitrary"),
        name="mm_res",
    )(*args, res)


def _row_ss_kernel(x_ref, ss_ref):
    x = x_ref[...]
    _tile_ss(x, ss_ref)


def _row_ss(x, tm=512):
    m, d = x.shape
    return pl.pallas_call(
        _row_ss_kernel,
        grid=(m // tm,),
        in_specs=[pl.BlockSpec((tm, d), lambda i: (i, 0))],
        out_specs=pl.BlockSpec((1, tm, LANES), lambda i: (0, i, 0)),
        out_shape=jax.ShapeDtypeStruct((1, m, LANES), F32),
        compiler_params=_params("parallel"),
        name="row_ss",
    )(x)


def _lane_lo(shape=(1, LANES)):
    return lax.broadcasted_iota(jnp.int32, shape, len(shape) - 1) < HEAD_DIM


def _rope_tables(n):
    half = HEAD_DIM // 2
    inv_freq = ROPE_THETA ** (-jnp.arange(half, dtype=F32) / half)
    ang = jnp.arange(n, dtype=F32)[:, None] * inv_freq[None, :]
    cos, sin = jnp.cos(ang), jnp.sin(ang)
    return (jnp.concatenate([cos, cos, cos, cos], axis=-1),
            jnp.concatenate([-sin, sin, -sin, sin], axis=-1))


def _rope_tile(x, cos, sin):
    lane = lax.broadcasted_iota(jnp.int32, (1, LANES), 1)
    first_half = (lane % HEAD_DIM) < (HEAD_DIM // 2)
    partner = jnp.where(first_half, pltpu.roll(x, LANES - HEAD_DIM // 2, 1),
                        pltpu.roll(x, HEAD_DIM // 2, 1))
    return x * cos + partner * sin


def _attn_a_kernel(q_ref, k_ref, v_ref, cos_ref, sin_ref, o_ref, qs, ks, m_s, l_s, acc_s):
    n = q_ref.shape[1]
    lo = _lane_lo()
    rb = 256

    def rope_body(i, carry):
        sl = pl.ds(pl.multiple_of(i * rb, rb), rb)
        c, s = cos_ref[sl, :], sin_ref[sl, :]
        qs[sl, :] = _rope_tile(q_ref[0, sl, :], c, s) * (HEAD_DIM ** -0.5)
        ks[sl, :] = _rope_tile(k_ref[0, sl, :], c, s)
        return carry

    lax.fori_loop(0, n // rb, rope_body, 0)
    m_s[...] = jnp.full(m_s.shape, MASKED, F32)
    l_s[...] = jnp.zeros(l_s.shape, F32)
    acc_s[...] = jnp.zeros(acc_s.shape, F32)

    group = 8

    for dil in A_DILATIONS:
        m = n // dil
        qb_rows = m if m <= 2 * A_QUERY_ROWS else A_QUERY_ROWS
        span = min(qb_rows + 2 * A_HALF, m)
        nqb = m // qb_rows

        def blocks(items, dil=dil, m=m, qb_rows=qb_rows, span=span):
            q0 = [qb * qb_rows for _, qb in items]
            k0 = [jnp.clip(t - A_HALF, 0, m - span) for t in q0]
            if dil == 1:
                rows_q = [pl.ds(pl.multiple_of(t, qb_rows), qb_rows) for t in q0]
                rows_k = [pl.ds(pl.multiple_of(t, A_HALF), span) for t in k0]
            else:
                rows_q = [pl.ds(r + t * dil, qb_rows, stride=dil) for (r, _), t in zip(items, q0)]
                rows_k = [pl.ds(r + t * dil, span, stride=dil) for (r, _), t in zip(items, k0)]
            unstack = lambda t: jnp.where(lo, t[:qb_rows], t[qb_rows:])
            q2 = [jnp.concatenate([jnp.where(lo, t, 0.0), jnp.where(lo, 0.0, t)], axis=0).astype(BF16)
                  for t in (qs[rq, :] for rq in rows_q)]
            kb = [ks[rk, :].astype(BF16) for rk in rows_k]
            vb = [v_ref[0, rk, :].astype(BF16) for rk in rows_k]
            m_old = [m_s[rq, :] for rq in rows_q]
            l_old = [l_s[rq, :] for rq in rows_q]
            acc_old = [acc_s[rq, :] for rq in rows_q]
            qrow = lax.broadcasted_iota(jnp.int32, (2 * qb_rows, 1), 0) % qb_rows
            kcol = lax.broadcasted_iota(jnp.int32, (1, span), 1)
            mask = _each(lambda a, c: jnp.abs(a + qrow - c - kcol) <= A_HALF, q0, k0)
            s = _each(lambda q, kb, mk: jnp.where(mk, _dot_nt(q, kb), MASKED), q2, kb, mask)
            mn = _each(lambda mo, s: jnp.maximum(jnp.concatenate([mo[:, :1], mo[:, LANES - 1:]], axis=0),
                                                 jnp.max(s, axis=1, keepdims=True)), m_old, s)
            p = _each(lambda s, mn: jnp.exp(s - mn), s, mn)
            psum = _each(lambda p: jnp.sum(p, axis=1, keepdims=True), p)
            pv = _each(_dot, p, vb)
            m_new = _each(unstack, mn)
            alpha = _each(lambda mo, mn: jnp.exp(mo - mn), m_old, m_new)
            for rq, mnew, al, lold, ps, ao, pvi in zip(rows_q, m_new, alpha, l_old, psum, acc_old, pv):
                m_s[rq, :] = mnew
                l_s[rq, :] = al * lold + unstack(ps)
                acc_s[rq, :] = al * ao + unstack(pvi)

        per = math.gcd(group, dil * nqb)

        def group_body(it, carry, blocks=blocks, dil=dil, per=per):
            ts = [it * per + u for u in range(per)]
            blocks([(t & (dil - 1), t >> (dil.bit_length() - 1)) for t in ts])
            return carry

        lax.fori_loop(0, dil * nqb // per, group_body, 0)

    def out_body(i, carry):
        sl = pl.ds(pl.multiple_of(i * rb, rb), rb)
        o_ref[0, sl, :] = (acc_s[sl, :] / l_s[sl, :]).astype(o_ref.dtype)
        return carry

    lax.fori_loop(0, n // rb, out_body, 0)


def _attn_a(za, cos, sin):
    b, n, _ = za.shape
    npair = A_W // LANES
    blk = lambda off: pl.BlockSpec((1, n, LANES), lambda i, j, off=off: (i, 0, off + j))
    tab = pl.BlockSpec((n, LANES), lambda i, j: (0, 0))
    return pl.pallas_call(
        _attn_a_kernel,
        grid=(b, npair),
        in_specs=[blk(0), blk(npair), blk(2 * npair), tab, tab],
        out_specs=pl.BlockSpec((1, n, LANES), lambda i, j: (i, 0, j)),
        out_shape=jax.ShapeDtypeStruct((b, n, A_W), BF16),
        scratch_shapes=[pltpu.VMEM((n, LANES), F32)] * 5,
        compiler_params=_params("parallel", "parallel"),
        name="attn_dilated",
    )(za, za, za, cos, sin)


def _attn_c_kernel(sink_ref, q_ref, k_ref, v_ref, cosq_ref, sinq_ref, cos_ref, sin_ref, o_ref,
                   klo, khi, vlo, vhi):
    n = k_ref.shape[1]
    qb_rows = q_ref.shape[1]
    span = qb_rows + 2 * C_HALF
    hk = pl.program_id(1)
    qb = pl.program_id(2)
    lo = _lane_lo()

    @pl.when(qb == 0)
    def _():
        rb = 256

        def body(i, carry):
            sl = pl.ds(pl.multiple_of(i * rb, rb), rb)
            kx = _rope_tile(k_ref[0, sl, :], cos_ref[sl, :], sin_ref[sl, :])
            kw = pltpu.roll(kx, HEAD_DIM, 1)
            vx = v_ref[0, sl, :]
            vw = pltpu.roll(vx, HEAD_DIM, 1)
            first = hk == 0
            klo[sl, :] = jnp.where(lo, jnp.where(first, kx, kw), 0.0).astype(BF16)
            khi[sl, :] = jnp.where(lo, 0.0, jnp.where(first, kw, kx)).astype(BF16)
            vlo[sl, :] = jnp.where(lo, jnp.where(first, vx, vw), 0.0).astype(BF16)
            vhi[sl, :] = jnp.where(lo, 0.0, jnp.where(first, vw, vx)).astype(BF16)
            return carry

        lax.fori_loop(0, n // rb, body, 0)

    q0 = qb * qb_rows
    k0 = pl.multiple_of(jnp.clip(q0 - C_HALF, 0, n - span), C_HALF)
    rows_k = pl.ds(k0, span)
    kl, kh = klo[rows_k, :], khi[rows_k, :]
    vl, vh = vlo[rows_k, :], vhi[rows_k, :]
    qpos = q0 + lax.broadcasted_iota(jnp.int32, (qb_rows, 1), 0)
    kpos = k0 + lax.broadcasted_iota(jnp.int32, (1, span), 1)
    mask = jnp.abs(qpos - kpos) <= C_HALF
    cq, sq = cosq_ref[...], sinq_ref[...]

    ntile = q_ref.shape[2] // LANES
    qt = [_rope_tile(q_ref[0, :, j * LANES:(j + 1) * LANES], cq, sq) * (HEAD_DIM ** -0.5) for j in range(ntile)]
    qt = [t for t in qt for _ in range(2)]
    kk = [kl, kh] * ntile
    sk = [sink_ref[hk * C_GROUP + h] for h in range(2 * ntile)]
    s = _each(lambda q, k: jnp.where(mask, _dot_nt(q, k), MASKED), qt, kk)
    mx = _each(lambda s, sk: jnp.maximum(jnp.max(s, axis=1, keepdims=True), sk), s, sk)
    p = _each(lambda s, mx: jnp.exp(s - mx), s, mx)
    den = _each(lambda p, sk, mx: jnp.sum(p, axis=1, keepdims=True) + jnp.exp(sk - mx), p, sk, mx)
    for j in range(ntile):
        pv = _dot(p[2 * j], vl) + _dot(p[2 * j + 1], vh)
        o_ref[0, :, j * LANES:(j + 1) * LANES] = (pv / jnp.where(lo, den[2 * j], den[2 * j + 1])).astype(o_ref.dtype)


def _attn_c(zc, sink, cos, sin, qb_rows=128):
    b, n, _ = zc.shape
    gw = C_GROUP * HEAD_DIM
    kcol = C_W // LANES
    full = lambda off: pl.BlockSpec((1, n, LANES), lambda i, h, t, off=off: (i, 0, off))
    tabq = pl.BlockSpec((qb_rows, LANES), lambda i, h, t: (t, 0))
    tab = pl.BlockSpec((n, LANES), lambda i, h, t: (0, 0))
    return pl.pallas_call(
        _attn_c_kernel,
        grid=(b, C_KV_HEADS, n // qb_rows),
        in_specs=[pl.BlockSpec(memory_space=pltpu.SMEM),
                  pl.BlockSpec((1, qb_rows, gw), lambda i, h, t: (i, t, h)),
                  full(kcol), full(kcol + 1), tabq, tabq, tab, tab],
        out_specs=pl.BlockSpec((1, qb_rows, gw), lambda i, h, t: (i, t, h)),
        out_shape=jax.ShapeDtypeStruct((b, n, C_W), BF16),
        scratch_shapes=[pltpu.VMEM((n, LANES), BF16)] * 4,
        compiler_params=_params("parallel", "parallel", "arbitrary"),
        name="attn_gqa_sink",
    )(sink.reshape(-1), zc, zc, zc, cos, sin, cos, sin)


def _na_bias_table(rpb, rows):
    kh = min(NA_KH, rows)
    qc = jnp.arange(GRID_W)[:, None]
    kc = jnp.arange(GRID_W)[None, :]
    qc0 = jnp.clip(qc - NA_KW // 2, 0, GRID_W - NA_KW)
    valid = (kc >= qc0) & (kc < qc0 + NA_KW)
    ci = jnp.clip(kc - qc, -(NA_KW - 1), NA_KW - 1) + NA_KW - 1
    pairs = rpb.astype(F32).reshape(D_HEADS // 2, 2, 2 * NA_KH - 1, 2 * NA_KW - 1)
    by_row = jnp.stack([pairs[:, :, off:off + kh] for off in range(NA_KH)], axis=1)
    onehot = (ci[None] == jnp.arange(2 * NA_KW - 1)[:, None, None]).astype(F32)
    bias = jnp.einsum('poekc,cqj->poeqkj', by_row, onehot, precision=lax.Precision.HIGHEST)
    bias = jnp.where(valid[:, None, :], bias, MASKED)
    return bias.reshape(D_HEADS // 2, NA_KH, 2 * GRID_W, kh * GRID_W)


def _attn_d_kernel(q_ref, k_ref, v_ref, bias_ref, o_ref):
    n = q_ref.shape[1]
    rows = n // GRID_W
    kh = min(NA_KH, rows)
    lo = _lane_lo()
    group = 4 if rows % 4 == 0 else 1

    def body(rg, carry):
        r = [rg * group + u for u in range(group)]
        r0 = [jnp.clip(t - NA_KH // 2, 0, rows - kh) for t in r]
        rows_q = [pl.ds(pl.multiple_of(t * GRID_W, GRID_W), GRID_W) for t in r]
        rows_k = [pl.ds(pl.multiple_of(t * GRID_W, GRID_W), kh * GRID_W) for t in r0]
        q = [q_ref[0, rq, :] * (HEAD_DIM ** -0.5) for rq in rows_q]
        q2 = _each(lambda q: jnp.concatenate([jnp.where(lo, q, 0.0), jnp.where(lo, 0.0, q)], axis=0), q)
        s = [_dot_nt(q2i, k_ref[0, rk, :]) + bias_ref[0, t0 - t + NA_KH - 1]
             for q2i, rk, t0, t in zip(q2, rows_k, r0, r)]
        p = _each(lambda s: jnp.exp(s - jnp.max(s, axis=1, keepdims=True)), s)
        pv = [_dot(pi, v_ref[0, rk, :]) / jnp.sum(pi, axis=1, keepdims=True) for pi, rk in zip(p, rows_k)]
        for rq, pvi in zip(rows_q, pv):
            o_ref[0, rq, :] = jnp.where(lo, pvi[:GRID_W], pvi[GRID_W:]).astype(o_ref.dtype)
        return carry

    lax.fori_loop(0, rows // group, body, 0)


def _attn_d(zc, bias):
    b, n, _ = zc.shape
    npair = D_W // LANES
    q_off = (C_W + 2 * C_KV_HEADS * HEAD_DIM) // LANES
    blk = lambda off: pl.BlockSpec((1, n, LANES), lambda i, j, off=off: (i, 0, off + j))
    return pl.pallas_call(
        _attn_d_kernel,
        grid=(b, npair),
        in_specs=[blk(q_off), blk(q_off + npair), blk(q_off + 2 * npair),
                  pl.BlockSpec((1,) + bias.shape[1:], lambda i, j: (j, 0, 0, 0))],
        out_specs=pl.BlockSpec((1, n, LANES), lambda i, j: (i, 0, j)),
        out_shape=jax.ShapeDtypeStruct((b, n, D_W), BF16),
        compiler_params=_params("parallel", "parallel"),
        name="attn_neighborhood",
    )(zc, zc, zc, bias)


def _rw_prep_kernel(z_ref, hp_ref, hn_ref, mup_ref, mun_ref, w0_ref, w2f_ref, w2b_ref, a0_ref,
                    a2f_ref, a2b_ref, g2_ref, r_o, k_o, v_o, cumf_o, cumb_o, af_o, ab_o, g_o):
    tm = z_ref.shape[1]
    first = pl.program_id(1) == 0
    last = pl.program_id(1) == pl.num_programs(1) - 1
    row = lax.broadcasted_iota(jnp.int32, (tm, 1), 0)
    crow = row % WKV_CHUNK

    def edges(cols):
        return (jnp.where(first, 0.0, hp_ref[0, SUBLANES - 1:SUBLANES, cols]),
                jnp.where(last, 0.0, hn_ref[0, 0:1, cols]))

    def mix(z, z_prev, z_next, cols):
        return z + mup_ref[:, cols] * (z_prev - z) + mun_ref[:, cols] * (z_next - z)

    def shifted(cols):
        z = z_ref[0, :, cols]
        edge_prev, edge_next = edges(cols)
        z_prev = jnp.where(row == 0, edge_prev, pltpu.roll(z, 1, 0))
        z_next = jnp.where(row == tm - 1, edge_next, pltpu.roll(z, tm - 1, 0))
        return mix(z, z_prev, z_next, cols)

    def store_shifted(o_ref, cs, cols):
        z = z_ref[0, :, cols]
        edge_prev, edge_next = edges(cols)
        o_ref[0, :, cs] = mix(z, pltpu.roll(z, 1, 0), pltpu.roll(z, tm - 1, 0), cols)
        o_ref[0, 0:1, cs] = mix(z[0:1], edge_prev, z[1:2], cols)
        o_ref[0, tm - 1:tm, cs] = mix(z[tm - 1:tm], z[tm - 2:tm - 1], edge_next, cols)

    def log_decay(x):
        return -RW_DECAY_SCALE * jax.nn.sigmoid(x)

    def chunk_cumsum(x, rev):
        s = 1
        while s < WKV_CHUNK:
            if rev:
                x = x + jnp.where(crow < WKV_CHUNK - s, pltpu.roll(x, tm - s, 0), 0.0)
            else:
                x = x + jnp.where(crow >= s, pltpu.roll(x, s, 0), 0.0)
            s *= 2
        return x

    base = 3 * B_W
    wl = jnp.tanh(shifted(slice(base, base + LANES))).astype(BF16)
    al = shifted(slice(base + LANES, base + 2 * LANES)).astype(BF16)
    gl = jax.nn.sigmoid(shifted(slice(base + 2 * LANES, base + 4 * LANES))).astype(BF16)

    for s0 in range(0, B_W, LANES):
        cs = slice(s0, s0 + LANES)
        store_shifted(r_o, cs, cs)
        store_shifted(k_o, cs, slice(B_W + s0, B_W + s0 + LANES))
        store_shifted(v_o, cs, slice(2 * B_W + s0, 2 * B_W + s0 + LANES))
        cumf_o[0, :, cs] = chunk_cumsum(log_decay(w0_ref[0:1, cs] + _dot(wl, w2f_ref[:, cs])), False)
        cumb_o[0, :, cs] = chunk_cumsum(log_decay(w0_ref[1:2, cs] + _dot(wl, w2b_ref[:, cs])), True)
        af_o[0, :, cs] = jax.nn.sigmoid(a0_ref[0:1, cs] + _dot(al, a2f_ref[:, cs]))
        ab_o[0, :, cs] = jax.nn.sigmoid(a0_ref[1:2, cs] + _dot(al, a2b_ref[:, cs]))
        g_o[0, :, cs] = _dot(gl, g2_ref[:, cs])


def _rw_prep(zb, mu_prev, mu_next, w0, w2, a0, a2, g2, tm=256):
    b, n, cols = zb.shape
    tm = min(tm, n)
    per = tm // SUBLANES
    pad_cols = lambda t: jnp.pad(t, (0, cols - t.shape[0])).reshape(1, cols)
    rows_f = lambda t: jnp.pad(t, ((0, LANES - RW_LORA), (0, 0))).astype(BF16)
    rows_b = lambda t: jnp.pad(t, ((RW_LORA, LANES - 2 * RW_LORA), (0, 0))).astype(BF16)
    g2p = jnp.pad(g2, ((0, 2 * LANES - RW_GATE_LORA), (0, 0))).astype(BF16)
    const = lambda shape: pl.BlockSpec(shape, lambda i, t: (0,) * len(shape))
    halo_prev = pl.BlockSpec((1, SUBLANES, cols), lambda i, t: (i, jnp.maximum(t * per - 1, 0), 0))
    halo_next = pl.BlockSpec((1, SUBLANES, cols), lambda i, t: (i, jnp.minimum((t + 1) * per, n // SUBLANES - 1), 0))
    tok = jax.ShapeDtypeStruct((b, n, B_W), F32)
    return pl.pallas_call(
        _rw_prep_kernel,
        grid=(b, n // tm),
        in_specs=[pl.BlockSpec((1, tm, cols), lambda i, t: (i, t, 0)), halo_prev, halo_next,
                  const((1, cols)), const((1, cols)), const((2, B_W)), const((LANES, B_W)),
                  const((LANES, B_W)), const((2, B_W)), const((LANES, B_W)), const((LANES, B_W)),
                  const((2 * LANES, B_W))],
        out_specs=[pl.BlockSpec((1, tm, B_W), lambda i, t: (i, t, 0))] * 8,
        out_shape=[tok] * 8,
        compiler_params=_params("parallel", "parallel"),
        name="rwkv_prep",
    )(zb, zb, zb, pad_cols(mu_prev), pad_cols(mu_next), w0, rows_f(w2[0]), rows_b(w2[1]),
      a0, rows_f(a2[0]), rows_b(a2[1]), g2p)


def _head_sum(x, lo):
    return jnp.where(lo, jnp.sum(jnp.where(lo, x, 0.0), axis=1, keepdims=True),
                     jnp.sum(jnp.where(lo, 0.0, x), axis=1, keepdims=True))


def _wkv_chunks(chains):
    c = chains[0][0].shape[0]
    ii = lax.broadcasted_iota(jnp.int32, (c, c), 0)
    jj = lax.broadcasted_iota(jnp.int32, (c, c), 1)
    row = lax.broadcasted_iota(jnp.int32, (c, 1), 0)
    eye_f = (ii == jj).astype(F32)
    masks = {False: (jj <= ii, jj < ii), True: (jj >= ii, jj > ii)}
    lo = _lane_lo()
    ki = lax.broadcasted_iota(jnp.int32, (LANES, LANES), 0)
    vj = lax.broadcasted_iota(jnp.int32, (LANES, LANES), 1)
    same_head = (ki < HEAD_DIM) == (vj < HEAD_DIM)
    diag = ki == vj
    r, k, v, cum, rate, st, k_k, k_a, rev = (list(col) for col in zip(*chains))
    twice = lambda xs: [x for x in xs for _ in range(2)]
    merge = lambda xs: [jnp.where(lo, xs[2 * i], xs[2 * i + 1]) for i in range(len(xs) // 2)]
    incl = twice([masks[x][0] for x in rev])
    strict = twice([masks[x][1] for x in rev])

    kk = _each(lambda k, k_k: k * k_k, k, k_k)
    kk = _each(lambda t: t / jnp.maximum(jnp.sqrt(_head_sum(t * t, lo)), 1e-12), kk)
    kd = _each(lambda k, rate, k_a: k * (1.0 + (rate - 1.0) * k_a), k, rate, k_a)
    b = _each(lambda kk, rate: kk * rate, kk, rate)

    tot = _each(lambda cum, x: cum[0:1] if x else cum[c - 1:c], cum, rev)
    excl = _each(lambda cum, x: jnp.where(row == c - 1, 0.0, pltpu.roll(cum, c - 1, 0)) if x
                 else jnp.where(row == 0, 0.0, pltpu.roll(cum, 1, 0)), cum, rev)
    ar = _each(lambda kk, r, cum, ex: jnp.concatenate([-kk * jnp.exp(ex), r * jnp.exp(cum)], axis=0),
               kk, r, cum, excl)
    ar_h = [jnp.where(m, x, 0.0).astype(BF16) for x in ar for m in (lo, ~lo)]
    ar = _each(lambda t: t.astype(BF16), ar)
    v = _each(lambda t: t.astype(BF16), v)
    g_inv = _each(lambda cum: jnp.exp(-cum), cum)
    bg = twice(_each(lambda b, g: (b * g).astype(BF16), b, g_inv))
    kg = twice(_each(lambda kd, g: (kd * g).astype(BF16), kd, g_inv))
    x1 = _each(_dot_nt, ar_h, bg)
    x2 = _each(_dot_nt, ar_h, kg)
    a_ab = _each(lambda x, m: jnp.where(m, x[:c], 0.0), x1, strict)
    a_rb = _each(lambda x, m: jnp.where(m, x[c:], 0.0).astype(BF16), x1, incl)
    a_k = _each(lambda x, ms, mi: jnp.concatenate([jnp.where(ms, x[:c], 0.0), jnp.where(mi, x[c:], 0.0)],
                                                  axis=0).astype(BF16), x2, strict, incl)

    inv = _each(lambda t: eye_f + t, a_ab)
    pw = _each(lambda t: _dot(t, t), a_ab)
    for _ in range(c.bit_length() - 3):
        pw = _each(lambda t: t.astype(BF16), pw)
        both = _each(lambda inv, pw: _dot(jnp.concatenate([inv.astype(BF16), pw], axis=0), pw), inv, pw)
        inv = _each(lambda inv, t: inv + t[:c], inv, both)
        pw = _each(lambda t: t[c:], both)
    inv = _each(lambda inv, pw: (inv + _dot(inv, pw)).astype(BF16), inv, pw)

    akv = merge(_each(_dot, a_k, twice(v)))
    ars = _each(_dot, ar, st)
    rhs = _each(lambda ars, akv: (ars[:c] + akv[:c]).astype(BF16), ars, akv)
    p = _each(lambda t: t.astype(BF16), merge(_each(_dot, inv, twice(rhs))))
    y = _each(lambda ars, arb, akv: ars[c:] + arb + akv[c:], ars, merge(_each(_dot, a_rb, twice(p))), akv)
    g_end = _each(lambda tot, cum: jnp.exp(tot - cum), tot, cum)
    upd = _each(lambda b, kd, g, p, v: _dot_tn(jnp.concatenate([b * g, kd * g], axis=0),
                                               jnp.concatenate([p, v], axis=0)), b, kd, g_end, p, v)
    g_col = _each(lambda tot: jnp.sum(jnp.where(diag, jnp.exp(tot), 0.0), axis=1, keepdims=True), tot)
    new_st = _each(lambda g, st, upd: jnp.where(same_head, g * st + upd, 0.0), g_col, st, upd)
    return list(zip(y, new_st))


def _rw_scan_kernel(rf, kf, vf, cumf, af, rb, kb, vb, cumb, ab, kk_ref, ka_ref, yf, yb, sf, sb):
    tb = rf.shape[1]
    npair = rf.shape[2] // LANES
    c = WKV_CHUNK
    nch = tb // c
    pair = lambda j: slice(j * LANES, (j + 1) * LANES)

    @pl.when(pl.program_id(2) == 0)
    def _():
        sf[...] = jnp.zeros(sf.shape, F32)
        sb[...] = jnp.zeros(sb.shape, F32)

    def body(ci, carry):
        rows_f = pl.ds(pl.multiple_of(ci * c, c), c)
        rows_b = pl.ds(pl.multiple_of((nch - 1 - ci) * c, c), c)
        chains = []
        for j in range(npair):
            par = (kk_ref[:, pair(j)], ka_ref[:, pair(j)])
            chains.append(tuple(ref[0, rows_f, pair(j)] for ref in (rf, kf, vf, cumf, af)) + (sf[j],) + par + (False,))
            chains.append(tuple(ref[0, rows_b, pair(j)] for ref in (rb, kb, vb, cumb, ab)) + (sb[j],) + par + (True,))
        outs = _wkv_chunks(chains)
        for j in range(npair):
            yf[0, rows_f, pair(j)], sf[j] = outs[2 * j]
            yb[0, rows_b, pair(j)], sb[j] = outs[2 * j + 1]
        return carry

    lax.fori_loop(0, nch, body, 0)


def _rw_scan(r, k, v, cumf, cumb, af, ab, k_k, k_a, hb=10, tb=256):
    b, n, width = r.shape
    tb = min(tb, n)
    nt = n // tb
    hw = hb * HEAD_DIM
    fwd = pl.BlockSpec((1, tb, hw), lambda i, g, t: (i, t, g))
    bwd = pl.BlockSpec((1, tb, hw), lambda i, g, t: (i, nt - 1 - t, g))
    par = pl.BlockSpec((1, hw), lambda i, g, t: (0, g))
    out = jax.ShapeDtypeStruct((b, n, width), F32)
    return pl.pallas_call(
        _rw_scan_kernel,
        grid=(b, width // hw, nt),
        in_specs=[fwd] * 5 + [bwd] * 5 + [par, par],
        out_specs=[fwd, bwd],
        out_shape=[out, out],
        scratch_shapes=[pltpu.VMEM((hw // LANES, LANES, LANES), F32)] * 2,
        compiler_params=_params("parallel", "parallel", "arbitrary"),
        name="rwkv_scan",
    )(r, k, v, cumf, af, r, k, v, cumb, ab, k_k.reshape(1, width), k_a.reshape(1, width))


def _rw_post_kernel(yf, yb, r, k, v, af, ab, g, ka_ref, rk_ref, lnw_ref, lnb_ref, o_ref):
    lo = _lane_lo()
    head_sum = lambda x: _head_sum(x, lo)
    for j in range(o_ref.shape[2] // LANES):
        sl = slice(j * LANES, (j + 1) * LANES)
        y = yf[0, :, sl] + yb[0, :, sl]
        dev = y - head_sum(y) * (1.0 / HEAD_DIM)
        var = head_sum(dev * dev) * (1.0 / HEAD_DIM)
        yn = dev * lax.rsqrt(var + RW_GN_EPS) * lnw_ref[:, sl] + lnb_ref[:, sl]
        kt, k_a = k[0, :, sl], ka_ref[:, sl]
        kd = kt * (1.0 + (af[0, :, sl] - 1.0) * k_a) + kt * (1.0 + (ab[0, :, sl] - 1.0) * k_a)
        bonus = head_sum(r[0, :, sl] * kd * rk_ref[:, sl]) * v[0, :, sl]
        o_ref[0, :, sl] = ((yn + bonus) * g[0, :, sl]).astype(o_ref.dtype)


def _rw_post(yf, yb, r, k, v, af, ab, g, k_a, r_k, ln_w, ln_b, tm=256):
    b, n, width = r.shape
    tm = min(tm, n)
    tok = pl.BlockSpec((1, tm, width), lambda i, t: (i, t, 0))
    par = pl.BlockSpec((1, width), lambda i, t: (0, 0))
    return pl.pallas_call(
        _rw_post_kernel,
        grid=(b, n // tm),
        in_specs=[tok] * 8 + [par] * 4,
        out_specs=tok,
        out_shape=jax.ShapeDtypeStruct((b, n, width), BF16),
        compiler_params=_params("parallel", "parallel"),
        name="rwkv_post",
    )(yf, yb, r, k, v, af, ab, g, k_a.reshape(1, width), r_k.reshape(1, width),
      ln_w.reshape(1, width), ln_b.reshape(1, width))


def _rwkv7_bidir(zb, mu_prev, mu_next, w0, w2, a0, a2, g2, k_k, k_a, r_k, ln_w, ln_b):
    r, k, v, cumf, cumb, af, ab, g = _rw_prep(zb, mu_prev, mu_next, w0, w2, a0, a2, g2)
    yf, yb = _rw_scan(r, k, v, cumf, cumb, af, ab, k_k, k_a)
    return _rw_post(yf, yb, r, k, v, af, ab, g, k_a, r_k, ln_w, ln_b)


def _ffn_half_step(x, ss, g, w1, w3, w2, layer):
    h = _ffn_up(x, ss, g, (w1, layer), (w3, layer))
    return _mm_res([(h, (w2, layer))], x, 0.5)


def _mix_ab(x, ss, g, b, n, cos, sin, w_in, layer, w_out, mu_prev, mu_next, w0, w2, a0, a2, g2, k_k, k_a,
            r_k, ln_w, ln_b):
    qkv = 3 * A_W
    w_b = jnp.pad(w_in[layer, :, qkv:], ((0, 0), (0, RW_PAD_COLS - (w_in.shape[2] - qkv))))
    za = _proj(x, ss, g, w_in[layer, :, :qkv], qkv, tn=qkv // 3).reshape(b, n, qkv)
    zb = _proj(x, ss, g, w_b, RW_PAD_COLS, tn=RW_PAD_COLS // 2).reshape(b, n, RW_PAD_COLS)
    oa = _attn_a(za, cos, sin).reshape(b * n, A_W)
    ob = _rwkv7_bidir(zb, mu_prev, mu_next, w0, w2, a0, a2, g2, k_k, k_a, r_k, ln_w, ln_b)
    return _mm_res([(oa, w_out[:A_W]), (ob.reshape(b * n, B_W), w_out[A_W:])], x, 1.0, tm=1024)


def _mix_cd(x, ss, g, b, n, cos, sin, w_in, layer, w_out, sink, rpb):
    cols = w_in.shape[2]
    zc = _proj(x, ss, g, (w_in, layer), cols, tn=cols // 2).reshape(b, n, cols)
    oc = _attn_c(zc, sink, cos, sin).reshape(b * n, C_W)
    od = _attn_d(zc, _na_bias_table(rpb, n // GRID_W)).reshape(b * n, D_W)
    return _mm_res([(oc, w_out[:C_W]), (od, w_out[C_W:])], x, 1.0, tm=1024)


def kernel(x, p, ffn1_norm, ffn1_w1, ffn1_w3, ffn1_w2, mix_norm, ffn2_norm, ffn2_w1, ffn2_w3, ffn2_w2, ple_norm, ple_w_gate, ple_w_proj, ab_w_in, ab_w_out, rw_mu_prev, rw_mu_next, rw_w0, rw_w2, rw_a0, rw_a2, rw_g2, rw_k_k, rw_k_a, rw_r_k, rw_ln_w, rw_ln_b, cd_w_in, cd_w_out, c_sink, d_rpb, final_norm):
    b, n, d = x.shape
    depth = p.shape[0]
    cos, sin = _rope_tables(n)
    x = x.reshape(b * n, d)
    p = p.reshape(depth, b * n, -1)
    ss = _row_ss(x)
    for i in range(depth):
        j = i // 2
        x, ss = _ffn_half_step(x, ss, ffn1_norm[i], ffn1_w1, ffn1_w3, ffn1_w2, i)
        if i % 2 == 0:
            x, ss = _mix_ab(x, ss, mix_norm[i], b, n, cos, sin, ab_w_in, j, ab_w_out[j], rw_mu_prev[j],
                            rw_mu_next[j], rw_w0[j], rw_w2[j], rw_a0[j], rw_a2[j], rw_g2[j], rw_k_k[j],
                            rw_k_a[j], rw_r_k[j], rw_ln_w[j], rw_ln_b[j])
        else:
            x, ss = _mix_cd(x, ss, mix_norm[i], b, n, cos, sin, cd_w_in, j, cd_w_out[j], c_sink[j], d_rpb[j])
        x, ss = _ffn_half_step(x, ss, ffn2_norm[i], ffn2_w1, ffn2_w3, ffn2_w2, i)
        x, ss = _ple(x, ss, ple_norm[i], (ple_w_gate, i), (p, i), (ple_w_proj, i))
    return _rmsnorm(x, final_norm, F32).reshape(b, n, d)
```

```python
import functools
import math

import jax
import jax.numpy as jnp
from jax import lax
from jax.experimental import pallas as pl
from jax.experimental.pallas import tpu as pltpu

F32 = jnp.float32
BF16 = jnp.bfloat16

HEAD_DIM = 64
LANES = 128
SUBLANES = 8
NORM_EPS = 1e-6
ROPE_THETA = 10000.0
GRID_W = 64
A_HEADS = 12
A_W = A_HEADS * HEAD_DIM
A_HALF = 64
A_DILATIONS = (1, 4, 16)
A_QUERY_ROWS = 64
B_HEADS = 20
B_W = B_HEADS * HEAD_DIM
RW_LORA = 64
RW_GATE_LORA = 192
RW_GN_EPS = 64e-5
RW_DECAY_SCALE = math.exp(-0.5)
RW_PAD_COLS = 3 * B_W + 4 * LANES
C_HEADS = 16
C_KV_HEADS = 2
C_GROUP = C_HEADS // C_KV_HEADS
C_W = C_HEADS * HEAD_DIM
C_HALF = 128
D_HEADS = 16
D_W = D_HEADS * HEAD_DIM
NA_KH = 8
NA_KW = 16
MASKED = -1e30
WKV_CHUNK = 64
VMEM_LIMIT = 56 * 1024 * 1024


def _params(*sem):
    return pltpu.CompilerParams(dimension_semantics=sem, vmem_limit_bytes=VMEM_LIMIT)


def _dot(a, b):
    return jnp.dot(a.astype(BF16), b.astype(BF16), preferred_element_type=F32)


def _dot_nt(a, b):
    return lax.dot_general(a.astype(BF16), b.astype(BF16), (((1,), (1,)), ((), ())),
                           preferred_element_type=F32)


def _dot_tn(a, b):
    return lax.dot_general(a.astype(BF16), b.astype(BF16), (((0,), (0,)), ((), ())),
                           preferred_element_type=F32)


def _each(f, *cols):
    return [f(*xs) for xs in zip(*cols)]


def _rmsnorm_kernel(x_ref, g_ref, o_ref):
    x = x_ref[...]
    ms = jnp.mean(x * x, axis=-1, keepdims=True)
    o_ref[...] = (x * lax.rsqrt(ms + NORM_EPS) * g_ref[...]).astype(o_ref.dtype)


def _rmsnorm(x, g, dtype, tm=512):
    m, d = x.shape
    return pl.pallas_call(
        _rmsnorm_kernel,
        grid=(m // tm,),
        in_specs=[pl.BlockSpec((tm, d), lambda i: (i, 0)),
                  pl.BlockSpec((1, d), lambda i: (0, 0))],
        out_specs=pl.BlockSpec((tm, d), lambda i: (i, 0)),
        out_shape=jax.ShapeDtypeStruct((m, d), dtype),
        compiler_params=_params("parallel"),
        name="rmsnorm",
    )(x, g.reshape(1, d))


def _cast_weight(w_ref, wb_ref):
    rows = 256

    @pl.when(pl.program_id(1) == 0)
    def _():
        def body(i, carry):
            sl = pl.ds(pl.multiple_of(i * rows, rows), rows)
            wb_ref[sl, :] = w_ref[sl, :].astype(BF16)
            return carry

        lax.fori_loop(0, w_ref.shape[0] // rows, body, 0)


def _row_spec(tm, k):
    return pl.BlockSpec((tm, k), lambda j, i: (i, 0))


def _col_spec(w, k, tn, **kw):
    if isinstance(w, tuple):
        layer = w[1]
        return pl.BlockSpec((None, k, tn), lambda j, i: (layer, 0, j), **kw)
    return pl.BlockSpec((k, tn), lambda j, i: (0, j), **kw)


def _arr(w):
    return w[0] if isinstance(w, tuple) else w


def _tile_spec(tm, tn):
    return pl.BlockSpec((tm, tn), lambda j, i: (i, j))


def _normed(x_ref, ss_ref, g_ref):
    ss = ss_ref[0, :, :1]
    for t in range(1, ss_ref.shape[0]):
        ss = ss + ss_ref[t, :, :1]
    rs = lax.rsqrt(ss * (1.0 / x_ref.shape[1]) + NORM_EPS)
    return (x_ref[...] * rs * g_ref[...]).astype(BF16)


def _ss_spec(nt, tm):
    return pl.BlockSpec((nt, tm, LANES), lambda j, i: (0, i, 0))


def _gain_spec(d):
    return pl.BlockSpec((1, d), lambda j, i: (0, 0))


def _ffn_up_kernel(*refs, normed):
    w1_ref, w3_ref, o_ref, w1b, w3b = refs[-5:]
    _cast_weight(w1_ref, w1b)
    _cast_weight(w3_ref, w3b)
    xn = refs[0][...] if normed else _normed(*refs[:3])
    a = jnp.dot(xn, w1b[...], preferred_element_type=F32)
    b = jnp.dot(xn, w3b[...], preferred_element_type=F32)
    o_ref[...] = (a * jax.nn.sigmoid(a) * b).astype(o_ref.dtype)


def _ffn_up(x, ss, g, w1, w3, tm=1024, tn=512):
    m, d = x.shape
    f = _arr(w1).shape[-1]
    normed = ss is None
    lead_specs = [_row_spec(tm, d)] if normed else [_row_spec(tm, d), _ss_spec(ss.shape[0], tm), _gain_spec(d)]
    lead_args = [x] if normed else [x, ss, g.reshape(1, d)]
    return pl.pallas_call(
        functools.partial(_ffn_up_kernel, normed=normed),
        grid=(f // tn, m // tm),
        in_specs=lead_specs + [_col_spec(w1, d, tn), _col_spec(w3, d, tn)],
        out_specs=_tile_spec(tm, tn),
        out_shape=jax.ShapeDtypeStruct((m, f), BF16),
        scratch_shapes=[pltpu.VMEM((d, tn), BF16)] * 2,
        compiler_params=_params("parallel", "arbitrary"),
        name="ffn_up",
    )(*lead_args, _arr(w1), _arr(w3))


def _proj_kernel(x_ref, ss_ref, g_ref, w_ref, o_ref, wb):
    _cast_weight(w_ref, wb)
    o_ref[...] = jnp.dot(_normed(x_ref, ss_ref, g_ref), wb[...], preferred_element_type=F32)


def _proj(x, ss, g, w, ncols, tn, tm=512):
    m, d = x.shape
    return pl.pallas_call(
        _proj_kernel,
        grid=(ncols // tn, m // tm),
        in_specs=[_row_spec(tm, d), _ss_spec(ss.shape[0], tm), _gain_spec(d),
                  _col_spec(w, d, tn, pipeline_mode=pl.Buffered(1))],
        out_specs=_tile_spec(tm, tn),
        out_shape=jax.ShapeDtypeStruct((m, ncols), F32),
        scratch_shapes=[pltpu.VMEM((d, tn), BF16)],
        compiler_params=_params("parallel", "arbitrary"),
        name="proj",
    )(x, ss, g.reshape(1, d), _arr(w))


def _tile_ss(x_new, ss_ref):
    ss_ref[0] = jnp.broadcast_to(jnp.sum(x_new * x_new, axis=1, keepdims=True), ss_ref.shape[1:])


def _ss_out(n, m, tm, tn):
    return (pl.BlockSpec((1, tm, LANES), lambda j, i: (j, i, 0)),
            jax.ShapeDtypeStruct((n // tn, m, LANES), F32))


def _ple_kernel(xrow_ref, ss_ref, g_ref, w_ref, x_ref, p_ref, wp_ref, o_ref, sso_ref, wb, wpb):
    _cast_weight(w_ref, wb)
    _cast_weight(wp_ref, wpb)
    gate = jax.nn.sigmoid(jnp.dot(_normed(xrow_ref, ss_ref, g_ref), wb[...], preferred_element_type=F32))
    emb = jnp.dot(p_ref[...].astype(BF16), wpb[...], preferred_element_type=F32)
    x_new = x_ref[...] + emb * gate
    o_ref[...] = x_new
    _tile_ss(x_new, sso_ref)


def _ple(x, ss, g, w_gate, p, w_proj, tm=1024, tn=512):
    m, d = x.shape
    pd = p[0].shape[-1]
    ss_spec, ss_shape = _ss_out(d, m, tm, tn)
    return pl.pallas_call(
        _ple_kernel,
        grid=(d // tn, m // tm),
        in_specs=[_row_spec(tm, d), _ss_spec(ss.shape[0], tm), _gain_spec(d), _col_spec(w_gate, d, tn),
                  _tile_spec(tm, tn), pl.BlockSpec((None, tm, pd), lambda j, i: (p[1], i, 0)),
                  _col_spec(w_proj, pd, tn)],
        out_specs=[_tile_spec(tm, tn), ss_spec],
        out_shape=[jax.ShapeDtypeStruct((m, d), F32), ss_shape],
        scratch_shapes=[pltpu.VMEM((d, tn), BF16), pltpu.VMEM((pd, tn), BF16)],
        compiler_params=_params("parallel", "arbitrary"),
        name="ple",
    )(x, ss, g.reshape(1, d), _arr(w_gate), x, p[0], _arr(w_proj))


def _mm_res_kernel(*refs, n_pairs, scale):
    res_ref, o_ref, sso_ref = refs[2 * n_pairs:2 * n_pairs + 3]
    scratch = refs[2 * n_pairs + 3:]
    acc = None
    for t in range(n_pairs):
        _cast_weight(refs[2 * t + 1], scratch[t])
        part = jnp.dot(refs[2 * t][...], scratch[t][...], preferred_element_type=F32)
        acc = part if acc is None else acc + part
    x_new = res_ref[...] + scale * acc
    o_ref[...] = x_new
    _tile_ss(x_new, sso_ref)


def _mm_res(pairs, res, scale, tm=512, tn=512):
    m, n = res.shape
    in_specs, args, scratch = [], [], []
    for a, w in pairs:
        k = a.shape[1]
        in_specs += [_row_spec(tm, k), _col_spec(w, k, tn)]
        args += [a, _arr(w)]
        scratch.append(pltpu.VMEM((k, tn), BF16))
    in_specs.append(_tile_spec(tm, tn))
    ss_spec, ss_shape = _ss_out(n, m, tm, tn)
    return pl.pallas_call(
        functools.partial(_mm_res_kernel, n_pairs=len(pairs), scale=scale),
        grid=(n // tn, m // tm),
        in_specs=in_specs,
        out_specs=[_tile_spec(tm, tn), ss_spec],
        out_shape=[jax.ShapeDtypeStruct((m, n), F32), ss_shape],
        scratch_shapes=scratch,
        compiler_params=_params("parallel", "arbitrary"),
        name="mm_res",
    )(*args, res)


def _lane_lo(shape=(1, LANES)):
    return lax.broadcasted_iota(jnp.int32, shape, len(shape) - 1) < HEAD_DIM


def _rope_tables(n):
    half = HEAD_DIM // 2
    inv_freq = ROPE_THETA ** (-jnp.arange(half, dtype=F32) / half)
    ang = jnp.arange(n, dtype=F32)[:, None] * inv_freq[None, :]
    cos, sin = jnp.cos(ang), jnp.sin(ang)
    return (jnp.concatenate([cos, cos, cos, cos], axis=-1),
            jnp.concatenate([-sin, sin, -sin, sin], axis=-1))


def _rope_tile(x, cos, sin):
    lane = lax.broadcasted_iota(jnp.int32, (1, LANES), 1)
    first_half = (lane % HEAD_DIM) < (HEAD_DIM // 2)
    partner = jnp.where(first_half, pltpu.roll(x, LANES - HEAD_DIM // 2, 1),
                        pltpu.roll(x, HEAD_DIM // 2, 1))
    return x * cos + partner * sin


def _attn_a_kernel(q_ref, k_ref, v_ref, cos_ref, sin_ref, o_ref, qs, ks, m_s, l_s, acc_s):
    n = q_ref.shape[1]
    lo = _lane_lo()
    rb = 256

    def rope_body(i, carry):
        sl = pl.ds(pl.multiple_of(i * rb, rb), rb)
        c, s = cos_ref[sl, :], sin_ref[sl, :]
        qs[sl, :] = _rope_tile(q_ref[0, sl, :], c, s) * (HEAD_DIM ** -0.5)
        ks[sl, :] = _rope_tile(k_ref[0, sl, :], c, s)
        return carry

    lax.fori_loop(0, n // rb, rope_body, 0)
    m_s[...] = jnp.full(m_s.shape, MASKED, F32)
    l_s[...] = jnp.zeros(l_s.shape, F32)
    acc_s[...] = jnp.zeros(acc_s.shape, F32)

    group = 8

    for dil in A_DILATIONS:
        m = n // dil
        qb_rows = m if m <= 2 * A_QUERY_ROWS else A_QUERY_ROWS
        span = min(qb_rows + 2 * A_HALF, m)
        nqb = m // qb_rows

        def blocks(items, dil=dil, m=m, qb_rows=qb_rows, span=span):
            q0 = [qb * qb_rows for _, qb in items]
            k0 = [jnp.clip(t - A_HALF, 0, m - span) for t in q0]
            if dil == 1:
                rows_q = [pl.ds(pl.multiple_of(t, qb_rows), qb_rows) for t in q0]
                rows_k = [pl.ds(pl.multiple_of(t, A_HALF), span) for t in k0]
            else:
                rows_q = [pl.ds(r + t * dil, qb_rows, stride=dil) for (r, _), t in zip(items, q0)]
                rows_k = [pl.ds(r + t * dil, span, stride=dil) for (r, _), t in zip(items, k0)]
            unstack = lambda t: jnp.where(lo, t[:qb_rows], t[qb_rows:])
            q2 = [jnp.concatenate([jnp.where(lo, t, 0.0), jnp.where(lo, 0.0, t)], axis=0).astype(BF16)
                  for t in (qs[rq, :] for rq in rows_q)]
            kb = [ks[rk, :].astype(BF16) for rk in rows_k]
            vb = [v_ref[0, rk, :].astype(BF16) for rk in rows_k]
            m_old = [m_s[rq, :] for rq in rows_q]
            l_old = [l_s[rq, :] for rq in rows_q]
            acc_old = [acc_s[rq, :] for rq in rows_q]
            qrow = lax.broadcasted_iota(jnp.int32, (2 * qb_rows, 1), 0) % qb_rows
            kcol = lax.broadcasted_iota(jnp.int32, (1, span), 1)
            mask = _each(lambda a, c: jnp.abs(a + qrow - c - kcol) <= A_HALF, q0, k0)
            s = _each(lambda q, kb, mk: jnp.where(mk, _dot_nt(q, kb), MASKED), q2, kb, mask)
            mn = _each(lambda mo, s: jnp.maximum(jnp.concatenate([mo[:, :1], mo[:, LANES - 1:]], axis=0),
                                                 jnp.max(s, axis=1, keepdims=True)), m_old, s)
            p = _each(lambda s, mn: jnp.exp(s - mn), s, mn)
            psum = _each(lambda p: jnp.sum(p, axis=1, keepdims=True), p)
            pv = _each(_dot, p, vb)
            m_new = _each(unstack, mn)
            alpha = _each(lambda mo, mn: jnp.exp(mo - mn), m_old, m_new)
            for rq, mnew, al, lold, ps, ao, pvi in zip(rows_q, m_new, alpha, l_old, psum, acc_old, pv):
                m_s[rq, :] = mnew
                l_s[rq, :] = al * lold + unstack(ps)
                acc_s[rq, :] = al * ao + unstack(pvi)

        per = math.gcd(group, dil * nqb)

        def group_body(it, carry, blocks=blocks, dil=dil, per=per):
            ts = [it * per + u for u in range(per)]
            blocks([(t & (dil - 1), t >> (dil.bit_length() - 1)) for t in ts])
            return carry

        lax.fori_loop(0, dil * nqb // per, group_body, 0)

    def out_body(i, carry):
        sl = pl.ds(pl.multiple_of(i * rb, rb), rb)
        o_ref[0, sl, :] = (acc_s[sl, :] / l_s[sl, :]).astype(o_ref.dtype)
        return carry

    lax.fori_loop(0, n // rb, out_body, 0)


def _attn_a(za, cos, sin):
    b, n, _ = za.shape
    npair = A_W // LANES
    blk = lambda off: pl.BlockSpec((1, n, LANES), lambda i, j, off=off: (i, 0, off + j))
    tab = pl.BlockSpec((n, LANES), lambda i, j: (0, 0))
    return pl.pallas_call(
        _attn_a_kernel,
        grid=(b, npair),
        in_specs=[blk(0), blk(npair), blk(2 * npair), tab, tab],
        out_specs=pl.BlockSpec((1, n, LANES), lambda i, j: (i, 0, j)),
        out_shape=jax.ShapeDtypeStruct((b, n, A_W), BF16),
        scratch_shapes=[pltpu.VMEM((n, LANES), F32)] * 5,
        compiler_params=_params("parallel", "parallel"),
        name="attn_dilated",
    )(za, za, za, cos, sin)


def _attn_c_kernel(sink_ref, q_ref, k_ref, v_ref, cosq_ref, sinq_ref, cos_ref, sin_ref, o_ref,
                   klo, khi, vlo, vhi):
    n = k_ref.shape[1]
    qb_rows = q_ref.shape[1]
    span = qb_rows + 2 * C_HALF
    hk = pl.program_id(1)
    qb = pl.program_id(2)
    lo = _lane_lo()

    @pl.when(qb == 0)
    def _():
        rb = 256

        def body(i, carry):
            sl = pl.ds(pl.multiple_of(i * rb, rb), rb)
            kx = _rope_tile(k_ref[0, sl, :], cos_ref[sl, :], sin_ref[sl, :])
            kw = pltpu.roll(kx, HEAD_DIM, 1)
            vx = v_ref[0, sl, :]
            vw = pltpu.roll(vx, HEAD_DIM, 1)
            first = hk == 0
            klo[sl, :] = jnp.where(lo, jnp.where(first, kx, kw), 0.0).astype(BF16)
            khi[sl, :] = jnp.where(lo, 0.0, jnp.where(first, kw, kx)).astype(BF16)
            vlo[sl, :] = jnp.where(lo, jnp.where(first, vx, vw), 0.0).astype(BF16)
            vhi[sl, :] = jnp.where(lo, 0.0, jnp.where(first, vw, vx)).astype(BF16)
            return carry

        lax.fori_loop(0, n // rb, body, 0)

    q0 = qb * qb_rows
    k0 = pl.multiple_of(jnp.clip(q0 - C_HALF, 0, n - span), C_HALF)
    rows_k = pl.ds(k0, span)
    kl, kh = klo[rows_k, :], khi[rows_k, :]
    vl, vh = vlo[rows_k, :], vhi[rows_k, :]
    qpos = q0 + lax.broadcasted_iota(jnp.int32, (qb_rows, 1), 0)
    kpos = k0 + lax.broadcasted_iota(jnp.int32, (1, span), 1)
    mask = jnp.abs(qpos - kpos) <= C_HALF
    cq, sq = cosq_ref[...], sinq_ref[...]

    ntile = q_ref.shape[2] // LANES
    qt = [_rope_tile(q_ref[0, :, j * LANES:(j + 1) * LANES], cq, sq) * (HEAD_DIM ** -0.5) for j in range(ntile)]
    qt = [t for t in qt for _ in range(2)]
    kk = [kl, kh] * ntile
    sk = [sink_ref[hk * C_GROUP + h] for h in range(2 * ntile)]
    s = _each(lambda q, k: jnp.where(mask, _dot_nt(q, k), MASKED), qt, kk)
    mx = _each(lambda s, sk: jnp.maximum(jnp.max(s, axis=1, keepdims=True), sk), s, sk)
    p = _each(lambda s, mx: jnp.exp(s - mx), s, mx)
    den = _each(lambda p, sk, mx: jnp.sum(p, axis=1, keepdims=True) + jnp.exp(sk - mx), p, sk, mx)
    for j in range(ntile):
        pv = _dot(p[2 * j], vl) + _dot(p[2 * j + 1], vh)
        o_ref[0, :, j * LANES:(j + 1) * LANES] = (pv / jnp.where(lo, den[2 * j], den[2 * j + 1])).astype(o_ref.dtype)


def _attn_c(zc, sink, cos, sin, qb_rows=128):
    b, n, _ = zc.shape
    gw = C_GROUP * HEAD_DIM
    kcol = C_W // LANES
    full = lambda off: pl.BlockSpec((1, n, LANES), lambda i, h, t, off=off: (i, 0, off))
    tabq = pl.BlockSpec((qb_rows, LANES), lambda i, h, t: (t, 0))
    tab = pl.BlockSpec((n, LANES), lambda i, h, t: (0, 0))
    return pl.pallas_call(
        _attn_c_kernel,
        grid=(b, C_KV_HEADS, n // qb_rows),
        in_specs=[pl.BlockSpec(memory_space=pltpu.SMEM),
                  pl.BlockSpec((1, qb_rows, gw), lambda i, h, t: (i, t, h)),
                  full(kcol), full(kcol + 1), tabq, tabq, tab, tab],
        out_specs=pl.BlockSpec((1, qb_rows, gw), lambda i, h, t: (i, t, h)),
        out_shape=jax.ShapeDtypeStruct((b, n, C_W), BF16),
        scratch_shapes=[pltpu.VMEM((n, LANES), BF16)] * 4,
        compiler_params=_params("parallel", "parallel", "arbitrary"),
        name="attn_gqa_sink",
    )(sink.reshape(-1), zc, zc, zc, cos, sin, cos, sin)


def _na_bias_table(rpb, rows):
    kh = min(NA_KH, rows)
    qc = jnp.arange(GRID_W)[:, None]
    kc = jnp.arange(GRID_W)[None, :]
    qc0 = jnp.clip(qc - NA_KW // 2, 0, GRID_W - NA_KW)
    valid = (kc >= qc0) & (kc < qc0 + NA_KW)
    ci = jnp.clip(kc - qc, -(NA_KW - 1), NA_KW - 1) + NA_KW - 1
    nci = 2 * NA_KW - 1
    pairs = rpb.astype(F32).reshape(D_HEADS // 2, 2, 2 * NA_KH - 1, nci)
    pairs = jnp.concatenate([pairs, jnp.full(pairs.shape[:-1] + (1,), MASKED, F32)], axis=-1)
    by_row = jnp.stack([pairs[:, :, off:off + kh] for off in range(NA_KH)], axis=1)
    cls = jnp.where(valid, ci, nci)
    onehot = (cls[None] == jnp.arange(nci + 1)[:, None, None]).astype(F32)
    bias = jnp.einsum('poekc,cqj->poeqkj', by_row, onehot, precision=lax.Precision.HIGHEST)
    return bias.reshape(D_HEADS // 2, NA_KH, 2 * GRID_W, kh * GRID_W)


def _attn_d_kernel(q_ref, k_ref, v_ref, bias_ref, o_ref):
    n = q_ref.shape[1]
    rows = n // GRID_W
    kh = min(NA_KH, rows)
    lo = _lane_lo()
    group = 4 if rows % 4 == 0 else 1

    def body(rg, carry):
        r = [rg * group + u for u in range(group)]
        r0 = [jnp.clip(t - NA_KH // 2, 0, rows - kh) for t in r]
        rows_q = [pl.ds(pl.multiple_of(t * GRID_W, GRID_W), GRID_W) for t in r]
        rows_k = [pl.ds(pl.multiple_of(t * GRID_W, GRID_W), kh * GRID_W) for t in r0]
        q = [q_ref[0, rq, :] * (HEAD_DIM ** -0.5) for rq in rows_q]
        q2 = _each(lambda q: jnp.concatenate([jnp.where(lo, q, 0.0), jnp.where(lo, 0.0, q)], axis=0), q)
        s = [_dot_nt(q2i, k_ref[0, rk, :]) + bias_ref[0, t0 - t + NA_KH - 1]
             for q2i, rk, t0, t in zip(q2, rows_k, r0, r)]
        p = _each(lambda s: jnp.exp(s - jnp.max(s, axis=1, keepdims=True)), s)
        pv = [_dot(pi, v_ref[0, rk, :]) / jnp.sum(pi, axis=1, keepdims=True) for pi, rk in zip(p, rows_k)]
        for rq, pvi in zip(rows_q, pv):
            o_ref[0, rq, :] = jnp.where(lo, pvi[:GRID_W], pvi[GRID_W:]).astype(o_ref.dtype)
        return carry

    lax.fori_loop(0, rows // group, body, 0)


def _attn_d(zc, bias):
    b, n, _ = zc.shape
    npair = D_W // LANES
    q_off = (C_W + 2 * C_KV_HEADS * HEAD_DIM) // LANES
    blk = lambda off: pl.BlockSpec((1, n, LANES), lambda i, j, off=off: (i, 0, off + j))
    return pl.pallas_call(
        _attn_d_kernel,
        grid=(b, npair),
        in_specs=[blk(q_off), blk(q_off + npair), blk(q_off + 2 * npair),
                  pl.BlockSpec((1,) + bias.shape[1:], lambda i, j: (j, 0, 0, 0))],
        out_specs=pl.BlockSpec((1, n, LANES), lambda i, j: (i, 0, j)),
        out_shape=jax.ShapeDtypeStruct((b, n, D_W), BF16),
        compiler_params=_params("parallel", "parallel"),
        name="attn_neighborhood",
    )(zc, zc, zc, bias)


def _rw_prep_kernel(z_ref, hp_ref, hn_ref, mup_ref, mun_ref, w0_ref, w2f_ref, w2b_ref, a0_ref,
                    a2f_ref, a2b_ref, g2_ref, r_o, k_o, v_o, cumf_o, cumb_o, af_o, ab_o, g_o):
    tm = z_ref.shape[1]
    first = pl.program_id(1) == 0
    last = pl.program_id(1) == pl.num_programs(1) - 1
    row = lax.broadcasted_iota(jnp.int32, (tm, 1), 0)
    crow = row % WKV_CHUNK

    def edges(cols):
        return (jnp.where(first, 0.0, hp_ref[0, SUBLANES - 1:SUBLANES, cols]),
                jnp.where(last, 0.0, hn_ref[0, 0:1, cols]))

    def mix(z, z_prev, z_next, cols):
        return z + mup_ref[:, cols] * (z_prev - z) + mun_ref[:, cols] * (z_next - z)

    def shifted(cols):
        z = z_ref[0, :, cols]
        edge_prev, edge_next = edges(cols)
        z_prev = jnp.where(row == 0, edge_prev, pltpu.roll(z, 1, 0))
        z_next = jnp.where(row == tm - 1, edge_next, pltpu.roll(z, tm - 1, 0))
        return mix(z, z_prev, z_next, cols)

    def store_shifted(o_ref, cs, cols):
        z = z_ref[0, :, cols]
        edge_prev, edge_next = edges(cols)
        o_ref[0, :, cs] = mix(z, pltpu.roll(z, 1, 0), pltpu.roll(z, tm - 1, 0), cols)
        o_ref[0, 0:1, cs] = mix(z[0:1], edge_prev, z[1:2], cols)
        o_ref[0, tm - 1:tm, cs] = mix(z[tm - 1:tm], z[tm - 2:tm - 1], edge_next, cols)

    def log_decay(x):
        return -RW_DECAY_SCALE * jax.nn.sigmoid(x)

    def chunk_cumsum(x, rev):
        s = 1
        while s < WKV_CHUNK:
            if rev:
                x = x + jnp.where(crow < WKV_CHUNK - s, pltpu.roll(x, tm - s, 0), 0.0)
            else:
                x = x + jnp.where(crow >= s, pltpu.roll(x, s, 0), 0.0)
            s *= 2
        return x

    base = 3 * B_W
    wl = jnp.tanh(shifted(slice(base, base + LANES))).astype(BF16)
    al = shifted(slice(base + LANES, base + 2 * LANES)).astype(BF16)
    gl = jax.nn.sigmoid(shifted(slice(base + 2 * LANES, base + 4 * LANES))).astype(BF16)

    for s0 in range(0, B_W, LANES):
        cs = slice(s0, s0 + LANES)
        store_shifted(r_o, cs, cs)
        store_shifted(k_o, cs, slice(B_W + s0, B_W + s0 + LANES))
        store_shifted(v_o, cs, slice(2 * B_W + s0, 2 * B_W + s0 + LANES))
        cumf_o[0, :, cs] = chunk_cumsum(log_decay(w0_ref[0:1, cs] + _dot(wl, w2f_ref[:, cs])), False)
        cumb_o[0, :, cs] = chunk_cumsum(log_decay(w0_ref[1:2, cs] + _dot(wl, w2b_ref[:, cs])), True)
        af_o[0, :, cs] = jax.nn.sigmoid(a0_ref[0:1, cs] + _dot(al, a2f_ref[:, cs]))
        ab_o[0, :, cs] = jax.nn.sigmoid(a0_ref[1:2, cs] + _dot(al, a2b_ref[:, cs]))
        g_o[0, :, cs] = _dot(gl, g2_ref[:, cs])


def _rw_prep(zb, mu_prev, mu_next, w0, w2, a0, a2, g2, tm=256):
    b, n, cols = zb.shape
    tm = min(tm, n)
    per = tm // SUBLANES
    pad_cols = lambda t: jnp.pad(t, (0, cols - t.shape[0])).reshape(1, cols)
    rows_f = lambda t: jnp.pad(t, ((0, LANES - RW_LORA), (0, 0))).astype(BF16)
    rows_b = lambda t: jnp.pad(t, ((RW_LORA, LANES - 2 * RW_LORA), (0, 0))).astype(BF16)
    g2p = jnp.pad(g2, ((0, 2 * LANES - RW_GATE_LORA), (0, 0))).astype(BF16)
    const = lambda shape: pl.BlockSpec(shape, lambda i, t: (0,) * len(shape))
    halo_prev = pl.BlockSpec((1, SUBLANES, cols), lambda i, t: (i, jnp.maximum(t * per - 1, 0), 0))
    halo_next = pl.BlockSpec((1, SUBLANES, cols), lambda i, t: (i, jnp.minimum((t + 1) * per, n // SUBLANES - 1), 0))
    tok = jax.ShapeDtypeStruct((b, n, B_W), F32)
    return pl.pallas_call(
        _rw_prep_kernel,
        grid=(b, n // tm),
        in_specs=[pl.BlockSpec((1, tm, cols), lambda i, t: (i, t, 0)), halo_prev, halo_next,
                  const((1, cols)), const((1, cols)), const((2, B_W)), const((LANES, B_W)),
                  const((LANES, B_W)), const((2, B_W)), const((LANES, B_W)), const((LANES, B_W)),
                  const((2 * LANES, B_W))],
        out_specs=[pl.BlockSpec((1, tm, B_W), lambda i, t: (i, t, 0))] * 8,
        out_shape=[tok] * 8,
        compiler_params=_params("parallel", "parallel"),
        name="rwkv_prep",
    )(zb, zb, zb, pad_cols(mu_prev), pad_cols(mu_next), w0, rows_f(w2[0]), rows_b(w2[1]),
      a0, rows_f(a2[0]), rows_b(a2[1]), g2p)


def _head_sum(x, lo):
    return jnp.where(lo, jnp.sum(jnp.where(lo, x, 0.0), axis=1, keepdims=True),
                     jnp.sum(jnp.where(lo, 0.0, x), axis=1, keepdims=True))


def _wkv_chunks(chains):
    c = chains[0][0].shape[0]
    ii = lax.broadcasted_iota(jnp.int32, (c, c), 0)
    jj = lax.broadcasted_iota(jnp.int32, (c, c), 1)
    row = lax.broadcasted_iota(jnp.int32, (c, 1), 0)
    eye_f = (ii == jj).astype(F32)
    masks = {False: (jj <= ii, jj < ii), True: (jj >= ii, jj > ii)}
    lo = _lane_lo()
    ki = lax.broadcasted_iota(jnp.int32, (LANES, LANES), 0)
    vj = lax.broadcasted_iota(jnp.int32, (LANES, LANES), 1)
    same_head = (ki < HEAD_DIM) == (vj < HEAD_DIM)
    diag = ki == vj
    r, k, v, cum, rate, st, k_k, k_a, rev = (list(col) for col in zip(*chains))
    twice = lambda xs: [x for x in xs for _ in range(2)]
    merge = lambda xs: [jnp.where(lo, xs[2 * i], xs[2 * i + 1]) for i in range(len(xs) // 2)]
    incl = twice([masks[x][0] for x in rev])
    strict = twice([masks[x][1] for x in rev])

    kk = _each(lambda k, k_k: k * k_k, k, k_k)
    kk = _each(lambda t: t / jnp.maximum(jnp.sqrt(_head_sum(t * t, lo)), 1e-12), kk)
    kd = _each(lambda k, rate, k_a: k * (1.0 + (rate - 1.0) * k_a), k, rate, k_a)
    b = _each(lambda kk, rate: kk * rate, kk, rate)

    tot = _each(lambda cum, x: cum[0:1] if x else cum[c - 1:c], cum, rev)
    excl = _each(lambda cum, x: jnp.where(row == c - 1, 0.0, pltpu.roll(cum, c - 1, 0)) if x
                 else jnp.where(row == 0, 0.0, pltpu.roll(cum, 1, 0)), cum, rev)
    ar = _each(lambda kk, r, cum, ex: jnp.concatenate([-kk * jnp.exp(ex), r * jnp.exp(cum)], axis=0),
               kk, r, cum, excl)
    ar_h = [jnp.where(m, x, 0.0).astype(BF16) for x in ar for m in (lo, ~lo)]
    ar = _each(lambda t: t.astype(BF16), ar)
    v = _each(lambda t: t.astype(BF16), v)
    g_inv = _each(lambda cum: jnp.exp(-cum), cum)
    bg = twice(_each(lambda b, g: (b * g).astype(BF16), b, g_inv))
    kg = twice(_each(lambda kd, g: (kd * g).astype(BF16), kd, g_inv))
    x1 = _each(_dot_nt, ar_h, bg)
    x2 = _each(_dot_nt, ar_h, kg)
    a_ab = _each(lambda x, m: jnp.where(m, x[:c], 0.0), x1, strict)
    a_rb = _each(lambda x, m: jnp.where(m, x[c:], 0.0).astype(BF16), x1, incl)
    a_k = _each(lambda x, ms, mi: jnp.concatenate([jnp.where(ms, x[:c], 0.0), jnp.where(mi, x[c:], 0.0)],
                                                  axis=0).astype(BF16), x2, strict, incl)

    inv = _each(lambda t: eye_f + t, a_ab)
    pw = _each(lambda t: _dot(t, t), a_ab)
    for _ in range(c.bit_length() - 3):
        pw = _each(lambda t: t.astype(BF16), pw)
        both = _each(lambda inv, pw: _dot(jnp.concatenate([inv.astype(BF16), pw], axis=0), pw), inv, pw)
        inv = _each(lambda inv, t: inv + t[:c], inv, both)
        pw = _each(lambda t: t[c:], both)
    inv = _each(lambda inv, pw: (inv + _dot(inv, pw)).astype(BF16), inv, pw)

    akv = merge(_each(_dot, a_k, twice(v)))
    ars = _each(_dot, ar, st)
    rhs = _each(lambda ars, akv: (ars[:c] + akv[:c]).astype(BF16), ars, akv)
    p = _each(lambda t: t.astype(BF16), merge(_each(_dot, inv, twice(rhs))))
    y = _each(lambda ars, arb, akv: ars[c:] + arb + akv[c:], ars, merge(_each(_dot, a_rb, twice(p))), akv)
    g_end = _each(lambda tot, cum: jnp.exp(tot - cum), tot, cum)
    upd = _each(lambda b, kd, g, p, v: _dot_tn(jnp.concatenate([b * g, kd * g], axis=0),
                                               jnp.concatenate([p, v], axis=0)), b, kd, g_end, p, v)
    g_col = _each(lambda tot: jnp.sum(jnp.where(diag, jnp.exp(tot), 0.0), axis=1, keepdims=True), tot)
    new_st = _each(lambda g, st, upd: jnp.where(same_head, g * st + upd, 0.0), g_col, st, upd)
    return list(zip(y, new_st))


def _rw_scan_kernel(rf, kf, vf, cumf, af, rb, kb, vb, cumb, ab, kk_ref, ka_ref, yf, yb, sf, sb):
    tb = rf.shape[1]
    npair = rf.shape[2] // LANES
    c = WKV_CHUNK
    nch = tb // c
    pair = lambda j: slice(j * LANES, (j + 1) * LANES)

    @pl.when(pl.program_id(2) == 0)
    def _():
        sf[...] = jnp.zeros(sf.shape, F32)
        sb[...] = jnp.zeros(sb.shape, F32)

    def body(ci, carry):
        rows_f = pl.ds(pl.multiple_of(ci * c, c), c)
        rows_b = pl.ds(pl.multiple_of((nch - 1 - ci) * c, c), c)
        chains = []
        for j in range(npair):
            par = (kk_ref[:, pair(j)], ka_ref[:, pair(j)])
            chains.append(tuple(ref[0, rows_f, pair(j)] for ref in (rf, kf, vf, cumf, af)) + (sf[j],) + par + (False,))
            chains.append(tuple(ref[0, rows_b, pair(j)] for ref in (rb, kb, vb, cumb, ab)) + (sb[j],) + par + (True,))
        outs = _wkv_chunks(chains)
        for j in range(npair):
            yf[0, rows_f, pair(j)], sf[j] = outs[2 * j]
            yb[0, rows_b, pair(j)], sb[j] = outs[2 * j + 1]
        return carry

    lax.fori_loop(0, nch, body, 0)


def _rw_scan(r, k, v, cumf, cumb, af, ab, k_k, k_a, hb=20, tb=256):
    b, n, width = r.shape
    tb = min(tb, n)
    nt = n // tb
    hw = hb * HEAD_DIM
    fwd = pl.BlockSpec((1, tb, hw), lambda i, g, t: (i, t, g))
    bwd = pl.BlockSpec((1, tb, hw), lambda i, g, t: (i, nt - 1 - t, g))
    par = pl.BlockSpec((1, hw), lambda i, g, t: (0, g))
    out = jax.ShapeDtypeStruct((b, n, width), F32)
    return pl.pallas_call(
        _rw_scan_kernel,
        grid=(b, width // hw, nt),
        in_specs=[fwd] * 5 + [bwd] * 5 + [par, par],
        out_specs=[fwd, bwd],
        out_shape=[out, out],
        scratch_shapes=[pltpu.VMEM((hw // LANES, LANES, LANES), F32)] * 2,
        compiler_params=_params("parallel", "parallel", "arbitrary"),
        name="rwkv_scan",
    )(r, k, v, cumf, af, r, k, v, cumb, ab, k_k.reshape(1, width), k_a.reshape(1, width))


def _rw_post_kernel(yf, yb, r, k, v, af, ab, g, ka_ref, rk_ref, lnw_ref, lnb_ref, o_ref):
    lo = _lane_lo()
    head_sum = lambda x: _head_sum(x, lo)
    for j in range(o_ref.shape[2] // LANES):
        sl = slice(j * LANES, (j + 1) * LANES)
        y = yf[0, :, sl] + yb[0, :, sl]
        dev = y - head_sum(y) * (1.0 / HEAD_DIM)
        var = head_sum(dev * dev) * (1.0 / HEAD_DIM)
        yn = dev * lax.rsqrt(var + RW_GN_EPS) * lnw_ref[:, sl] + lnb_ref[:, sl]
        kt, k_a = k[0, :, sl], ka_ref[:, sl]
        kd = kt * (1.0 + (af[0, :, sl] - 1.0) * k_a) + kt * (1.0 + (ab[0, :, sl] - 1.0) * k_a)
        bonus = head_sum(r[0, :, sl] * kd * rk_ref[:, sl]) * v[0, :, sl]
        o_ref[0, :, sl] = ((yn + bonus) * g[0, :, sl]).astype(o_ref.dtype)


def _rw_post(yf, yb, r, k, v, af, ab, g, k_a, r_k, ln_w, ln_b, tm=256):
    b, n, width = r.shape
    tm = min(tm, n)
    tok = pl.BlockSpec((1, tm, width), lambda i, t: (i, t, 0))
    par = pl.BlockSpec((1, width), lambda i, t: (0, 0))
    return pl.pallas_call(
        _rw_post_kernel,
        grid=(b, n // tm),
        in_specs=[tok] * 8 + [par] * 4,
        out_specs=tok,
        out_shape=jax.ShapeDtypeStruct((b, n, width), BF16),
        compiler_params=_params("parallel", "parallel"),
        name="rwkv_post",
    )(yf, yb, r, k, v, af, ab, g, k_a.reshape(1, width), r_k.reshape(1, width),
      ln_w.reshape(1, width), ln_b.reshape(1, width))


def _rwkv7_bidir(zb, mu_prev, mu_next, w0, w2, a0, a2, g2, k_k, k_a, r_k, ln_w, ln_b):
    r, k, v, cumf, cumb, af, ab, g = _rw_prep(zb, mu_prev, mu_next, w0, w2, a0, a2, g2)
    yf, yb = _rw_scan(r, k, v, cumf, cumb, af, ab, k_k, k_a)
    return _rw_post(yf, yb, r, k, v, af, ab, g, k_a, r_k, ln_w, ln_b)


def _ffn_half_step(x, ss, g, w1, w3, w2, layer):
    h = (_ffn_up(_rmsnorm(x, g, BF16), None, None, (w1, layer), (w3, layer)) if ss is None
         else _ffn_up(x, ss, g, (w1, layer), (w3, layer)))
    return _mm_res([(h, (w2, layer))], x, 0.5)


def _mix_ab(x, ss, g, b, n, cos, sin, w_in, layer, w_out, mu_prev, mu_next, w0, w2, a0, a2, g2, k_k, k_a,
            r_k, ln_w, ln_b):
    qkv = 3 * A_W
    w_b = jnp.pad(w_in[layer, :, qkv:], ((0, 0), (0, RW_PAD_COLS - (w_in.shape[2] - qkv))))
    za = _proj(x, ss, g, w_in[layer, :, :qkv], qkv, tn=qkv // 3).reshape(b, n, qkv)
    zb = _proj(x, ss, g, w_b, RW_PAD_COLS, tn=RW_PAD_COLS // 2).reshape(b, n, RW_PAD_COLS)
    oa = _attn_a(za, cos, sin).reshape(b * n, A_W)
    ob = _rwkv7_bidir(zb, mu_prev, mu_next, w0, w2, a0, a2, g2, k_k, k_a, r_k, ln_w, ln_b)
    return _mm_res([(oa, w_out[:A_W]), (ob.reshape(b * n, B_W), w_out[A_W:])], x, 1.0, tm=1024)


def _mix_cd(x, ss, g, b, n, cos, sin, w_in, layer, w_out, sink, rpb):
    cols = w_in.shape[2]
    zc = _proj(x, ss, g, (w_in, layer), cols, tn=cols // 2).reshape(b, n, cols)
    oc = _attn_c(zc, sink, cos, sin).reshape(b * n, C_W)
    od = _attn_d(zc, _na_bias_table(rpb, n // GRID_W)).reshape(b * n, D_W)
    return _mm_res([(oc, w_out[:C_W]), (od, w_out[C_W:])], x, 1.0, tm=1024)


def kernel(x, p, ffn1_norm, ffn1_w1, ffn1_w3, ffn1_w2, mix_norm, ffn2_norm, ffn2_w1, ffn2_w3, ffn2_w2, ple_norm, ple_w_gate, ple_w_proj, ab_w_in, ab_w_out, rw_mu_prev, rw_mu_next, rw_w0, rw_w2, rw_a0, rw_a2, rw_g2, rw_k_k, rw_k_a, rw_r_k, rw_ln_w, rw_ln_b, cd_w_in, cd_w_out, c_sink, d_rpb, final_norm):
    b, n, d = x.shape
    depth = p.shape[0]
    cos, sin = _rope_tables(n)
    x = x.reshape(b * n, d)
    p = p.reshape(depth, b * n, -1)
    ss = None
    for i in range(depth):
        j = i // 2
        x, ss = _ffn_half_step(x, ss, ffn1_norm[i], ffn1_w1, ffn1_w3, ffn1_w2, i)
        if i % 2 == 0:
            x, ss = _mix_ab(x, ss, mix_norm[i], b, n, cos, sin, ab_w_in, j, ab_w_out[j], rw_mu_prev[j],
                            rw_mu_next[j], rw_w0[j], rw_w2[j], rw_a0[j], rw_a2[j], rw_g2[j], rw_k_k[j],
                            rw_k_a[j], rw_r_k[j], rw_ln_w[j], rw_ln_b[j])
        else:
            x, ss = _mix_cd(x, ss, mix_norm[i], b, n, cos, sin, cd_w_in, j, cd_w_out[j], c_sink[j], d_rpb[j])
        x, ss = _ffn_half_step(x, ss, ffn2_norm[i], ffn2_w1, ffn2_w3, ffn2_w2, i)
        x, ss = _ple(x, ss, ple_norm[i], (ple_w_gate, i), (p, i), (ple_w_proj, i))
    return _rmsnorm(x, final_norm, F32).reshape(b, n, d)
```

```python
import functools
import math

import jax
import jax.numpy as jnp
from jax import lax
from jax.experimental import pallas as pl
from jax.experimental.pallas import tpu as pltpu

F32 = jnp.float32
BF16 = jnp.bfloat16

HEAD_DIM = 64
LANES = 128
SUBLANES = 8
NORM_EPS = 1e-6
ROPE_THETA = 10000.0
GRID_W = 64
A_HEADS = 12
A_W = A_HEADS * HEAD_DIM
A_HALF = 64
A_DILATIONS = (1, 4, 16)
A_QUERY_ROWS = 64
B_HEADS = 20
B_W = B_HEADS * HEAD_DIM
RW_LORA = 64
RW_GATE_LORA = 192
RW_GN_EPS = 64e-5
RW_DECAY_SCALE = math.exp(-0.5)
RW_PAD_COLS = 3 * B_W + 4 * LANES
C_HEADS = 16
C_KV_HEADS = 2
C_GROUP = C_HEADS // C_KV_HEADS
C_W = C_HEADS * HEAD_DIM
C_HALF = 128
D_HEADS = 16
D_W = D_HEADS * HEAD_DIM
NA_KH = 8
NA_KW = 16
MASKED = -1e30
WKV_CHUNK = 64
VMEM_LIMIT = 56 * 1024 * 1024


def _params(*sem):
    return pltpu.CompilerParams(dimension_semantics=sem, vmem_limit_bytes=VMEM_LIMIT)


def _dot(a, b):
    return jnp.dot(a.astype(BF16), b.astype(BF16), preferred_element_type=F32)


def _dot_nt(a, b):
    return lax.dot_general(a.astype(BF16), b.astype(BF16), (((1,), (1,)), ((), ())),
                           preferred_element_type=F32)


def _dot_tn(a, b):
    return lax.dot_general(a.astype(BF16), b.astype(BF16), (((0,), (0,)), ((), ())),
                           preferred_element_type=F32)


def _each(f, *cols):
    return [f(*xs) for xs in zip(*cols)]


def _rmsnorm_kernel(x_ref, g_ref, o_ref):
    x = x_ref[...]
    ms = jnp.mean(x * x, axis=-1, keepdims=True)
    o_ref[...] = (x * lax.rsqrt(ms + NORM_EPS) * g_ref[...]).astype(o_ref.dtype)


def _rmsnorm(x, g, dtype, tm=512):
    m, d = x.shape
    return pl.pallas_call(
        _rmsnorm_kernel,
        grid=(m // tm,),
        in_specs=[pl.BlockSpec((tm, d), lambda i: (i, 0)),
                  pl.BlockSpec((1, d), lambda i: (0, 0))],
        out_specs=pl.BlockSpec((tm, d), lambda i: (i, 0)),
        out_shape=jax.ShapeDtypeStruct((m, d), dtype),
        compiler_params=_params("parallel"),
        name="rmsnorm",
    )(x, g.reshape(1, d))


def _cast_weight(w_ref, wb_ref):
    rows = 256

    @pl.when(pl.program_id(1) == 0)
    def _():
        def body(i, carry):
            sl = pl.ds(pl.multiple_of(i * rows, rows), rows)
            wb_ref[sl, :] = w_ref[sl, :].astype(BF16)
            return carry

        lax.fori_loop(0, w_ref.shape[0] // rows, body, 0)


def _row_spec(tm, k):
    return pl.BlockSpec((tm, k), lambda j, i: (i, 0))


def _col_spec(w, k, tn, **kw):
    if isinstance(w, tuple):
        layer = w[1]
        return pl.BlockSpec((None, k, tn), lambda j, i: (layer, 0, j), **kw)
    return pl.BlockSpec((k, tn), lambda j, i: (0, j), **kw)


def _arr(w):
    return w[0] if isinstance(w, tuple) else w


def _tile_spec(tm, tn):
    return pl.BlockSpec((tm, tn), lambda j, i: (i, j))


def _normed(x_ref, ss_ref, g_ref):
    ss = ss_ref[0, :, :1]
    for t in range(1, ss_ref.shape[0]):
        ss = ss + ss_ref[t, :, :1]
    rs = lax.rsqrt(ss * (1.0 / x_ref.shape[1]) + NORM_EPS)
    return (x_ref[...] * rs * g_ref[...]).astype(BF16)


def _ss_spec(nt, tm):
    return pl.BlockSpec((nt, tm, LANES), lambda j, i: (0, i, 0))


def _gain_spec(d):
    return pl.BlockSpec((1, d), lambda j, i: (0, 0))


def _ffn_up_kernel(*refs, normed):
    w1_ref, w3_ref, o_ref, w1b, w3b = refs[-5:]
    _cast_weight(w1_ref, w1b)
    _cast_weight(w3_ref, w3b)
    xn = refs[0][...] if normed else _normed(*refs[:3])
    a = jnp.dot(xn, w1b[...], preferred_element_type=F32)
    b = jnp.dot(xn, w3b[...], preferred_element_type=F32)
    o_ref[...] = (a * jax.nn.sigmoid(a) * b).astype(o_ref.dtype)


def _ffn_up(x, ss, g, w1, w3, tm=1024, tn=512):
    m, d = x.shape
    f = _arr(w1).shape[-1]
    normed = ss is None
    lead_specs = [_row_spec(tm, d)] if normed else [_row_spec(tm, d), _ss_spec(ss.shape[0], tm), _gain_spec(d)]
    lead_args = [x] if normed else [x, ss, g.reshape(1, d)]
    return pl.pallas_call(
        functools.partial(_ffn_up_kernel, normed=normed),
        grid=(f // tn, m // tm),
        in_specs=lead_specs + [_col_spec(w1, d, tn), _col_spec(w3, d, tn)],
        out_specs=_tile_spec(tm, tn),
        out_shape=jax.ShapeDtypeStruct((m, f), BF16),
        scratch_shapes=[pltpu.VMEM((d, tn), BF16)] * 2,
        compiler_params=_params("parallel", "arbitrary"),
        name="ffn_up",
    )(*lead_args, _arr(w1), _arr(w3))


def _proj_kernel(x_ref, ss_ref, g_ref, w_ref, o_ref, wb):
    _cast_weight(w_ref, wb)
    o_ref[...] = jnp.dot(_normed(x_ref, ss_ref, g_ref), wb[...], preferred_element_type=F32)


def _proj(x, ss, g, w, ncols, tn, tm=512):
    m, d = x.shape
    return pl.pallas_call(
        _proj_kernel,
        grid=(ncols // tn, m // tm),
        in_specs=[_row_spec(tm, d), _ss_spec(ss.shape[0], tm), _gain_spec(d),
                  _col_spec(w, d, tn, pipeline_mode=pl.Buffered(1))],
        out_specs=_tile_spec(tm, tn),
        out_shape=jax.ShapeDtypeStruct((m, ncols), F32),
        scratch_shapes=[pltpu.VMEM((d, tn), BF16)],
        compiler_params=_params("parallel", "arbitrary"),
        name="proj",
    )(x, ss, g.reshape(1, d), _arr(w))


def _tile_ss(x_new, ss_ref):
    ss_ref[0] = jnp.broadcast_to(jnp.sum(x_new * x_new, axis=1, keepdims=True), ss_ref.shape[1:])


def _ss_out(n, m, tm, tn):
    return (pl.BlockSpec((1, tm, LANES), lambda j, i: (j, i, 0)),
            jax.ShapeDtypeStruct((n // tn, m, LANES), F32))


def _ple_kernel(x_ref, ss_ref, g_ref, w_ref, p_ref, wp_ref, o_ref, sso_ref, wb, wpb):
    _cast_weight(w_ref, wb)
    _cast_weight(wp_ref, wpb)
    gate = jax.nn.sigmoid(jnp.dot(_normed(x_ref, ss_ref, g_ref), wb[...], preferred_element_type=F32))
    emb = jnp.dot(p_ref[...].astype(BF16), wpb[...], preferred_element_type=F32)
    x_new = x_ref[...] + emb * gate
    o_ref[...] = x_new
    _tile_ss(x_new, sso_ref)


def _ple(x, ss, g, w_gate, p, w_proj, tm=512):
    m, d = x.shape
    pd = p[0].shape[-1]
    ss_spec, ss_shape = _ss_out(d, m, tm, d)
    once = dict(pipeline_mode=pl.Buffered(1))
    return pl.pallas_call(
        _ple_kernel,
        grid=(1, m // tm),
        in_specs=[_row_spec(tm, d), _ss_spec(ss.shape[0], tm), _gain_spec(d), _col_spec(w_gate, d, d, **once),
                  pl.BlockSpec((None, tm, pd), lambda j, i: (p[1], i, 0)), _col_spec(w_proj, pd, d, **once)],
        out_specs=[_tile_spec(tm, d), ss_spec],
        out_shape=[jax.ShapeDtypeStruct((m, d), F32), ss_shape],
        scratch_shapes=[pltpu.VMEM((d, d), BF16), pltpu.VMEM((pd, d), BF16)],
        compiler_params=_params("parallel", "arbitrary"),
        name="ple",
    )(x, ss, g.reshape(1, d), _arr(w_gate), p[0], _arr(w_proj))


def _mm_res_kernel(*refs, n_pairs, scale):
    res_ref, o_ref, sso_ref = refs[2 * n_pairs:2 * n_pairs + 3]
    scratch = refs[2 * n_pairs + 3:]
    acc = None
    for t in range(n_pairs):
        _cast_weight(refs[2 * t + 1], scratch[t])
        part = jnp.dot(refs[2 * t][...], scratch[t][...], preferred_element_type=F32)
        acc = part if acc is None else acc + part
    x_new = res_ref[...] + scale * acc
    o_ref[...] = x_new
    _tile_ss(x_new, sso_ref)


def _mm_res(pairs, res, scale, tm=512, tn=512):
    m, n = res.shape
    in_specs, args, scratch = [], [], []
    once = dict(pipeline_mode=pl.Buffered(1)) if tn == n else {}
    for a, w in pairs:
        k = a.shape[1]
        in_specs += [_row_spec(tm, k), _col_spec(w, k, tn, **once)]
        args += [a, _arr(w)]
        scratch.append(pltpu.VMEM((k, tn), BF16))
    in_specs.append(_tile_spec(tm, tn))
    ss_spec, ss_shape = _ss_out(n, m, tm, tn)
    return pl.pallas_call(
        functools.partial(_mm_res_kernel, n_pairs=len(pairs), scale=scale),
        grid=(n // tn, m // tm),
        in_specs=in_specs,
        out_specs=[_tile_spec(tm, tn), ss_spec],
        out_shape=[jax.ShapeDtypeStruct((m, n), F32), ss_shape],
        scratch_shapes=scratch,
        compiler_params=_params("parallel", "arbitrary"),
        name="mm_res",
    )(*args, res)


def _lane_lo(shape=(1, LANES)):
    return lax.broadcasted_iota(jnp.int32, shape, len(shape) - 1) < HEAD_DIM


def _rope_tables(n):
    half = HEAD_DIM // 2
    inv_freq = ROPE_THETA ** (-jnp.arange(half, dtype=F32) / half)
    ang = jnp.arange(n, dtype=F32)[:, None] * inv_freq[None, :]
    cos, sin = jnp.cos(ang), jnp.sin(ang)
    return (jnp.concatenate([cos, cos, cos, cos], axis=-1),
            jnp.concatenate([-sin, sin, -sin, sin], axis=-1))


def _rope_tile(x, cos, sin):
    lane = lax.broadcasted_iota(jnp.int32, (1, LANES), 1)
    first_half = (lane % HEAD_DIM) < (HEAD_DIM // 2)
    partner = jnp.where(first_half, pltpu.roll(x, LANES - HEAD_DIM // 2, 1),
                        pltpu.roll(x, HEAD_DIM // 2, 1))
    return x * cos + partner * sin


def _attn_a_kernel(q_ref, k_ref, v_ref, cos_ref, sin_ref, o_ref, qs, ks, m_s, l_s, acc_s):
    n = q_ref.shape[1]
    lo = _lane_lo()
    rb = 256

    def rope_body(i, carry):
        sl = pl.ds(pl.multiple_of(i * rb, rb), rb)
        c, s = cos_ref[sl, :], sin_ref[sl, :]
        qs[sl, :] = _rope_tile(q_ref[0, sl, :], c, s) * (HEAD_DIM ** -0.5)
        ks[sl, :] = _rope_tile(k_ref[0, sl, :], c, s)
        return carry

    lax.fori_loop(0, n // rb, rope_body, 0)
    m_s[...] = jnp.full(m_s.shape, MASKED, F32)
    l_s[...] = jnp.zeros(l_s.shape, F32)
    acc_s[...] = jnp.zeros(acc_s.shape, F32)

    group = 8

    for dil in A_DILATIONS:
        m = n // dil
        qb_rows = m if m <= 2 * A_QUERY_ROWS else A_QUERY_ROWS
        span = min(qb_rows + 2 * A_HALF, m)
        nqb = m // qb_rows

        def blocks(items, dil=dil, m=m, qb_rows=qb_rows, span=span):
            q0 = [qb * qb_rows for _, qb in items]
            k0 = [jnp.clip(t - A_HALF, 0, m - span) for t in q0]
            if dil == 1:
                rows_q = [pl.ds(pl.multiple_of(t, qb_rows), qb_rows) for t in q0]
                rows_k = [pl.ds(pl.multiple_of(t, A_HALF), span) for t in k0]
            else:
                rows_q = [pl.ds(r + t * dil, qb_rows, stride=dil) for (r, _), t in zip(items, q0)]
                rows_k = [pl.ds(r + t * dil, span, stride=dil) for (r, _), t in zip(items, k0)]
            unstack = lambda t: jnp.where(lo, t[:qb_rows], t[qb_rows:])
            q2 = [jnp.concatenate([jnp.where(lo, t, 0.0), jnp.where(lo, 0.0, t)], axis=0).astype(BF16)
                  for t in (qs[rq, :] for rq in rows_q)]
            kb = [ks[rk, :].astype(BF16) for rk in rows_k]
            vb = [v_ref[0, rk, :].astype(BF16) for rk in rows_k]
            m_old = [m_s[rq, :] for rq in rows_q]
            l_old = [l_s[rq, :] for rq in rows_q]
            acc_old = [acc_s[rq, :] for rq in rows_q]
            qrow = lax.broadcasted_iota(jnp.int32, (2 * qb_rows, 1), 0) % qb_rows
            kcol = lax.broadcasted_iota(jnp.int32, (1, span), 1)
            mask = _each(lambda a, c: jnp.abs(a + qrow - c - kcol) <= A_HALF, q0, k0)
            s = _each(lambda q, kb, mk: jnp.where(mk, _dot_nt(q, kb), MASKED), q2, kb, mask)
            mn = _each(lambda mo, s: jnp.maximum(jnp.concatenate([mo[:, :1], mo[:, LANES - 1:]], axis=0),
                                                 jnp.max(s, axis=1, keepdims=True)), m_old, s)
            p = _each(lambda s, mn: jnp.exp(s - mn), s, mn)
            psum = _each(lambda p: jnp.sum(p, axis=1, keepdims=True), p)
            pv = _each(_dot, p, vb)
            m_new = _each(unstack, mn)
            alpha = _each(lambda mo, mn: jnp.exp(mo - mn), m_old, m_new)
            for rq, mnew, al, lold, ps, ao, pvi in zip(rows_q, m_new, alpha, l_old, psum, acc_old, pv):
                m_s[rq, :] = mnew
                l_s[rq, :] = al * lold + unstack(ps)
                acc_s[rq, :] = al * ao + unstack(pvi)

        per = math.gcd(group, dil * nqb)

        def group_body(it, carry, blocks=blocks, dil=dil, per=per):
            ts = [it * per + u for u in range(per)]
            blocks([(t & (dil - 1), t >> (dil.bit_length() - 1)) for t in ts])
            return carry

        lax.fori_loop(0, dil * nqb // per, group_body, 0)

    def out_body(i, carry):
        sl = pl.ds(pl.multiple_of(i * rb, rb), rb)
        o_ref[0, sl, :] = (acc_s[sl, :] / l_s[sl, :]).astype(o_ref.dtype)
        return carry

    lax.fori_loop(0, n // rb, out_body, 0)


def _attn_a(za, cos, sin):
    b, n, _ = za.shape
    npair = A_W // LANES
    blk = lambda off: pl.BlockSpec((1, n, LANES), lambda i, j, off=off: (i, 0, off + j))
    tab = pl.BlockSpec((n, LANES), lambda i, j: (0, 0))
    return pl.pallas_call(
        _attn_a_kernel,
        grid=(b, npair),
        in_specs=[blk(0), blk(npair), blk(2 * npair), tab, tab],
        out_specs=pl.BlockSpec((1, n, LANES), lambda i, j: (i, 0, j)),
        out_shape=jax.ShapeDtypeStruct((b, n, A_W), BF16),
        scratch_shapes=[pltpu.VMEM((n, LANES), F32)] * 5,
        compiler_params=_params("parallel", "parallel"),
        name="attn_dilated",
    )(za, za, za, cos, sin)


def _attn_c_kernel(sink_ref, q_ref, k_ref, v_ref, cosq_ref, sinq_ref, cos_ref, sin_ref, o_ref,
                   klo, khi, vlo, vhi):
    n = k_ref.shape[1]
    qb_rows = q_ref.shape[1]
    span = qb_rows + 2 * C_HALF
    hk = pl.program_id(1)
    qb = pl.program_id(2)
    lo = _lane_lo()

    @pl.when(qb == 0)
    def _():
        rb = 256

        def body(i, carry):
            sl = pl.ds(pl.multiple_of(i * rb, rb), rb)
            kx = _rope_tile(k_ref[0, sl, :], cos_ref[sl, :], sin_ref[sl, :])
            kw = pltpu.roll(kx, HEAD_DIM, 1)
            vx = v_ref[0, sl, :]
            vw = pltpu.roll(vx, HEAD_DIM, 1)
            first = hk == 0
            klo[sl, :] = jnp.where(lo, jnp.where(first, kx, kw), 0.0).astype(BF16)
            khi[sl, :] = jnp.where(lo, 0.0, jnp.where(first, kw, kx)).astype(BF16)
            vlo[sl, :] = jnp.where(lo, jnp.where(first, vx, vw), 0.0).astype(BF16)
            vhi[sl, :] = jnp.where(lo, 0.0, jnp.where(first, vw, vx)).astype(BF16)
            return carry

        lax.fori_loop(0, n // rb, body, 0)

    q0 = qb * qb_rows
    k0 = pl.multiple_of(jnp.clip(q0 - C_HALF, 0, n - span), C_HALF)
    rows_k = pl.ds(k0, span)
    kl, kh = klo[rows_k, :], khi[rows_k, :]
    vl, vh = vlo[rows_k, :], vhi[rows_k, :]
    qpos = q0 + lax.broadcasted_iota(jnp.int32, (qb_rows, 1), 0)
    kpos = k0 + lax.broadcasted_iota(jnp.int32, (1, span), 1)
    mask = jnp.abs(qpos - kpos) <= C_HALF
    cq, sq = cosq_ref[...], sinq_ref[...]

    ntile = q_ref.shape[2] // LANES
    qt = [_rope_tile(q_ref[0, :, j * LANES:(j + 1) * LANES], cq, sq) * (HEAD_DIM ** -0.5) for j in range(ntile)]
    qt = [t for t in qt for _ in range(2)]
    kk = [kl, kh] * ntile
    sk = [sink_ref[hk * C_GROUP + h] for h in range(2 * ntile)]
    s = _each(lambda q, k: jnp.where(mask, _dot_nt(q, k), MASKED), qt, kk)
    mx = _each(lambda s, sk: jnp.maximum(jnp.max(s, axis=1, keepdims=True), sk), s, sk)
    p = _each(lambda s, mx: jnp.exp(s - mx), s, mx)
    den = _each(lambda p, sk, mx: jnp.sum(p, axis=1, keepdims=True) + jnp.exp(sk - mx), p, sk, mx)
    for j in range(ntile):
        pv = _dot(p[2 * j], vl) + _dot(p[2 * j + 1], vh)
        o_ref[0, :, j * LANES:(j + 1) * LANES] = (pv / jnp.where(lo, den[2 * j], den[2 * j + 1])).astype(o_ref.dtype)


def _attn_c(zc, sink, cos, sin, qb_rows=128):
    b, n, _ = zc.shape
    gw = C_GROUP * HEAD_DIM
    kcol = C_W // LANES
    full = lambda off: pl.BlockSpec((1, n, LANES), lambda i, h, t, off=off: (i, 0, off))
    tabq = pl.BlockSpec((qb_rows, LANES), lambda i, h, t: (t, 0))
    tab = pl.BlockSpec((n, LANES), lambda i, h, t: (0, 0))
    return pl.pallas_call(
        _attn_c_kernel,
        grid=(b, C_KV_HEADS, n // qb_rows),
        in_specs=[pl.BlockSpec(memory_space=pltpu.SMEM),
                  pl.BlockSpec((1, qb_rows, gw), lambda i, h, t: (i, t, h)),
                  full(kcol), full(kcol + 1), tabq, tabq, tab, tab],
        out_specs=pl.BlockSpec((1, qb_rows, gw), lambda i, h, t: (i, t, h)),
        out_shape=jax.ShapeDtypeStruct((b, n, C_W), BF16),
        scratch_shapes=[pltpu.VMEM((n, LANES), BF16)] * 4,
        compiler_params=_params("parallel", "parallel", "arbitrary"),
        name="attn_gqa_sink",
    )(sink.reshape(-1), zc, zc, zc, cos, sin, cos, sin)


def _na_bias_table(rpb, rows):
    kh = min(NA_KH, rows)
    qc = jnp.arange(GRID_W)[:, None]
    kc = jnp.arange(GRID_W)[None, :]
    qc0 = jnp.clip(qc - NA_KW // 2, 0, GRID_W - NA_KW)
    valid = (kc >= qc0) & (kc < qc0 + NA_KW)
    ci = jnp.clip(kc - qc, -(NA_KW - 1), NA_KW - 1) + NA_KW - 1
    nci = 2 * NA_KW - 1
    pairs = rpb.astype(F32).reshape(D_HEADS // 2, 2, 2 * NA_KH - 1, nci)
    pairs = jnp.concatenate([pairs, jnp.full(pairs.shape[:-1] + (1,), MASKED, F32)], axis=-1)
    by_row = jnp.stack([pairs[:, :, off:off + kh] for off in range(NA_KH)], axis=1)
    cls = jnp.where(valid, ci, nci)
    onehot = (cls[None] == jnp.arange(nci + 1)[:, None, None]).astype(F32)
    bias = jnp.einsum('poekc,cqj->poeqkj', by_row, onehot, precision=lax.Precision.HIGHEST)
    return bias.reshape(D_HEADS // 2, NA_KH, 2 * GRID_W, kh * GRID_W)


def _attn_d_kernel(q_ref, k_ref, v_ref, bias_ref, o_ref):
    n = q_ref.shape[1]
    rows = n // GRID_W
    kh = min(NA_KH, rows)
    lo = _lane_lo()
    group = 4 if rows % 4 == 0 else 1

    def body(rg, carry):
        r = [rg * group + u for u in range(group)]
        r0 = [jnp.clip(t - NA_KH // 2, 0, rows - kh) for t in r]
        rows_q = [pl.ds(pl.multiple_of(t * GRID_W, GRID_W), GRID_W) for t in r]
        rows_k = [pl.ds(pl.multiple_of(t * GRID_W, GRID_W), kh * GRID_W) for t in r0]
        q = [q_ref[0, rq, :] * (HEAD_DIM ** -0.5) for rq in rows_q]
        q2 = _each(lambda q: jnp.concatenate([jnp.where(lo, q, 0.0), jnp.where(lo, 0.0, q)], axis=0), q)
        s = [_dot_nt(q2i, k_ref[0, rk, :]) + bias_ref[0, t0 - t + NA_KH - 1]
             for q2i, rk, t0, t in zip(q2, rows_k, r0, r)]
        p = _each(lambda s: jnp.exp(s - jnp.max(s, axis=1, keepdims=True)), s)
        pv = [_dot(pi, v_ref[0, rk, :]) / jnp.sum(pi, axis=1, keepdims=True) for pi, rk in zip(p, rows_k)]
        for rq, pvi in zip(rows_q, pv):
            o_ref[0, rq, :] = jnp.where(lo, pvi[:GRID_W], pvi[GRID_W:]).astype(o_ref.dtype)
        return carry

    lax.fori_loop(0, rows // group, body, 0)


def _attn_d(zc, bias):
    b, n, _ = zc.shape
    npair = D_W // LANES
    q_off = (C_W + 2 * C_KV_HEADS * HEAD_DIM) // LANES
    blk = lambda off: pl.BlockSpec((1, n, LANES), lambda i, j, off=off: (i, 0, off + j))
    return pl.pallas_call(
        _attn_d_kernel,
        grid=(b, npair),
        in_specs=[blk(q_off), blk(q_off + npair), blk(q_off + 2 * npair),
                  pl.BlockSpec((1,) + bias.shape[1:], lambda i, j: (j, 0, 0, 0))],
        out_specs=pl.BlockSpec((1, n, LANES), lambda i, j: (i, 0, j)),
        out_shape=jax.ShapeDtypeStruct((b, n, D_W), BF16),
        compiler_params=_params("parallel", "parallel"),
        name="attn_neighborhood",
    )(zc, zc, zc, bias)


def _rw_prep_kernel(z_ref, hp_ref, hn_ref, mup_ref, mun_ref, w0_ref, w2f_ref, w2b_ref, a0_ref,
                    a2f_ref, a2b_ref, g2_ref, r_o, k_o, v_o, cumf_o, cumb_o, af_o, ab_o, g_o):
    tm = z_ref.shape[1]
    first = pl.program_id(1) == 0
    last = pl.program_id(1) == pl.num_programs(1) - 1
    row = lax.broadcasted_iota(jnp.int32, (tm, 1), 0)
    crow = row % WKV_CHUNK

    def edges(cols):
        return (jnp.where(first, 0.0, hp_ref[0, SUBLANES - 1:SUBLANES, cols]),
                jnp.where(last, 0.0, hn_ref[0, 0:1, cols]))

    def mix(z, z_prev, z_next, cols):
        return z + mup_ref[:, cols] * (z_prev - z) + mun_ref[:, cols] * (z_next - z)

    def shifted(cols):
        z = z_ref[0, :, cols]
        edge_prev, edge_next = edges(cols)
        z_prev = jnp.where(row == 0, edge_prev, pltpu.roll(z, 1, 0))
        z_next = jnp.where(row == tm - 1, edge_next, pltpu.roll(z, tm - 1, 0))
        return mix(z, z_prev, z_next, cols)

    def store_shifted(o_ref, cs, cols):
        z = z_ref[0, :, cols]
        edge_prev, edge_next = edges(cols)
        o_ref[0, :, cs] = mix(z, pltpu.roll(z, 1, 0), pltpu.roll(z, tm - 1, 0), cols)
        o_ref[0, 0:1, cs] = mix(z[0:1], edge_prev, z[1:2], cols)
        o_ref[0, tm - 1:tm, cs] = mix(z[tm - 1:tm], z[tm - 2:tm - 1], edge_next, cols)

    def log_decay(x):
        return -RW_DECAY_SCALE * jax.nn.sigmoid(x)

    def chunk_cumsum(x, rev):
        s = 1
        while s < WKV_CHUNK:
            if rev:
                x = x + jnp.where(crow < WKV_CHUNK - s, pltpu.roll(x, tm - s, 0), 0.0)
            else:
                x = x + jnp.where(crow >= s, pltpu.roll(x, s, 0), 0.0)
            s *= 2
        return x

    base = 3 * B_W
    wl = jnp.tanh(shifted(slice(base, base + LANES))).astype(BF16)
    al = shifted(slice(base + LANES, base + 2 * LANES)).astype(BF16)
    gl = jax.nn.sigmoid(shifted(slice(base + 2 * LANES, base + 4 * LANES))).astype(BF16)

    for s0 in range(0, B_W, LANES):
        cs = slice(s0, s0 + LANES)
        store_shifted(r_o, cs, cs)
        store_shifted(k_o, cs, slice(B_W + s0, B_W + s0 + LANES))
        store_shifted(v_o, cs, slice(2 * B_W + s0, 2 * B_W + s0 + LANES))
        cumf_o[0, :, cs] = chunk_cumsum(log_decay(w0_ref[0:1, cs] + _dot(wl, w2f_ref[:, cs])), False)
        cumb_o[0, :, cs] = chunk_cumsum(log_decay(w0_ref[1:2, cs] + _dot(wl, w2b_ref[:, cs])), True)
        af_o[0, :, cs] = jax.nn.sigmoid(a0_ref[0:1, cs] + _dot(al, a2f_ref[:, cs]))
        ab_o[0, :, cs] = jax.nn.sigmoid(a0_ref[1:2, cs] + _dot(al, a2b_ref[:, cs]))
        g_o[0, :, cs] = _dot(gl, g2_ref[:, cs])


def _rw_prep(zb, mu_prev, mu_next, w0, w2, a0, a2, g2, tm=256):
    b, n, cols = zb.shape
    tm = min(tm, n)
    per = tm // SUBLANES
    pad_cols = lambda t: jnp.pad(t, (0, cols - t.shape[0])).reshape(1, cols)
    rows_f = lambda t: jnp.pad(t, ((0, LANES - RW_LORA), (0, 0))).astype(BF16)
    rows_b = lambda t: jnp.pad(t, ((RW_LORA, LANES - 2 * RW_LORA), (0, 0))).astype(BF16)
    g2p = jnp.pad(g2, ((0, 2 * LANES - RW_GATE_LORA), (0, 0))).astype(BF16)
    const = lambda shape: pl.BlockSpec(shape, lambda i, t: (0,) * len(shape))
    halo_prev = pl.BlockSpec((1, SUBLANES, cols), lambda i, t: (i, jnp.maximum(t * per - 1, 0), 0))
    halo_next = pl.BlockSpec((1, SUBLANES, cols), lambda i, t: (i, jnp.minimum((t + 1) * per, n // SUBLANES - 1), 0))
    tok = jax.ShapeDtypeStruct((b, n, B_W), F32)
    return pl.pallas_call(
        _rw_prep_kernel,
        grid=(b, n // tm),
        in_specs=[pl.BlockSpec((1, tm, cols), lambda i, t: (i, t, 0)), halo_prev, halo_next,
                  const((1, cols)), const((1, cols)), const((2, B_W)), const((LANES, B_W)),
                  const((LANES, B_W)), const((2, B_W)), const((LANES, B_W)), const((LANES, B_W)),
                  const((2 * LANES, B_W))],
        out_specs=[pl.BlockSpec((1, tm, B_W), lambda i, t: (i, t, 0))] * 8,
        out_shape=[tok] * 8,
        compiler_params=_params("parallel", "parallel"),
        name="rwkv_prep",
    )(zb, zb, zb, pad_cols(mu_prev), pad_cols(mu_next), w0, rows_f(w2[0]), rows_b(w2[1]),
      a0, rows_f(a2[0]), rows_b(a2[1]), g2p)


def _head_sum(x, lo):
    return jnp.where(lo, jnp.sum(jnp.where(lo, x, 0.0), axis=1, keepdims=True),
                     jnp.sum(jnp.where(lo, 0.0, x), axis=1, keepdims=True))


def _wkv_chunks(chains):
    c = chains[0][0].shape[0]
    ii = lax.broadcasted_iota(jnp.int32, (c, c), 0)
    jj = lax.broadcasted_iota(jnp.int32, (c, c), 1)
    row = lax.broadcasted_iota(jnp.int32, (c, 1), 0)
    eye_f = (ii == jj).astype(F32)
    masks = {False: (jj <= ii, jj < ii), True: (jj >= ii, jj > ii)}
    lo = _lane_lo()
    ki = lax.broadcasted_iota(jnp.int32, (LANES, LANES), 0)
    vj = lax.broadcasted_iota(jnp.int32, (LANES, LANES), 1)
    same_head = (ki < HEAD_DIM) == (vj < HEAD_DIM)
    diag = ki == vj
    r, k, v, cum, rate, st, k_k, k_a, rev = (list(col) for col in zip(*chains))
    twice = lambda xs: [x for x in xs for _ in range(2)]
    merge = lambda xs: [jnp.where(lo, xs[2 * i], xs[2 * i + 1]) for i in range(len(xs) // 2)]
    incl = twice([masks[x][0] for x in rev])
    strict = twice([masks[x][1] for x in rev])

    kk = _each(lambda k, k_k: k * k_k, k, k_k)
    kk = _each(lambda t: t / jnp.maximum(jnp.sqrt(_head_sum(t * t, lo)), 1e-12), kk)
    kd = _each(lambda k, rate, k_a: k * (1.0 + (rate - 1.0) * k_a), k, rate, k_a)
    b = _each(lambda kk, rate: kk * rate, kk, rate)

    tot = _each(lambda cum, x: cum[0:1] if x else cum[c - 1:c], cum, rev)
    excl = _each(lambda cum, x: jnp.where(row == c - 1, 0.0, pltpu.roll(cum, c - 1, 0)) if x
                 else jnp.where(row == 0, 0.0, pltpu.roll(cum, 1, 0)), cum, rev)
    ar = _each(lambda kk, r, cum, ex: jnp.concatenate([-kk * jnp.exp(ex), r * jnp.exp(cum)], axis=0),
               kk, r, cum, excl)
    ar_h = [jnp.where(m, x, 0.0).astype(BF16) for x in ar for m in (lo, ~lo)]
    ar = _each(lambda t: t.astype(BF16), ar)
    v = _each(lambda t: t.astype(BF16), v)
    g_inv = _each(lambda cum: jnp.exp(-cum), cum)
    bg = twice(_each(lambda b, g: (b * g).astype(BF16), b, g_inv))
    kg = twice(_each(lambda kd, g: (kd * g).astype(BF16), kd, g_inv))
    x1 = _each(_dot_nt, ar_h, bg)
    x2 = _each(_dot_nt, ar_h, kg)
    a_ab = _each(lambda x, m: jnp.where(m, x[:c], 0.0), x1, strict)
    a_rb = _each(lambda x, m: jnp.where(m, x[c:], 0.0).astype(BF16), x1, incl)
    a_k = _each(lambda x, ms, mi: jnp.concatenate([jnp.where(ms, x[:c], 0.0), jnp.where(mi, x[c:], 0.0)],
                                                  axis=0).astype(BF16), x2, strict, incl)

    inv = _each(lambda t: eye_f + t, a_ab)
    pw = _each(lambda t: _dot(t, t), a_ab)
    for _ in range(c.bit_length() - 3):
        pw = _each(lambda t: t.astype(BF16), pw)
        both = _each(lambda inv, pw: _dot(jnp.concatenate([inv.astype(BF16), pw], axis=0), pw), inv, pw)
        inv = _each(lambda inv, t: inv + t[:c], inv, both)
        pw = _each(lambda t: t[c:], both)
    inv = _each(lambda inv, pw: (inv + _dot(inv, pw)).astype(BF16), inv, pw)

    akv = merge(_each(_dot, a_k, twice(v)))
    ars = _each(_dot, ar, st)
    rhs = _each(lambda ars, akv: (ars[:c] + akv[:c]).astype(BF16), ars, akv)
    p = _each(lambda t: t.astype(BF16), merge(_each(_dot, inv, twice(rhs))))
    y = _each(lambda ars, arb, akv: ars[c:] + arb + akv[c:], ars, merge(_each(_dot, a_rb, twice(p))), akv)
    g_end = _each(lambda tot, cum: jnp.exp(tot - cum), tot, cum)
    upd = _each(lambda b, kd, g, p, v: _dot_tn(jnp.concatenate([b * g, kd * g], axis=0),
                                               jnp.concatenate([p, v], axis=0)), b, kd, g_end, p, v)
    g_col = _each(lambda tot: jnp.sum(jnp.where(diag, jnp.exp(tot), 0.0), axis=1, keepdims=True), tot)
    new_st = _each(lambda g, st, upd: jnp.where(same_head, g * st + upd, 0.0), g_col, st, upd)
    return list(zip(y, new_st))


def _rw_scan_kernel(rf, kf, vf, cumf, af, rb, kb, vb, cumb, ab, kk_ref, ka_ref, yf, yb, sf, sb):
    tb = rf.shape[1]
    npair = rf.shape[2] // LANES
    c = WKV_CHUNK
    nch = tb // c
    pair = lambda j: slice(j * LANES, (j + 1) * LANES)

    @pl.when(pl.program_id(2) == 0)
    def _():
        sf[...] = jnp.zeros(sf.shape, F32)
        sb[...] = jnp.zeros(sb.shape, F32)

    def body(ci, carry):
        rows_f = pl.ds(pl.multiple_of(ci * c, c), c)
        rows_b = pl.ds(pl.multiple_of((nch - 1 - ci) * c, c), c)
        chains = []
        for j in range(npair):
            par = (kk_ref[:, pair(j)], ka_ref[:, pair(j)])
            chains.append(tuple(ref[0, rows_f, pair(j)] for ref in (rf, kf, vf, cumf, af)) + (sf[j],) + par + (False,))
            chains.append(tuple(ref[0, rows_b, pair(j)] for ref in (rb, kb, vb, cumb, ab)) + (sb[j],) + par + (True,))
        outs = _wkv_chunks(chains)
        for j in range(npair):
            yf[0, rows_f, pair(j)], sf[j] = outs[2 * j]
            yb[0, rows_b, pair(j)], sb[j] = outs[2 * j + 1]
        return carry

    lax.fori_loop(0, nch, body, 0)


def _rw_scan(r, k, v, cumf, cumb, af, ab, k_k, k_a, hb=20, tb=256):
    b, n, width = r.shape
    tb = min(tb, n)
    nt = n // tb
    hw = hb * HEAD_DIM
    fwd = pl.BlockSpec((1, tb, hw), lambda i, g, t: (i, t, g))
    bwd = pl.BlockSpec((1, tb, hw), lambda i, g, t: (i, nt - 1 - t, g))
    par = pl.BlockSpec((1, hw), lambda i, g, t: (0, g))
    out = jax.ShapeDtypeStruct((b, n, width), F32)
    return pl.pallas_call(
        _rw_scan_kernel,
        grid=(b, width // hw, nt),
        in_specs=[fwd] * 5 + [bwd] * 5 + [par, par],
        out_specs=[fwd, bwd],
        out_shape=[out, out],
        scratch_shapes=[pltpu.VMEM((hw // LANES, LANES, LANES), F32)] * 2,
        compiler_params=_params("parallel", "parallel", "arbitrary"),
        name="rwkv_scan",
    )(r, k, v, cumf, af, r, k, v, cumb, ab, k_k.reshape(1, width), k_a.reshape(1, width))


def _rw_post_kernel(yf, yb, r, k, v, af, ab, g, ka_ref, rk_ref, lnw_ref, lnb_ref, o_ref):
    lo = _lane_lo()
    head_sum = lambda x: _head_sum(x, lo)
    for j in range(o_ref.shape[2] // LANES):
        sl = slice(j * LANES, (j + 1) * LANES)
        y = yf[0, :, sl] + yb[0, :, sl]
        dev = y - head_sum(y) * (1.0 / HEAD_DIM)
        var = head_sum(dev * dev) * (1.0 / HEAD_DIM)
        yn = dev * lax.rsqrt(var + RW_GN_EPS) * lnw_ref[:, sl] + lnb_ref[:, sl]
        kt, k_a = k[0, :, sl], ka_ref[:, sl]
        kd = kt * (1.0 + (af[0, :, sl] - 1.0) * k_a) + kt * (1.0 + (ab[0, :, sl] - 1.0) * k_a)
        bonus = head_sum(r[0, :, sl] * kd * rk_ref[:, sl]) * v[0, :, sl]
        o_ref[0, :, sl] = ((yn + bonus) * g[0, :, sl]).astype(o_ref.dtype)


def _rw_post(yf, yb, r, k, v, af, ab, g, k_a, r_k, ln_w, ln_b, tm=256):
    b, n, width = r.shape
    tm = min(tm, n)
    tok = pl.BlockSpec((1, tm, width), lambda i, t: (i, t, 0))
    par = pl.BlockSpec((1, width), lambda i, t: (0, 0))
    return pl.pallas_call(
        _rw_post_kernel,
        grid=(b, n // tm),
        in_specs=[tok] * 8 + [par] * 4,
        out_specs=tok,
        out_shape=jax.ShapeDtypeStruct((b, n, width), BF16),
        compiler_params=_params("parallel", "parallel"),
        name="rwkv_post",
    )(yf, yb, r, k, v, af, ab, g, k_a.reshape(1, width), r_k.reshape(1, width),
      ln_w.reshape(1, width), ln_b.reshape(1, width))


def _rwkv7_bidir(zb, mu_prev, mu_next, w0, w2, a0, a2, g2, k_k, k_a, r_k, ln_w, ln_b):
    r, k, v, cumf, cumb, af, ab, g = _rw_prep(zb, mu_prev, mu_next, w0, w2, a0, a2, g2)
    yf, yb = _rw_scan(r, k, v, cumf, cumb, af, ab, k_k, k_a)
    return _rw_post(yf, yb, r, k, v, af, ab, g, k_a, r_k, ln_w, ln_b)


def _ffn_half_step(x, ss, g, w1, w3, w2, layer):
    h = (_ffn_up(_rmsnorm(x, g, BF16), None, None, (w1, layer), (w3, layer)) if ss is None
         else _ffn_up(x, ss, g, (w1, layer), (w3, layer)))
    return _mm_res([(h, (w2, layer))], x, 0.5)


def _mix_ab(x, ss, g, b, n, cos, sin, w_in, layer, w_out, mu_prev, mu_next, w0, w2, a0, a2, g2, k_k, k_a,
            r_k, ln_w, ln_b):
    qkv = 3 * A_W
    w_b = jnp.pad(w_in[layer, :, qkv:], ((0, 0), (0, RW_PAD_COLS - (w_in.shape[2] - qkv))))
    za = _proj(x, ss, g, w_in[layer, :, :qkv], qkv, tn=qkv).reshape(b, n, qkv)
    zb = _proj(x, ss, g, w_b, RW_PAD_COLS, tn=RW_PAD_COLS // 2).reshape(b, n, RW_PAD_COLS)
    oa = _attn_a(za, cos, sin).reshape(b * n, A_W)
    ob = _rwkv7_bidir(zb, mu_prev, mu_next, w0, w2, a0, a2, g2, k_k, k_a, r_k, ln_w, ln_b)
    return _mm_res([(oa, w_out[:A_W]), (ob.reshape(b * n, B_W), w_out[A_W:])], x, 1.0, tn=x.shape[1])


def _mix_cd(x, ss, g, b, n, cos, sin, w_in, layer, w_out, sink, rpb):
    cols = w_in.shape[2]
    zc = _proj(x, ss, g, (w_in, layer), cols, tn=cols // 2).reshape(b, n, cols)
    oc = _attn_c(zc, sink, cos, sin).reshape(b * n, C_W)
    od = _attn_d(zc, _na_bias_table(rpb, n // GRID_W)).reshape(b * n, D_W)
    return _mm_res([(oc, w_out[:C_W]), (od, w_out[C_W:])], x, 1.0, tn=x.shape[1])


def kernel(x, p, ffn1_norm, ffn1_w1, ffn1_w3, ffn1_w2, mix_norm, ffn2_norm, ffn2_w1, ffn2_w3, ffn2_w2, ple_norm, ple_w_gate, ple_w_proj, ab_w_in, ab_w_out, rw_mu_prev, rw_mu_next, rw_w0, rw_w2, rw_a0, rw_a2, rw_g2, rw_k_k, rw_k_a, rw_r_k, rw_ln_w, rw_ln_b, cd_w_in, cd_w_out, c_sink, d_rpb, final_norm):
    b, n, d = x.shape
    depth = p.shape[0]
    cos, sin = _rope_tables(n)
    x = x.reshape(b * n, d)
    p = p.reshape(depth, b * n, -1)
    ss = None
    for i in range(depth):
        j = i // 2
        x, ss = _ffn_half_step(x, ss, ffn1_norm[i], ffn1_w1, ffn1_w3, ffn1_w2, i)
        if i % 2 == 0:
            x, ss = _mix_ab(x, ss, mix_norm[i], b, n, cos, sin, ab_w_in, j, ab_w_out[j], rw_mu_prev[j],
                            rw_mu_next[j], rw_w0[j], rw_w2[j], rw_a0[j], rw_a2[j], rw_g2[j], rw_k_k[j],
                            rw_k_a[j], rw_r_k[j], rw_ln_w[j], rw_ln_b[j])
        else:
            x, ss = _mix_cd(x, ss, mix_norm[i], b, n, cos, sin, cd_w_in, j, cd_w_out[j], c_sink[j], d_rpb[j])
        x, ss = _ffn_half_step(x, ss, ffn2_norm[i], ffn2_w1, ffn2_w3, ffn2_w2, i)
        x, ss = _ple(x, ss, ple_norm[i], (ple_w_gate, i), (p, i), (ple_w_proj, i))
    return _rmsnorm(x, final_norm, F32).reshape(b, n, d)
```

```python
import functools
import math

import jax
import jax.numpy as jnp
from jax import lax
from jax.experimental import pallas as pl
from jax.experimental.pallas import tpu as pltpu

F32 = jnp.float32
BF16 = jnp.bfloat16

HEAD_DIM = 64
LANES = 128
SUBLANES = 8
NORM_EPS = 1e-6
ROPE_THETA = 10000.0
GRID_W = 64
A_HEADS = 12
A_W = A_HEADS * HEAD_DIM
A_HALF = 64
A_DILATIONS = (1, 4, 16)
A_QUERY_ROWS = 64
B_HEADS = 20
B_W = B_HEADS * HEAD_DIM
RW_LORA = 64
RW_GATE_LORA = 192
RW_GN_EPS = 64e-5
RW_DECAY_SCALE = math.exp(-0.5)
RW_PAD_COLS = 3 * B_W + 4 * LANES
C_HEADS = 16
C_KV_HEADS = 2
C_GROUP = C_HEADS // C_KV_HEADS
C_W = C_HEADS * HEAD_DIM
C_HALF = 128
D_HEADS = 16
D_W = D_HEADS * HEAD_DIM
NA_KH = 8
NA_KW = 16
MASKED = -1e30
WKV_CHUNK = 64
VMEM_LIMIT = 56 * 1024 * 1024


def _params(*sem):
    return pltpu.CompilerParams(dimension_semantics=sem, vmem_limit_bytes=VMEM_LIMIT)


def _dot(a, b):
    return jnp.dot(a.astype(BF16), b.astype(BF16), preferred_element_type=F32)


def _dot_nt(a, b):
    return lax.dot_general(a.astype(BF16), b.astype(BF16), (((1,), (1,)), ((), ())),
                           preferred_element_type=F32)


def _dot_tn(a, b):
    return lax.dot_general(a.astype(BF16), b.astype(BF16), (((0,), (0,)), ((), ())),
                           preferred_element_type=F32)


def _each(f, *cols):
    return [f(*xs) for xs in zip(*cols)]


def _rmsnorm_kernel(x_ref, g_ref, o_ref):
    x = x_ref[...]
    ms = jnp.mean(x * x, axis=-1, keepdims=True)
    o_ref[...] = (x * lax.rsqrt(ms + NORM_EPS) * g_ref[...]).astype(o_ref.dtype)


def _rmsnorm(x, g, dtype, tm=512):
    m, d = x.shape
    return pl.pallas_call(
        _rmsnorm_kernel,
        grid=(m // tm,),
        in_specs=[pl.BlockSpec((tm, d), lambda i: (i, 0)),
                  pl.BlockSpec((1, d), lambda i: (0, 0))],
        out_specs=pl.BlockSpec((tm, d), lambda i: (i, 0)),
        out_shape=jax.ShapeDtypeStruct((m, d), dtype),
        compiler_params=_params("parallel"),
        name="rmsnorm",
    )(x, g.reshape(1, d))


def _cast_weight(w_ref, wb_ref):
    rows = 256

    @pl.when(pl.program_id(1) == 0)
    def _():
        def body(i, carry):
            sl = pl.ds(pl.multiple_of(i * rows, rows), rows)
            wb_ref[sl, :] = w_ref[sl, :].astype(BF16)
            return carry

        lax.fori_loop(0, w_ref.shape[0] // rows, body, 0)


def _row_spec(tm, k):
    return pl.BlockSpec((tm, k), lambda j, i: (i, 0))


def _col_spec(w, k, tn, **kw):
    if isinstance(w, tuple):
        layer = w[1]
        return pl.BlockSpec((None, k, tn), lambda j, i: (layer, 0, j), **kw)
    return pl.BlockSpec((k, tn), lambda j, i: (0, j), **kw)


def _arr(w):
    return w[0] if isinstance(w, tuple) else w


def _tile_spec(tm, tn):
    return pl.BlockSpec((tm, tn), lambda j, i: (i, j))


def _normed(x_ref, ss_ref, g_ref):
    ss = ss_ref[0, :, :1]
    for t in range(1, ss_ref.shape[0]):
        ss = ss + ss_ref[t, :, :1]
    rs = lax.rsqrt(ss * (1.0 / x_ref.shape[1]) + NORM_EPS)
    return (x_ref[...] * rs * g_ref[...]).astype(BF16)


def _ss_spec(nt, tm):
    return pl.BlockSpec((nt, tm, LANES), lambda j, i: (0, i, 0))


def _gain_spec(d):
    return pl.BlockSpec((1, d), lambda j, i: (0, 0))


def _ffn_up_kernel(xn_ref, w1_ref, w3_ref, o_ref, w1b, w3b):
    _cast_weight(w1_ref, w1b)
    _cast_weight(w3_ref, w3b)
    xn = xn_ref[...]
    a = jnp.dot(xn, w1b[...], preferred_element_type=F32)
    b = jnp.dot(xn, w3b[...], preferred_element_type=F32)
    o_ref[...] = (a * jax.nn.sigmoid(a) * b).astype(o_ref.dtype)


def _ffn_up(xn, w1, w3, tm=1024, tn=512):
    m, d = xn.shape
    f = _arr(w1).shape[-1]
    return pl.pallas_call(
        _ffn_up_kernel,
        grid=(f // tn, m // tm),
        in_specs=[_row_spec(tm, d), _col_spec(w1, d, tn), _col_spec(w3, d, tn)],
        out_specs=_tile_spec(tm, tn),
        out_shape=jax.ShapeDtypeStruct((m, f), BF16),
        scratch_shapes=[pltpu.VMEM((d, tn), BF16)] * 2,
        compiler_params=_params("parallel", "arbitrary"),
        name="ffn_up",
    )(xn, _arr(w1), _arr(w3))


def _proj_kernel(x_ref, ss_ref, g_ref, w_ref, o_ref, wb):
    _cast_weight(w_ref, wb)
    o_ref[...] = jnp.dot(_normed(x_ref, ss_ref, g_ref), wb[...], preferred_element_type=F32)


def _proj(x, ss, g, w, ncols, tn, tm=512):
    m, d = x.shape
    return pl.pallas_call(
        _proj_kernel,
        grid=(ncols // tn, m // tm),
        in_specs=[_row_spec(tm, d), _ss_spec(ss.shape[0], tm), _gain_spec(d),
                  _col_spec(w, d, tn, pipeline_mode=pl.Buffered(1))],
        out_specs=_tile_spec(tm, tn),
        out_shape=jax.ShapeDtypeStruct((m, ncols), F32),
        scratch_shapes=[pltpu.VMEM((d, tn), BF16)],
        compiler_params=_params("parallel", "arbitrary"),
        name="proj",
    )(x, ss, g.reshape(1, d), _arr(w))


def _tile_ss(x_new, ss_ref):
    ss_ref[0] = jnp.broadcast_to(jnp.sum(x_new * x_new, axis=1, keepdims=True), ss_ref.shape[1:])


def _ss_out(n, m, tm, tn):
    return (pl.BlockSpec((1, tm, LANES), lambda j, i: (j, i, 0)),
            jax.ShapeDtypeStruct((n // tn, m, LANES), F32))


def _emit_norm(x_new, g_ref, o_ref):
    ms = jnp.mean(x_new * x_new, axis=-1, keepdims=True)
    o_ref[...] = (x_new * lax.rsqrt(ms + NORM_EPS) * g_ref[...]).astype(o_ref.dtype)


def _ple_kernel(x_ref, ss_ref, g_ref, w_ref, p_ref, wp_ref, gn_ref, *refs, final):
    wb, wpb = refs[-2:]
    _cast_weight(w_ref, wb)
    _cast_weight(wp_ref, wpb)
    gate = jax.nn.sigmoid(jnp.dot(_normed(x_ref, ss_ref, g_ref), wb[...], preferred_element_type=F32))
    emb = jnp.dot(p_ref[...].astype(BF16), wpb[...], preferred_element_type=F32)
    x_new = x_ref[...] + emb * gate
    if not final:
        refs[0][...] = x_new
    _emit_norm(x_new, gn_ref, refs[-3])


def _ple(x, ss, g, w_gate, p, w_proj, next_gain, final, tm=512):
    m, d = x.shape
    pd = p[0].shape[-1]
    once = dict(pipeline_mode=pl.Buffered(1))
    rows = _tile_spec(tm, d)
    normed = jax.ShapeDtypeStruct((m, d), F32 if final else BF16)
    return pl.pallas_call(
        functools.partial(_ple_kernel, final=final),
        grid=(1, m // tm),
        in_specs=[_row_spec(tm, d), _ss_spec(ss.shape[0], tm), _gain_spec(d), _col_spec(w_gate, d, d, **once),
                  pl.BlockSpec((None, tm, pd), lambda j, i: (p[1], i, 0)), _col_spec(w_proj, pd, d, **once),
                  _gain_spec(d)],
        out_specs=rows if final else [rows, rows],
        out_shape=normed if final else [jax.ShapeDtypeStruct((m, d), F32), normed],
        scratch_shapes=[pltpu.VMEM((d, d), BF16), pltpu.VMEM((pd, d), BF16)],
        compiler_params=_params("parallel", "arbitrary"),
        name="ple",
    )(x, ss, g.reshape(1, d), _arr(w_gate), p[0], _arr(w_proj), next_gain.reshape(1, d))


def _mm_res_kernel(*refs, n_pairs, scale, with_norm):
    res_ref = refs[2 * n_pairs]
    outs = refs[2 * n_pairs + 1 + with_norm:]
    scratch = outs[2:]
    acc = None
    for t in range(n_pairs):
        _cast_weight(refs[2 * t + 1], scratch[t])
        part = jnp.dot(refs[2 * t][...], scratch[t][...], preferred_element_type=F32)
        acc = part if acc is None else acc + part
    x_new = res_ref[...] + scale * acc
    outs[0][...] = x_new
    if with_norm:
        _emit_norm(x_new, refs[2 * n_pairs + 1], outs[1])
    else:
        _tile_ss(x_new, outs[1])


def _mm_res(pairs, res, scale, tm=512, tn=512, next_gain=None):
    m, n = res.shape
    with_norm = next_gain is not None
    assert tn == n or not with_norm
    in_specs, args, scratch = [], [], []
    once = dict(pipeline_mode=pl.Buffered(1)) if tn == n else {}
    for a, w in pairs:
        k = a.shape[1]
        in_specs += [_row_spec(tm, k), _col_spec(w, k, tn, **once)]
        args += [a, _arr(w)]
        scratch.append(pltpu.VMEM((k, tn), BF16))
    in_specs.append(_tile_spec(tm, tn))
    args.append(res)
    if with_norm:
        in_specs.append(_gain_spec(n))
        args.append(next_gain.reshape(1, n))
        second = (_tile_spec(tm, tn), jax.ShapeDtypeStruct((m, n), BF16))
    else:
        second = _ss_out(n, m, tm, tn)
    return pl.pallas_call(
        functools.partial(_mm_res_kernel, n_pairs=len(pairs), scale=scale, with_norm=with_norm),
        grid=(n // tn, m // tm),
        in_specs=in_specs,
        out_specs=[_tile_spec(tm, tn), second[0]],
        out_shape=[jax.ShapeDtypeStruct((m, n), F32), second[1]],
        scratch_shapes=scratch,
        compiler_params=_params("parallel", "arbitrary"),
        name="mm_res",
    )(*args)


def _lane_lo(shape=(1, LANES)):
    return lax.broadcasted_iota(jnp.int32, shape, len(shape) - 1) < HEAD_DIM


def _rope_tables(n):
    half = HEAD_DIM // 2
    inv_freq = ROPE_THETA ** (-jnp.arange(half, dtype=F32) / half)
    ang = jnp.arange(n, dtype=F32)[:, None] * inv_freq[None, :]
    cos, sin = jnp.cos(ang), jnp.sin(ang)
    return (jnp.concatenate([cos, cos, cos, cos], axis=-1),
            jnp.concatenate([-sin, sin, -sin, sin], axis=-1))


def _rope_tile(x, cos, sin):
    lane = lax.broadcasted_iota(jnp.int32, (1, LANES), 1)
    first_half = (lane % HEAD_DIM) < (HEAD_DIM // 2)
    partner = jnp.where(first_half, pltpu.roll(x, LANES - HEAD_DIM // 2, 1),
                        pltpu.roll(x, HEAD_DIM // 2, 1))
    return x * cos + partner * sin


def _attn_a_kernel(q_ref, k_ref, v_ref, cos_ref, sin_ref, o_ref, qs, ks, m_s, l_s, acc_s):
    n = q_ref.shape[1]
    lo = _lane_lo()
    rb = 256

    def rope_body(i, carry):
        sl = pl.ds(pl.multiple_of(i * rb, rb), rb)
        c, s = cos_ref[sl, :], sin_ref[sl, :]
        qs[sl, :] = _rope_tile(q_ref[0, sl, :], c, s) * (HEAD_DIM ** -0.5)
        ks[sl, :] = _rope_tile(k_ref[0, sl, :], c, s)
        return carry

    lax.fori_loop(0, n // rb, rope_body, 0)
    m_s[...] = jnp.full(m_s.shape, MASKED, F32)
    l_s[...] = jnp.zeros(l_s.shape, F32)
    acc_s[...] = jnp.zeros(acc_s.shape, F32)

    group = 8

    for dil in A_DILATIONS:
        m = n // dil
        qb_rows = m if m <= 2 * A_QUERY_ROWS else A_QUERY_ROWS
        span = min(qb_rows + 2 * A_HALF, m)
        nqb = m // qb_rows

        def blocks(items, dil=dil, m=m, qb_rows=qb_rows, span=span):
            q0 = [qb * qb_rows for _, qb in items]
            k0 = [jnp.clip(t - A_HALF, 0, m - span) for t in q0]
            if dil == 1:
                rows_q = [pl.ds(pl.multiple_of(t, qb_rows), qb_rows) for t in q0]
                rows_k = [pl.ds(pl.multiple_of(t, A_HALF), span) for t in k0]
            else:
                rows_q = [pl.ds(r + t * dil, qb_rows, stride=dil) for (r, _), t in zip(items, q0)]
                rows_k = [pl.ds(r + t * dil, span, stride=dil) for (r, _), t in zip(items, k0)]
            unstack = lambda t: jnp.where(lo, t[:qb_rows], t[qb_rows:])
            q2 = [jnp.concatenate([jnp.where(lo, t, 0.0), jnp.where(lo, 0.0, t)], axis=0).astype(BF16)
                  for t in (qs[rq, :] for rq in rows_q)]
            kb = [ks[rk, :].astype(BF16) for rk in rows_k]
            vb = [v_ref[0, rk, :].astype(BF16) for rk in rows_k]
            m_old = [m_s[rq, :] for rq in rows_q]
            l_old = [l_s[rq, :] for rq in rows_q]
            acc_old = [acc_s[rq, :] for rq in rows_q]
            qrow = lax.broadcasted_iota(jnp.int32, (2 * qb_rows, 1), 0) % qb_rows
            kcol = lax.broadcasted_iota(jnp.int32, (1, span), 1)
            mask = _each(lambda a, c: jnp.abs(a + qrow - c - kcol) <= A_HALF, q0, k0)
            s = _each(lambda q, kb, mk: jnp.where(mk, _dot_nt(q, kb), MASKED), q2, kb, mask)
            mn = _each(lambda mo, s: jnp.maximum(jnp.concatenate([mo[:, :1], mo[:, LANES - 1:]], axis=0),
                                                 jnp.max(s, axis=1, keepdims=True)), m_old, s)
            p = _each(lambda s, mn: jnp.exp(s - mn), s, mn)
            psum = _each(lambda p: jnp.sum(p, axis=1, keepdims=True), p)
            pv = _each(_dot, p, vb)
            m_new = _each(unstack, mn)
            alpha = _each(lambda mo, mn: jnp.exp(mo - mn), m_old, m_new)
            for rq, mnew, al, lold, ps, ao, pvi in zip(rows_q, m_new, alpha, l_old, psum, acc_old, pv):
                m_s[rq, :] = mnew
                l_s[rq, :] = al * lold + unstack(ps)
                acc_s[rq, :] = al * ao + unstack(pvi)

        per = math.gcd(group, dil * nqb)

        def group_body(it, carry, blocks=blocks, dil=dil, per=per):
            ts = [it * per + u for u in range(per)]
            blocks([(t & (dil - 1), t >> (dil.bit_length() - 1)) for t in ts])
            return carry

        lax.fori_loop(0, dil * nqb // per, group_body, 0)

    def out_body(i, carry):
        sl = pl.ds(pl.multiple_of(i * rb, rb), rb)
        o_ref[0, sl, :] = (acc_s[sl, :] / l_s[sl, :]).astype(o_ref.dtype)
        return carry

    lax.fori_loop(0, n // rb, out_body, 0)


def _attn_a(za, cos, sin):
    b, n, _ = za.shape
    npair = A_W // LANES
    blk = lambda off: pl.BlockSpec((1, n, LANES), lambda i, j, off=off: (i, 0, off + j))
    tab = pl.BlockSpec((n, LANES), lambda i, j: (0, 0))
    return pl.pallas_call(
        _attn_a_kernel,
        grid=(b, npair),
        in_specs=[blk(0), blk(npair), blk(2 * npair), tab, tab],
        out_specs=pl.BlockSpec((1, n, LANES), lambda i, j: (i, 0, j)),
        out_shape=jax.ShapeDtypeStruct((b, n, A_W), BF16),
        scratch_shapes=[pltpu.VMEM((n, LANES), F32)] * 5,
        compiler_params=_params("parallel", "parallel"),
        name="attn_dilated",
    )(za, za, za, cos, sin)


def _attn_c_kernel(sink_ref, q_ref, k_ref, v_ref, cosq_ref, sinq_ref, cos_ref, sin_ref, o_ref,
                   klo, khi, vlo, vhi):
    n = k_ref.shape[1]
    qb_rows = q_ref.shape[1]
    span = qb_rows + 2 * C_HALF
    hk = pl.program_id(1)
    qb = pl.program_id(2)
    lo = _lane_lo()

    @pl.when(qb == 0)
    def _():
        rb = 256

        def body(i, carry):
            sl = pl.ds(pl.multiple_of(i * rb, rb), rb)
            kx = _rope_tile(k_ref[0, sl, :], cos_ref[sl, :], sin_ref[sl, :])
            kw = pltpu.roll(kx, HEAD_DIM, 1)
            vx = v_ref[0, sl, :]
            vw = pltpu.roll(vx, HEAD_DIM, 1)
            first = hk == 0
            klo[sl, :] = jnp.where(lo, jnp.where(first, kx, kw), 0.0).astype(BF16)
            khi[sl, :] = jnp.where(lo, 0.0, jnp.where(first, kw, kx)).astype(BF16)
            vlo[sl, :] = jnp.where(lo, jnp.where(first, vx, vw), 0.0).astype(BF16)
            vhi[sl, :] = jnp.where(lo, 0.0, jnp.where(first, vw, vx)).astype(BF16)
            return carry

        lax.fori_loop(0, n // rb, body, 0)

    q0 = qb * qb_rows
    k0 = pl.multiple_of(jnp.clip(q0 - C_HALF, 0, n - span), C_HALF)
    rows_k = pl.ds(k0, span)
    kl, kh = klo[rows_k, :], khi[rows_k, :]
    vl, vh = vlo[rows_k, :], vhi[rows_k, :]
    qpos = q0 + lax.broadcasted_iota(jnp.int32, (qb_rows, 1), 0)
    kpos = k0 + lax.broadcasted_iota(jnp.int32, (1, span), 1)
    mask = jnp.abs(qpos - kpos) <= C_HALF
    cq, sq = cosq_ref[...], sinq_ref[...]

    ntile = q_ref.shape[2] // LANES
    qt = [_rope_tile(q_ref[0, :, j * LANES:(j + 1) * LANES], cq, sq) * (HEAD_DIM ** -0.5) for j in range(ntile)]
    qt = [t for t in qt for _ in range(2)]
    kk = [kl, kh] * ntile
    sk = [sink_ref[hk * C_GROUP + h] for h in range(2 * ntile)]
    s = _each(lambda q, k: jnp.where(mask, _dot_nt(q, k), MASKED), qt, kk)
    mx = _each(lambda s, sk: jnp.maximum(jnp.max(s, axis=1, keepdims=True), sk), s, sk)
    p = _each(lambda s, mx: jnp.exp(s - mx), s, mx)
    den = _each(lambda p, sk, mx: jnp.sum(p, axis=1, keepdims=True) + jnp.exp(sk - mx), p, sk, mx)
    for j in range(ntile):
        pv = _dot(p[2 * j], vl) + _dot(p[2 * j + 1], vh)
        o_ref[0, :, j * LANES:(j + 1) * LANES] = (pv / jnp.where(lo, den[2 * j], den[2 * j + 1])).astype(o_ref.dtype)


def _attn_c(zc, sink, cos, sin, qb_rows=128):
    b, n, _ = zc.shape
    gw = C_GROUP * HEAD_DIM
    kcol = C_W // LANES
    full = lambda off: pl.BlockSpec((1, n, LANES), lambda i, h, t, off=off: (i, 0, off))
    tabq = pl.BlockSpec((qb_rows, LANES), lambda i, h, t: (t, 0))
    tab = pl.BlockSpec((n, LANES), lambda i, h, t: (0, 0))
    return pl.pallas_call(
        _attn_c_kernel,
        grid=(b, C_KV_HEADS, n // qb_rows),
        in_specs=[pl.BlockSpec(memory_space=pltpu.SMEM),
                  pl.BlockSpec((1, qb_rows, gw), lambda i, h, t: (i, t, h)),
                  full(kcol), full(kcol + 1), tabq, tabq, tab, tab],
        out_specs=pl.BlockSpec((1, qb_rows, gw), lambda i, h, t: (i, t, h)),
        out_shape=jax.ShapeDtypeStruct((b, n, C_W), BF16),
        scratch_shapes=[pltpu.VMEM((n, LANES), BF16)] * 4,
        compiler_params=_params("parallel", "parallel", "arbitrary"),
        name="attn_gqa_sink",
    )(sink.reshape(-1), zc, zc, zc, cos, sin, cos, sin)


def _na_bias_table(rpb, rows):
    kh = min(NA_KH, rows)
    qc = jnp.arange(GRID_W)[:, None]
    kc = jnp.arange(GRID_W)[None, :]
    qc0 = jnp.clip(qc - NA_KW // 2, 0, GRID_W - NA_KW)
    valid = (kc >= qc0) & (kc < qc0 + NA_KW)
    ci = jnp.clip(kc - qc, -(NA_KW - 1), NA_KW - 1) + NA_KW - 1
    nci = 2 * NA_KW - 1
    pairs = rpb.astype(F32).reshape(D_HEADS // 2, 2, 2 * NA_KH - 1, nci)
    pairs = jnp.concatenate([pairs, jnp.full(pairs.shape[:-1] + (1,), MASKED, F32)], axis=-1)
    by_row = jnp.stack([pairs[:, :, off:off + kh] for off in range(NA_KH)], axis=1)
    cls = jnp.where(valid, ci, nci)
    onehot = (cls[None] == jnp.arange(nci + 1)[:, None, None]).astype(F32)
    bias = jnp.einsum('poekc,cqj->poeqkj', by_row, onehot, precision=lax.Precision.HIGHEST)
    return bias.reshape(D_HEADS // 2, NA_KH, 2 * GRID_W, kh * GRID_W)


def _attn_d_kernel(q_ref, k_ref, v_ref, bias_ref, o_ref):
    n = q_ref.shape[1]
    rows = n // GRID_W
    kh = min(NA_KH, rows)
    lo = _lane_lo()
    group = 4 if rows % 4 == 0 else 1

    def body(rg, carry):
        r = [rg * group + u for u in range(group)]
        r0 = [jnp.clip(t - NA_KH // 2, 0, rows - kh) for t in r]
        rows_q = [pl.ds(pl.multiple_of(t * GRID_W, GRID_W), GRID_W) for t in r]
        rows_k = [pl.ds(pl.multiple_of(t * GRID_W, GRID_W), kh * GRID_W) for t in r0]
        q = [q_ref[0, rq, :] * (HEAD_DIM ** -0.5) for rq in rows_q]
        q2 = _each(lambda q: jnp.concatenate([jnp.where(lo, q, 0.0), jnp.where(lo, 0.0, q)], axis=0), q)
        s = [_dot_nt(q2i, k_ref[0, rk, :]) + bias_ref[0, t0 - t + NA_KH - 1]
             for q2i, rk, t0, t in zip(q2, rows_k, r0, r)]
        p = _each(lambda s: jnp.exp(s - jnp.max(s, axis=1, keepdims=True)), s)
        pv = [_dot(pi, v_ref[0, rk, :]) / jnp.sum(pi, axis=1, keepdims=True) for pi, rk in zip(p, rows_k)]
        for rq, pvi in zip(rows_q, pv):
            o_ref[0, rq, :] = jnp.where(lo, pvi[:GRID_W], pvi[GRID_W:]).astype(o_ref.dtype)
        return carry

    lax.fori_loop(0, rows // group, body, 0)


def _attn_d(zc, bias):
    b, n, _ = zc.shape
    npair = D_W // LANES
    q_off = (C_W + 2 * C_KV_HEADS * HEAD_DIM) // LANES
    blk = lambda off: pl.BlockSpec((1, n, LANES), lambda i, j, off=off: (i, 0, off + j))
    return pl.pallas_call(
        _attn_d_kernel,
        grid=(b, npair),
        in_specs=[blk(q_off), blk(q_off + npair), blk(q_off + 2 * npair),
                  pl.BlockSpec((1,) + bias.shape[1:], lambda i, j: (j, 0, 0, 0))],
        out_specs=pl.BlockSpec((1, n, LANES), lambda i, j: (i, 0, j)),
        out_shape=jax.ShapeDtypeStruct((b, n, D_W), BF16),
        compiler_params=_params("parallel", "parallel"),
        name="attn_neighborhood",
    )(zc, zc, zc, bias)


def _rw_prep_kernel(z_ref, hp_ref, hn_ref, mup_ref, mun_ref, w0_ref, w2f_ref, w2b_ref, a0_ref,
                    a2f_ref, a2b_ref, g2_ref, r_o, k_o, v_o, cumf_o, cumb_o, af_o, ab_o, g_o):
    tm = z_ref.shape[1]
    first = pl.program_id(1) == 0
    last = pl.program_id(1) == pl.num_programs(1) - 1
    row = lax.broadcasted_iota(jnp.int32, (tm, 1), 0)
    crow = row % WKV_CHUNK

    def edges(cols):
        return (jnp.where(first, 0.0, hp_ref[0, SUBLANES - 1:SUBLANES, cols]),
                jnp.where(last, 0.0, hn_ref[0, 0:1, cols]))

    def mix(z, z_prev, z_next, cols):
        return z + mup_ref[:, cols] * (z_prev - z) + mun_ref[:, cols] * (z_next - z)

    def shifted(cols):
        z = z_ref[0, :, cols]
        edge_prev, edge_next = edges(cols)
        z_prev = jnp.where(row == 0, edge_prev, pltpu.roll(z, 1, 0))
        z_next = jnp.where(row == tm - 1, edge_next, pltpu.roll(z, tm - 1, 0))
        return mix(z, z_prev, z_next, cols)

    def store_shifted(o_ref, cs, cols):
        z = z_ref[0, :, cols]
        edge_prev, edge_next = edges(cols)
        o_ref[0, :, cs] = mix(z, pltpu.roll(z, 1, 0), pltpu.roll(z, tm - 1, 0), cols)
        o_ref[0, 0:1, cs] = mix(z[0:1], edge_prev, z[1:2], cols)
        o_ref[0, tm - 1:tm, cs] = mix(z[tm - 1:tm], z[tm - 2:tm - 1], edge_next, cols)

    def log_decay(x):
        return -RW_DECAY_SCALE * jax.nn.sigmoid(x)

    def chunk_cumsum(x, rev):
        s = 1
        while s < WKV_CHUNK:
            if rev:
                x = x + jnp.where(crow < WKV_CHUNK - s, pltpu.roll(x, tm - s, 0), 0.0)
            else:
                x = x + jnp.where(crow >= s, pltpu.roll(x, s, 0), 0.0)
            s *= 2
        return x

    base = 3 * B_W
    wl = jnp.tanh(shifted(slice(base, base + LANES))).astype(BF16)
    al = shifted(slice(base + LANES, base + 2 * LANES)).astype(BF16)
    gl = jax.nn.sigmoid(shifted(slice(base + 2 * LANES, base + 4 * LANES))).astype(BF16)

    for s0 in range(0, B_W, LANES):
        cs = slice(s0, s0 + LANES)
        store_shifted(r_o, cs, cs)
        store_shifted(k_o, cs, slice(B_W + s0, B_W + s0 + LANES))
        store_shifted(v_o, cs, slice(2 * B_W + s0, 2 * B_W + s0 + LANES))
        cumf_o[0, :, cs] = chunk_cumsum(log_decay(w0_ref[0:1, cs] + _dot(wl, w2f_ref[:, cs])), False)
        cumb_o[0, :, cs] = chunk_cumsum(log_decay(w0_ref[1:2, cs] + _dot(wl, w2b_ref[:, cs])), True)
        af_o[0, :, cs] = jax.nn.sigmoid(a0_ref[0:1, cs] + _dot(al, a2f_ref[:, cs]))
        ab_o[0, :, cs] = jax.nn.sigmoid(a0_ref[1:2, cs] + _dot(al, a2b_ref[:, cs]))
        g_o[0, :, cs] = _dot(gl, g2_ref[:, cs])


def _rw_prep(zb, mu_prev, mu_next, w0, w2, a0, a2, g2, tm=256):
    b, n, cols = zb.shape
    tm = min(tm, n)
    per = tm // SUBLANES
    pad_cols = lambda t: jnp.pad(t, (0, cols - t.shape[0])).reshape(1, cols)
    rows_f = lambda t: jnp.pad(t, ((0, LANES - RW_LORA), (0, 0))).astype(BF16)
    rows_b = lambda t: jnp.pad(t, ((RW_LORA, LANES - 2 * RW_LORA), (0, 0))).astype(BF16)
    g2p = jnp.pad(g2, ((0, 2 * LANES - RW_GATE_LORA), (0, 0))).astype(BF16)
    const = lambda shape: pl.BlockSpec(shape, lambda i, t: (0,) * len(shape))
    halo_prev = pl.BlockSpec((1, SUBLANES, cols), lambda i, t: (i, jnp.maximum(t * per - 1, 0), 0))
    halo_next = pl.BlockSpec((1, SUBLANES, cols), lambda i, t: (i, jnp.minimum((t + 1) * per, n // SUBLANES - 1), 0))
    tok = jax.ShapeDtypeStruct((b, n, B_W), F32)
    return pl.pallas_call(
        _rw_prep_kernel,
        grid=(b, n // tm),
        in_specs=[pl.BlockSpec((1, tm, cols), lambda i, t: (i, t, 0)), halo_prev, halo_next,
                  const((1, cols)), const((1, cols)), const((2, B_W)), const((LANES, B_W)),
                  const((LANES, B_W)), const((2, B_W)), const((LANES, B_W)), const((LANES, B_W)),
                  const((2 * LANES, B_W))],
        out_specs=[pl.BlockSpec((1, tm, B_W), lambda i, t: (i, t, 0))] * 8,
        out_shape=[tok] * 8,
        compiler_params=_params("parallel", "parallel"),
        name="rwkv_prep",
    )(zb, zb, zb, pad_cols(mu_prev), pad_cols(mu_next), w0, rows_f(w2[0]), rows_b(w2[1]),
      a0, rows_f(a2[0]), rows_b(a2[1]), g2p)


def _head_sum(x, lo):
    return jnp.where(lo, jnp.sum(jnp.where(lo, x, 0.0), axis=1, keepdims=True),
                     jnp.sum(jnp.where(lo, 0.0, x), axis=1, keepdims=True))


def _wkv_chunks(chains):
    c = chains[0][0].shape[0]
    ii = lax.broadcasted_iota(jnp.int32, (c, c), 0)
    jj = lax.broadcasted_iota(jnp.int32, (c, c), 1)
    row = lax.broadcasted_iota(jnp.int32, (c, 1), 0)
    eye_f = (ii == jj).astype(F32)
    masks = {False: (jj <= ii, jj < ii), True: (jj >= ii, jj > ii)}
    lo = _lane_lo()
    ki = lax.broadcasted_iota(jnp.int32, (LANES, LANES), 0)
    vj = lax.broadcasted_iota(jnp.int32, (LANES, LANES), 1)
    same_head = (ki < HEAD_DIM) == (vj < HEAD_DIM)
    diag = ki == vj
    r, k, v, cum, rate, st, k_k, k_a, rev = (list(col) for col in zip(*chains))
    twice = lambda xs: [x for x in xs for _ in range(2)]
    merge = lambda xs: [jnp.where(lo, xs[2 * i], xs[2 * i + 1]) for i in range(len(xs) // 2)]
    incl = twice([masks[x][0] for x in rev])
    strict = twice([masks[x][1] for x in rev])

    kk = _each(lambda k, k_k: k * k_k, k, k_k)
    kk = _each(lambda t: t / jnp.maximum(jnp.sqrt(_head_sum(t * t, lo)), 1e-12), kk)
    kd = _each(lambda k, rate, k_a: k * (1.0 + (rate - 1.0) * k_a), k, rate, k_a)
    b = _each(lambda kk, rate: kk * rate, kk, rate)

    tot = _each(lambda cum, x: cum[0:1] if x else cum[c - 1:c], cum, rev)
    excl = _each(lambda cum, x: jnp.where(row == c - 1, 0.0, pltpu.roll(cum, c - 1, 0)) if x
                 else jnp.where(row == 0, 0.0, pltpu.roll(cum, 1, 0)), cum, rev)
    ar = _each(lambda kk, r, cum, ex: jnp.concatenate([-kk * jnp.exp(ex), r * jnp.exp(cum)], axis=0),
               kk, r, cum, excl)
    ar_h = [jnp.where(m, x, 0.0).astype(BF16) for x in ar for m in (lo, ~lo)]
    ar = _each(lambda t: t.astype(BF16), ar)
    v = _each(lambda t: t.astype(BF16), v)
    g_inv = _each(lambda cum: jnp.exp(-cum), cum)
    bg = twice(_each(lambda b, g: (b * g).astype(BF16), b, g_inv))
    kg = twice(_each(lambda kd, g: (kd * g).astype(BF16), kd, g_inv))
    x1 = _each(_dot_nt, ar_h, bg)
    x2 = _each(_dot_nt, ar_h, kg)
    a_ab = _each(lambda x, m: jnp.where(m, x[:c], 0.0), x1, strict)
    a_rb = _each(lambda x, m: jnp.where(m, x[c:], 0.0).astype(BF16), x1, incl)
    a_k = _each(lambda x, ms, mi: jnp.concatenate([jnp.where(ms, x[:c], 0.0), jnp.where(mi, x[c:], 0.0)],
                                                  axis=0).astype(BF16), x2, strict, incl)

    inv = _each(lambda t: eye_f + t, a_ab)
    pw = _each(lambda t: _dot(t, t), a_ab)
    for _ in range(c.bit_length() - 3):
        pw = _each(lambda t: t.astype(BF16), pw)
        both = _each(lambda inv, pw: _dot(jnp.concatenate([inv.astype(BF16), pw], axis=0), pw), inv, pw)
        inv = _each(lambda inv, t: inv + t[:c], inv, both)
        pw = _each(lambda t: t[c:], both)
    inv = _each(lambda inv, pw: (inv + _dot(inv, pw)).astype(BF16), inv, pw)

    akv = merge(_each(_dot, a_k, twice(v)))
    ars = _each(_dot, ar, st)
    rhs = _each(lambda ars, akv: (ars[:c] + akv[:c]).astype(BF16), ars, akv)
    p = _each(lambda t: t.astype(BF16), merge(_each(_dot, inv, twice(rhs))))
    y = _each(lambda ars, arb, akv: ars[c:] + arb + akv[c:], ars, merge(_each(_dot, a_rb, twice(p))), akv)
    g_end = _each(lambda tot, cum: jnp.exp(tot - cum), tot, cum)
    upd = _each(lambda b, kd, g, p, v: _dot_tn(jnp.concatenate([b * g, kd * g], axis=0),
                                               jnp.concatenate([p, v], axis=0)), b, kd, g_end, p, v)
    g_col = _each(lambda tot: jnp.sum(jnp.where(diag, jnp.exp(tot), 0.0), axis=1, keepdims=True), tot)
    new_st = _each(lambda g, st, upd: jnp.where(same_head, g * st + upd, 0.0), g_col, st, upd)
    return list(zip(y, new_st))


def _rw_scan_kernel(rf, kf, vf, cumf, af, rb, kb, vb, cumb, ab, kk_ref, ka_ref, yf, yb, sf, sb):
    tb = rf.shape[1]
    npair = rf.shape[2] // LANES
    c = WKV_CHUNK
    nch = tb // c
    pair = lambda j: slice(j * LANES, (j + 1) * LANES)

    @pl.when(pl.program_id(2) == 0)
    def _():
        sf[...] = jnp.zeros(sf.shape, F32)
        sb[...] = jnp.zeros(sb.shape, F32)

    def body(ci, carry):
        rows_f = pl.ds(pl.multiple_of(ci * c, c), c)
        rows_b = pl.ds(pl.multiple_of((nch - 1 - ci) * c, c), c)
        chains = []
        for j in range(npair):
            par = (kk_ref[:, pair(j)], ka_ref[:, pair(j)])
            chains.append(tuple(ref[0, rows_f, pair(j)] for ref in (rf, kf, vf, cumf, af)) + (sf[j],) + par + (False,))
            chains.append(tuple(ref[0, rows_b, pair(j)] for ref in (rb, kb, vb, cumb, ab)) + (sb[j],) + par + (True,))
        outs = _wkv_chunks(chains)
        for j in range(npair):
            yf[0, rows_f, pair(j)], sf[j] = outs[2 * j]
            yb[0, rows_b, pair(j)], sb[j] = outs[2 * j + 1]
        return carry

    lax.fori_loop(0, nch, body, 0)


def _rw_scan(r, k, v, cumf, cumb, af, ab, k_k, k_a, hb=20, tb=256):
    b, n, width = r.shape
    tb = min(tb, n)
    nt = n // tb
    hw = hb * HEAD_DIM
    fwd = pl.BlockSpec((1, tb, hw), lambda i, g, t: (i, t, g))
    bwd = pl.BlockSpec((1, tb, hw), lambda i, g, t: (i, nt - 1 - t, g))
    par = pl.BlockSpec((1, hw), lambda i, g, t: (0, g))
    out = jax.ShapeDtypeStruct((b, n, width), F32)
    return pl.pallas_call(
        _rw_scan_kernel,
        grid=(b, width // hw, nt),
        in_specs=[fwd] * 5 + [bwd] * 5 + [par, par],
        out_specs=[fwd, bwd],
        out_shape=[out, out],
        scratch_shapes=[pltpu.VMEM((hw // LANES, LANES, LANES), F32)] * 2,
        compiler_params=_params("parallel", "parallel", "arbitrary"),
        name="rwkv_scan",
    )(r, k, v, cumf, af, r, k, v, cumb, ab, k_k.reshape(1, width), k_a.reshape(1, width))


def _rw_post_kernel(yf, yb, r, k, v, af, ab, g, ka_ref, rk_ref, lnw_ref, lnb_ref, o_ref):
    lo = _lane_lo()
    head_sum = lambda x: _head_sum(x, lo)
    for j in range(o_ref.shape[2] // LANES):
        sl = slice(j * LANES, (j + 1) * LANES)
        y = yf[0, :, sl] + yb[0, :, sl]
        dev = y - head_sum(y) * (1.0 / HEAD_DIM)
        var = head_sum(dev * dev) * (1.0 / HEAD_DIM)
        yn = dev * lax.rsqrt(var + RW_GN_EPS) * lnw_ref[:, sl] + lnb_ref[:, sl]
        kt, k_a = k[0, :, sl], ka_ref[:, sl]
        kd = kt * (1.0 + (af[0, :, sl] - 1.0) * k_a) + kt * (1.0 + (ab[0, :, sl] - 1.0) * k_a)
        bonus = head_sum(r[0, :, sl] * kd * rk_ref[:, sl]) * v[0, :, sl]
        o_ref[0, :, sl] = ((yn + bonus) * g[0, :, sl]).astype(o_ref.dtype)


def _rw_post(yf, yb, r, k, v, af, ab, g, k_a, r_k, ln_w, ln_b, tm=256):
    b, n, width = r.shape
    tm = min(tm, n)
    tok = pl.BlockSpec((1, tm, width), lambda i, t: (i, t, 0))
    par = pl.BlockSpec((1, width), lambda i, t: (0, 0))
    return pl.pallas_call(
        _rw_post_kernel,
        grid=(b, n // tm),
        in_specs=[tok] * 8 + [par] * 4,
        out_specs=tok,
        out_shape=jax.ShapeDtypeStruct((b, n, width), BF16),
        compiler_params=_params("parallel", "parallel"),
        name="rwkv_post",
    )(yf, yb, r, k, v, af, ab, g, k_a.reshape(1, width), r_k.reshape(1, width),
      ln_w.reshape(1, width), ln_b.reshape(1, width))


def _rwkv7_bidir(zb, mu_prev, mu_next, w0, w2, a0, a2, g2, k_k, k_a, r_k, ln_w, ln_b):
    r, k, v, cumf, cumb, af, ab, g = _rw_prep(zb, mu_prev, mu_next, w0, w2, a0, a2, g2)
    yf, yb = _rw_scan(r, k, v, cumf, cumb, af, ab, k_k, k_a)
    return _rw_post(yf, yb, r, k, v, af, ab, g, k_a, r_k, ln_w, ln_b)


def _ffn_half_step(x, xn, w1, w3, w2, layer):
    h = _ffn_up(xn, (w1, layer), (w3, layer))
    return _mm_res([(h, (w2, layer))], x, 0.5)


def _mix_ab(x, ss, g, next_gain, b, n, cos, sin, w_in, layer, w_out, mu_prev, mu_next, w0, w2, a0, a2, g2,
            k_k, k_a, r_k, ln_w, ln_b):
    qkv = 3 * A_W
    w_b = jnp.pad(w_in[layer, :, qkv:], ((0, 0), (0, RW_PAD_COLS - (w_in.shape[2] - qkv))))
    za = _proj(x, ss, g, w_in[layer, :, :qkv], qkv, tn=qkv).reshape(b, n, qkv)
    zb = _proj(x, ss, g, w_b, RW_PAD_COLS, tn=RW_PAD_COLS // 2).reshape(b, n, RW_PAD_COLS)
    oa = _attn_a(za, cos, sin).reshape(b * n, A_W)
    ob = _rwkv7_bidir(zb, mu_prev, mu_next, w0, w2, a0, a2, g2, k_k, k_a, r_k, ln_w, ln_b)
    return _mm_res([(oa, w_out[:A_W]), (ob.reshape(b * n, B_W), w_out[A_W:])], x, 1.0, tn=x.shape[1],
                   next_gain=next_gain)


def _mix_cd(x, ss, g, next_gain, b, n, cos, sin, w_in, layer, w_out, sink, rpb):
    cols = w_in.shape[2]
    zc = _proj(x, ss, g, (w_in, layer), cols, tn=cols // 2).reshape(b, n, cols)
    oc = _attn_c(zc, sink, cos, sin).reshape(b * n, C_W)
    od = _attn_d(zc, _na_bias_table(rpb, n // GRID_W)).reshape(b * n, D_W)
    return _mm_res([(oc, w_out[:C_W]), (od, w_out[C_W:])], x, 1.0, tn=x.shape[1], next_gain=next_gain)


def kernel(x, p, ffn1_norm, ffn1_w1, ffn1_w3, ffn1_w2, mix_norm, ffn2_norm, ffn2_w1, ffn2_w3, ffn2_w2, ple_norm, ple_w_gate, ple_w_proj, ab_w_in, ab_w_out, rw_mu_prev, rw_mu_next, rw_w0, rw_w2, rw_a0, rw_a2, rw_g2, rw_k_k, rw_k_a, rw_r_k, rw_ln_w, rw_ln_b, cd_w_in, cd_w_out, c_sink, d_rpb, final_norm):
    b, n, d = x.shape
    depth = p.shape[0]
    cos, sin = _rope_tables(n)
    x = x.reshape(b * n, d)
    p = p.reshape(depth, b * n, -1)
    xn = _rmsnorm(x, ffn1_norm[0], BF16)
    for i in range(depth):
        j = i // 2
        x, ss = _ffn_half_step(x, xn, ffn1_w1, ffn1_w3, ffn1_w2, i)
        if i % 2 == 0:
            x, xn = _mix_ab(x, ss, mix_norm[i], ffn2_norm[i], b, n, cos, sin, ab_w_in, j, ab_w_out[j],
                            rw_mu_prev[j], rw_mu_next[j], rw_w0[j], rw_w2[j], rw_a0[j], rw_a2[j], rw_g2[j],
                            rw_k_k[j], rw_k_a[j], rw_r_k[j], rw_ln_w[j], rw_ln_b[j])
        else:
            x, xn = _mix_cd(x, ss, mix_norm[i], ffn2_norm[i], b, n, cos, sin, cd_w_in, j, cd_w_out[j],
                            c_sink[j], d_rpb[j])
        x, ss = _ffn_half_step(x, xn, ffn2_w1, ffn2_w3, ffn2_w2, i)
        ple_args = (x, ss, ple_norm[i], (ple_w_gate, i), (p, i), (ple_w_proj, i))
        if i + 1 < depth:
            x, xn = _ple(*ple_args, ffn1_norm[i + 1], False)
    return _ple(*ple_args, final_norm, True).reshape(b, n, d)
```

```python
import functools
import math

import jax
import jax.numpy as jnp
from jax import lax
from jax.experimental import pallas as pl
from jax.experimental.pallas import tpu as pltpu

F32 = jnp.float32
BF16 = jnp.bfloat16

HEAD_DIM = 64
LANES = 128
SUBLANES = 8
NORM_EPS = 1e-6
ROPE_THETA = 10000.0
GRID_W = 64
A_HEADS = 12
A_W = A_HEADS * HEAD_DIM
A_HALF = 64
A_DILATIONS = (1, 4, 16)
A_QUERY_ROWS = 64
B_HEADS = 20
B_W = B_HEADS * HEAD_DIM
RW_LORA = 64
RW_GATE_LORA = 192
RW_GN_EPS = 64e-5
RW_DECAY_SCALE = math.exp(-0.5)
RW_PAD_COLS = 3 * B_W + 4 * LANES
C_HEADS = 16
C_KV_HEADS = 2
C_GROUP = C_HEADS // C_KV_HEADS
C_W = C_HEADS * HEAD_DIM
C_HALF = 128
D_HEADS = 16
D_W = D_HEADS * HEAD_DIM
NA_KH = 8
NA_KW = 16
MASKED = -1e30
WKV_CHUNK = 64
VMEM_LIMIT = 56 * 1024 * 1024


def _params(*sem):
    return pltpu.CompilerParams(dimension_semantics=sem, vmem_limit_bytes=VMEM_LIMIT)


def _dot(a, b):
    return jnp.dot(a.astype(BF16), b.astype(BF16), preferred_element_type=F32)


def _dot_nt(a, b):
    return lax.dot_general(a.astype(BF16), b.astype(BF16), (((1,), (1,)), ((), ())),
                           preferred_element_type=F32)


def _dot_tn(a, b):
    return lax.dot_general(a.astype(BF16), b.astype(BF16), (((0,), (0,)), ((), ())),
                           preferred_element_type=F32)


def _each(f, *cols):
    return [f(*xs) for xs in zip(*cols)]


def _rmsnorm_kernel(x_ref, g_ref, o_ref):
    x = x_ref[...]
    ms = jnp.mean(x * x, axis=-1, keepdims=True)
    o_ref[...] = (x * lax.rsqrt(ms + NORM_EPS) * g_ref[...]).astype(o_ref.dtype)


def _rmsnorm(x, g, dtype, tm=512):
    m, d = x.shape
    return pl.pallas_call(
        _rmsnorm_kernel,
        grid=(m // tm,),
        in_specs=[pl.BlockSpec((tm, d), lambda i: (i, 0)),
                  pl.BlockSpec((1, d), lambda i: (0, 0))],
        out_specs=pl.BlockSpec((tm, d), lambda i: (i, 0)),
        out_shape=jax.ShapeDtypeStruct((m, d), dtype),
        compiler_params=_params("parallel"),
        name="rmsnorm",
    )(x, g.reshape(1, d))


def _cast_weight(w_ref, wb_ref):
    rows = 256

    @pl.when(pl.program_id(1) == 0)
    def _():
        def body(i, carry):
            sl = pl.ds(pl.multiple_of(i * rows, rows), rows)
            wb_ref[sl, :] = w_ref[sl, :].astype(BF16)
            return carry

        lax.fori_loop(0, w_ref.shape[0] // rows, body, 0)


def _row_spec(tm, k):
    return pl.BlockSpec((tm, k), lambda j, i: (i, 0))


def _col_spec(w, k, tn, **kw):
    if isinstance(w, tuple):
        layer = w[1]
        return pl.BlockSpec((None, k, tn), lambda j, i: (layer, 0, j), **kw)
    return pl.BlockSpec((k, tn), lambda j, i: (0, j), **kw)


def _arr(w):
    return w[0] if isinstance(w, tuple) else w


def _tile_spec(tm, tn):
    return pl.BlockSpec((tm, tn), lambda j, i: (i, j))


def _normed(x_ref, ss_ref, g_ref):
    ss = ss_ref[0, :, :1]
    for t in range(1, ss_ref.shape[0]):
        ss = ss + ss_ref[t, :, :1]
    rs = lax.rsqrt(ss * (1.0 / x_ref.shape[1]) + NORM_EPS)
    return (x_ref[...] * rs * g_ref[...]).astype(BF16)


def _ss_spec(nt, tm):
    return pl.BlockSpec((nt, tm, LANES), lambda j, i: (0, i, 0))


def _gain_spec(d):
    return pl.BlockSpec((1, d), lambda j, i: (0, 0))


def _ffn_up_kernel(xn_ref, w1_ref, w3_ref, o_ref, w1b, w3b):
    _cast_weight(w1_ref, w1b)
    _cast_weight(w3_ref, w3b)
    xn = xn_ref[...]
    a = jnp.dot(xn, w1b[...], preferred_element_type=F32)
    b = jnp.dot(xn, w3b[...], preferred_element_type=F32)
    o_ref[...] = (a * jax.nn.sigmoid(a) * b).astype(o_ref.dtype)


def _ffn_up(xn, w1, w3, tm=1024, tn=512):
    m, d = xn.shape
    f = _arr(w1).shape[-1]
    return pl.pallas_call(
        _ffn_up_kernel,
        grid=(f // tn, m // tm),
        in_specs=[_row_spec(tm, d), _col_spec(w1, d, tn), _col_spec(w3, d, tn)],
        out_specs=_tile_spec(tm, tn),
        out_shape=jax.ShapeDtypeStruct((m, f), BF16),
        scratch_shapes=[pltpu.VMEM((d, tn), BF16)] * 2,
        compiler_params=_params("parallel", "arbitrary"),
        name="ffn_up",
    )(xn, _arr(w1), _arr(w3))


def _proj_kernel(x_ref, ss_ref, g_ref, w_ref, o_ref, wb):
    _cast_weight(w_ref, wb)
    o_ref[...] = jnp.dot(_normed(x_ref, ss_ref, g_ref), wb[...], preferred_element_type=F32)


def _proj(x, ss, g, w, ncols, tn, tm=512):
    m, d = x.shape
    return pl.pallas_call(
        _proj_kernel,
        grid=(ncols // tn, m // tm),
        in_specs=[_row_spec(tm, d), _ss_spec(ss.shape[0], tm), _gain_spec(d),
                  _col_spec(w, d, tn, pipeline_mode=pl.Buffered(1))],
        out_specs=_tile_spec(tm, tn),
        out_shape=jax.ShapeDtypeStruct((m, ncols), F32),
        scratch_shapes=[pltpu.VMEM((d, tn), BF16)],
        compiler_params=_params("parallel", "arbitrary"),
        name="proj",
    )(x, ss, g.reshape(1, d), _arr(w))


def _tile_ss(x_new, ss_ref):
    ss_ref[0] = jnp.broadcast_to(jnp.sum(x_new * x_new, axis=1, keepdims=True), ss_ref.shape[1:])


def _ss_out(n, m, tm, tn):
    return (pl.BlockSpec((1, tm, LANES), lambda j, i: (j, i, 0)),
            jax.ShapeDtypeStruct((n // tn, m, LANES), F32))


def _emit_norm(x_new, g_ref, o_ref):
    ms = jnp.mean(x_new * x_new, axis=-1, keepdims=True)
    o_ref[...] = (x_new * lax.rsqrt(ms + NORM_EPS) * g_ref[...]).astype(o_ref.dtype)


def _ple_kernel(x_ref, ss_ref, g_ref, w_ref, p_ref, wp_ref, gn_ref, *refs, final):
    wb, wpb = refs[-2:]
    _cast_weight(w_ref, wb)
    _cast_weight(wp_ref, wpb)
    gate = jax.nn.sigmoid(jnp.dot(_normed(x_ref, ss_ref, g_ref), wb[...], preferred_element_type=F32))
    emb = jnp.dot(p_ref[...].astype(BF16), wpb[...], preferred_element_type=F32)
    x_new = x_ref[...] + emb * gate
    if not final:
        refs[0][...] = x_new
    _emit_norm(x_new, gn_ref, refs[-3])


def _ple(x, ss, g, w_gate, p, w_proj, next_gain, final, tm=512):
    m, d = x.shape
    pd = p[0].shape[-1]
    once = dict(pipeline_mode=pl.Buffered(1))
    rows = _tile_spec(tm, d)
    normed = jax.ShapeDtypeStruct((m, d), F32 if final else BF16)
    return pl.pallas_call(
        functools.partial(_ple_kernel, final=final),
        grid=(1, m // tm),
        in_specs=[_row_spec(tm, d), _ss_spec(ss.shape[0], tm), _gain_spec(d), _col_spec(w_gate, d, d, **once),
                  pl.BlockSpec((None, tm, pd), lambda j, i: (p[1], i, 0)), _col_spec(w_proj, pd, d, **once),
                  _gain_spec(d)],
        out_specs=rows if final else [rows, rows],
        out_shape=normed if final else [jax.ShapeDtypeStruct((m, d), F32), normed],
        scratch_shapes=[pltpu.VMEM((d, d), BF16), pltpu.VMEM((pd, d), BF16)],
        compiler_params=_params("parallel", "arbitrary"),
        name="ple",
    )(x, ss, g.reshape(1, d), _arr(w_gate), p[0], _arr(w_proj), next_gain.reshape(1, d))


def _mm_res_kernel(*refs, n_pairs, scale, with_norm):
    res_ref = refs[2 * n_pairs]
    outs = refs[2 * n_pairs + 1 + with_norm:]
    scratch = outs[2:]
    acc = None
    for t in range(n_pairs):
        _cast_weight(refs[2 * t + 1], scratch[t])
        part = jnp.dot(refs[2 * t][...], scratch[t][...], preferred_element_type=F32)
        acc = part if acc is None else acc + part
    x_new = res_ref[...] + scale * acc
    outs[0][...] = x_new
    if with_norm:
        _emit_norm(x_new, refs[2 * n_pairs + 1], outs[1])
    else:
        _tile_ss(x_new, outs[1])


def _mm_res(pairs, res, scale, tm=512, tn=512, next_gain=None):
    m, n = res.shape
    with_norm = next_gain is not None
    assert tn == n or not with_norm
    in_specs, args, scratch = [], [], []
    once = dict(pipeline_mode=pl.Buffered(1)) if n // tn <= 2 else {}
    for a, w in pairs:
        k = a.shape[1]
        in_specs += [_row_spec(tm, k), _col_spec(w, k, tn, **once)]
        args += [a, _arr(w)]
        scratch.append(pltpu.VMEM((k, tn), BF16))
    in_specs.append(_tile_spec(tm, tn))
    args.append(res)
    if with_norm:
        in_specs.append(_gain_spec(n))
        args.append(next_gain.reshape(1, n))
        second = (_tile_spec(tm, tn), jax.ShapeDtypeStruct((m, n), BF16))
    else:
        second = _ss_out(n, m, tm, tn)
    return pl.pallas_call(
        functools.partial(_mm_res_kernel, n_pairs=len(pairs), scale=scale, with_norm=with_norm),
        grid=(n // tn, m // tm),
        in_specs=in_specs,
        out_specs=[_tile_spec(tm, tn), second[0]],
        out_shape=[jax.ShapeDtypeStruct((m, n), F32), second[1]],
        scratch_shapes=scratch,
        compiler_params=_params("parallel", "arbitrary"),
        name="mm_res",
    )(*args)


def _lane_lo(shape=(1, LANES)):
    return lax.broadcasted_iota(jnp.int32, shape, len(shape) - 1) < HEAD_DIM


def _rope_tables(n):
    half = HEAD_DIM // 2
    inv_freq = ROPE_THETA ** (-jnp.arange(half, dtype=F32) / half)
    ang = jnp.arange(n, dtype=F32)[:, None] * inv_freq[None, :]
    cos, sin = jnp.cos(ang), jnp.sin(ang)
    return (jnp.concatenate([cos, cos, cos, cos], axis=-1),
            jnp.concatenate([-sin, sin, -sin, sin], axis=-1))


def _rope_tile(x, cos, sin):
    lane = lax.broadcasted_iota(jnp.int32, (1, LANES), 1)
    first_half = (lane % HEAD_DIM) < (HEAD_DIM // 2)
    partner = jnp.where(first_half, pltpu.roll(x, LANES - HEAD_DIM // 2, 1),
                        pltpu.roll(x, HEAD_DIM // 2, 1))
    return x * cos + partner * sin


def _attn_a_kernel(q_ref, k_ref, v_ref, cos_ref, sin_ref, o_ref, qs, ks, m_s, l_s, acc_s):
    n = q_ref.shape[1]
    lo = _lane_lo()
    rb = 256

    def rope_body(i, carry):
        sl = pl.ds(pl.multiple_of(i * rb, rb), rb)
        c, s = cos_ref[sl, :], sin_ref[sl, :]
        qs[sl, :] = _rope_tile(q_ref[0, sl, :], c, s) * (HEAD_DIM ** -0.5)
        ks[sl, :] = _rope_tile(k_ref[0, sl, :], c, s)
        return carry

    lax.fori_loop(0, n // rb, rope_body, 0)
    m_s[...] = jnp.full(m_s.shape, MASKED, F32)
    l_s[...] = jnp.zeros(l_s.shape, F32)
    acc_s[...] = jnp.zeros(acc_s.shape, F32)

    group = 8

    for dil in A_DILATIONS:
        m = n // dil
        qb_rows = m if m <= 2 * A_QUERY_ROWS else A_QUERY_ROWS
        span = min(qb_rows + 2 * A_HALF, m)
        nqb = m // qb_rows

        def blocks(items, dil=dil, m=m, qb_rows=qb_rows, span=span):
            q0 = [qb * qb_rows for _, qb in items]
            k0 = [jnp.clip(t - A_HALF, 0, m - span) for t in q0]
            if dil == 1:
                rows_q = [pl.ds(pl.multiple_of(t, qb_rows), qb_rows) for t in q0]
                rows_k = [pl.ds(pl.multiple_of(t, A_HALF), span) for t in k0]
            else:
                rows_q = [pl.ds(r + t * dil, qb_rows, stride=dil) for (r, _), t in zip(items, q0)]
                rows_k = [pl.ds(r + t * dil, span, stride=dil) for (r, _), t in zip(items, k0)]
            unstack = lambda t: jnp.where(lo, t[:qb_rows], t[qb_rows:])
            q2 = [jnp.concatenate([jnp.where(lo, t, 0.0), jnp.where(lo, 0.0, t)], axis=0).astype(BF16)
                  for t in (qs[rq, :] for rq in rows_q)]
            kb = [ks[rk, :].astype(BF16) for rk in rows_k]
            vb = [v_ref[0, rk, :].astype(BF16) for rk in rows_k]
            m_old = [m_s[rq, :] for rq in rows_q]
            l_old = [l_s[rq, :] for rq in rows_q]
            acc_old = [acc_s[rq, :] for rq in rows_q]
            qrow = lax.broadcasted_iota(jnp.int32, (2 * qb_rows, 1), 0) % qb_rows
            kcol = lax.broadcasted_iota(jnp.int32, (1, span), 1)
            mask = _each(lambda a, c: jnp.abs(a + qrow - c - kcol) <= A_HALF, q0, k0)
            s = _each(lambda q, kb, mk: jnp.where(mk, _dot_nt(q, kb), MASKED), q2, kb, mask)
            mn = _each(lambda mo, s: jnp.maximum(jnp.concatenate([mo[:, :1], mo[:, LANES - 1:]], axis=0),
                                                 jnp.max(s, axis=1, keepdims=True)), m_old, s)
            p = _each(lambda s, mn: jnp.exp(s - mn), s, mn)
            psum = _each(lambda p: jnp.sum(p, axis=1, keepdims=True), p)
            pv = _each(_dot, p, vb)
            m_new = _each(unstack, mn)
            alpha = _each(lambda mo, mn: jnp.exp(mo - mn), m_old, m_new)
            for rq, mnew, al, lold, ps, ao, pvi in zip(rows_q, m_new, alpha, l_old, psum, acc_old, pv):
                m_s[rq, :] = mnew
                l_s[rq, :] = al * lold + unstack(ps)
                acc_s[rq, :] = al * ao + unstack(pvi)

        per = math.gcd(group, dil * nqb)

        def group_body(it, carry, blocks=blocks, dil=dil, per=per):
            ts = [it * per + u for u in range(per)]
            blocks([(t & (dil - 1), t >> (dil.bit_length() - 1)) for t in ts])
            return carry

        lax.fori_loop(0, dil * nqb // per, group_body, 0)

    def out_body(i, carry):
        sl = pl.ds(pl.multiple_of(i * rb, rb), rb)
        o_ref[0, sl, :] = (acc_s[sl, :] / l_s[sl, :]).astype(o_ref.dtype)
        return carry

    lax.fori_loop(0, n // rb, out_body, 0)


def _attn_a(za, cos, sin):
    b, n, _ = za.shape
    npair = A_W // LANES
    blk = lambda off: pl.BlockSpec((1, n, LANES), lambda i, j, off=off: (i, 0, off + j))
    tab = pl.BlockSpec((n, LANES), lambda i, j: (0, 0))
    return pl.pallas_call(
        _attn_a_kernel,
        grid=(b, npair),
        in_specs=[blk(0), blk(npair), blk(2 * npair), tab, tab],
        out_specs=pl.BlockSpec((1, n, LANES), lambda i, j: (i, 0, j)),
        out_shape=jax.ShapeDtypeStruct((b, n, A_W), BF16),
        scratch_shapes=[pltpu.VMEM((n, LANES), F32)] * 5,
        compiler_params=_params("parallel", "parallel"),
        name="attn_dilated",
    )(za, za, za, cos, sin)


def _attn_c_kernel(sink_ref, q_ref, k_ref, v_ref, cosq_ref, sinq_ref, cos_ref, sin_ref, o_ref,
                   klo, khi, vlo, vhi):
    n = k_ref.shape[1]
    qb_rows = q_ref.shape[1]
    span = qb_rows + 2 * C_HALF
    hk = pl.program_id(1)
    qb = pl.program_id(2)
    lo = _lane_lo()

    @pl.when(qb == 0)
    def _():
        rb = 256

        def body(i, carry):
            sl = pl.ds(pl.multiple_of(i * rb, rb), rb)
            kx = _rope_tile(k_ref[0, sl, :], cos_ref[sl, :], sin_ref[sl, :])
            kw = pltpu.roll(kx, HEAD_DIM, 1)
            vx = v_ref[0, sl, :]
            vw = pltpu.roll(vx, HEAD_DIM, 1)
            first = hk == 0
            klo[sl, :] = jnp.where(lo, jnp.where(first, kx, kw), 0.0).astype(BF16)
            khi[sl, :] = jnp.where(lo, 0.0, jnp.where(first, kw, kx)).astype(BF16)
            vlo[sl, :] = jnp.where(lo, jnp.where(first, vx, vw), 0.0).astype(BF16)
            vhi[sl, :] = jnp.where(lo, 0.0, jnp.where(first, vw, vx)).astype(BF16)
            return carry

        lax.fori_loop(0, n // rb, body, 0)

    q0 = qb * qb_rows
    k0 = pl.multiple_of(jnp.clip(q0 - C_HALF, 0, n - span), C_HALF)
    rows_k = pl.ds(k0, span)
    kl, kh = klo[rows_k, :], khi[rows_k, :]
    vl, vh = vlo[rows_k, :], vhi[rows_k, :]
    qpos = q0 + lax.broadcasted_iota(jnp.int32, (qb_rows, 1), 0)
    kpos = k0 + lax.broadcasted_iota(jnp.int32, (1, span), 1)
    mask = jnp.abs(qpos - kpos) <= C_HALF
    cq, sq = cosq_ref[...], sinq_ref[...]

    ntile = q_ref.shape[2] // LANES
    qt = [_rope_tile(q_ref[0, :, j * LANES:(j + 1) * LANES], cq, sq) * (HEAD_DIM ** -0.5) for j in range(ntile)]
    qt = [t for t in qt for _ in range(2)]
    kk = [kl, kh] * ntile
    sk = [sink_ref[hk * C_GROUP + h] for h in range(2 * ntile)]
    s = _each(lambda q, k: jnp.where(mask, _dot_nt(q, k), MASKED), qt, kk)
    mx = _each(lambda s, sk: jnp.maximum(jnp.max(s, axis=1, keepdims=True), sk), s, sk)
    p = _each(lambda s, mx: jnp.exp(s - mx), s, mx)
    den = _each(lambda p, sk, mx: jnp.sum(p, axis=1, keepdims=True) + jnp.exp(sk - mx), p, sk, mx)
    for j in range(ntile):
        pv = _dot(p[2 * j], vl) + _dot(p[2 * j + 1], vh)
        o_ref[0, :, j * LANES:(j + 1) * LANES] = (pv / jnp.where(lo, den[2 * j], den[2 * j + 1])).astype(o_ref.dtype)


def _attn_c(zc, sink, cos, sin, qb_rows=128):
    b, n, _ = zc.shape
    gw = C_GROUP * HEAD_DIM
    kcol = C_W // LANES
    full = lambda off: pl.BlockSpec((1, n, LANES), lambda i, h, t, off=off: (i, 0, off))
    tabq = pl.BlockSpec((qb_rows, LANES), lambda i, h, t: (t, 0))
    tab = pl.BlockSpec((n, LANES), lambda i, h, t: (0, 0))
    return pl.pallas_call(
        _attn_c_kernel,
        grid=(b, C_KV_HEADS, n // qb_rows),
        in_specs=[pl.BlockSpec(memory_space=pltpu.SMEM),
                  pl.BlockSpec((1, qb_rows, gw), lambda i, h, t: (i, t, h)),
                  full(kcol), full(kcol + 1), tabq, tabq, tab, tab],
        out_specs=pl.BlockSpec((1, qb_rows, gw), lambda i, h, t: (i, t, h)),
        out_shape=jax.ShapeDtypeStruct((b, n, C_W), BF16),
        scratch_shapes=[pltpu.VMEM((n, LANES), BF16)] * 4,
        compiler_params=_params("parallel", "parallel", "arbitrary"),
        name="attn_gqa_sink",
    )(sink.reshape(-1), zc, zc, zc, cos, sin, cos, sin)


def _na_bias_table(rpb, rows):
    kh = min(NA_KH, rows)
    qc = jnp.arange(GRID_W)[:, None]
    kc = jnp.arange(GRID_W)[None, :]
    qc0 = jnp.clip(qc - NA_KW // 2, 0, GRID_W - NA_KW)
    valid = (kc >= qc0) & (kc < qc0 + NA_KW)
    ci = jnp.clip(kc - qc, -(NA_KW - 1), NA_KW - 1) + NA_KW - 1
    nci = 2 * NA_KW - 1
    pairs = rpb.astype(F32).reshape(D_HEADS // 2, 2, 2 * NA_KH - 1, nci)
    pairs = jnp.concatenate([pairs, jnp.full(pairs.shape[:-1] + (1,), MASKED, F32)], axis=-1)
    by_row = jnp.stack([pairs[:, :, off:off + kh] for off in range(NA_KH)], axis=1)
    cls = jnp.where(valid, ci, nci)
    onehot = (cls[None] == jnp.arange(nci + 1)[:, None, None]).astype(F32)
    bias = jnp.einsum('poekc,cqj->poeqkj', by_row, onehot, precision=lax.Precision.HIGHEST)
    return bias.reshape(D_HEADS // 2, NA_KH, 2 * GRID_W, kh * GRID_W)


def _attn_d_kernel(q_ref, k_ref, v_ref, bias_ref, o_ref):
    n = q_ref.shape[1]
    rows = n // GRID_W
    kh = min(NA_KH, rows)
    lo = _lane_lo()
    group = math.gcd(8, rows)

    def body(rg, carry):
        r = [rg * group + u for u in range(group)]
        r0 = [jnp.clip(t - NA_KH // 2, 0, rows - kh) for t in r]
        rows_q = [pl.ds(pl.multiple_of(t * GRID_W, GRID_W), GRID_W) for t in r]
        rows_k = [pl.ds(pl.multiple_of(t * GRID_W, GRID_W), kh * GRID_W) for t in r0]
        q = [q_ref[0, rq, :] * (HEAD_DIM ** -0.5) for rq in rows_q]
        q2 = _each(lambda q: jnp.concatenate([jnp.where(lo, q, 0.0), jnp.where(lo, 0.0, q)], axis=0), q)
        s = [_dot_nt(q2i, k_ref[0, rk, :]) + bias_ref[0, t0 - t + NA_KH - 1]
             for q2i, rk, t0, t in zip(q2, rows_k, r0, r)]
        p = _each(lambda s: jnp.exp(s - jnp.max(s, axis=1, keepdims=True)), s)
        pv = [_dot(pi, v_ref[0, rk, :]) / jnp.sum(pi, axis=1, keepdims=True) for pi, rk in zip(p, rows_k)]
        for rq, pvi in zip(rows_q, pv):
            o_ref[0, rq, :] = jnp.where(lo, pvi[:GRID_W], pvi[GRID_W:]).astype(o_ref.dtype)
        return carry

    lax.fori_loop(0, rows // group, body, 0)


def _attn_d(zc, bias):
    b, n, _ = zc.shape
    npair = D_W // LANES
    q_off = (C_W + 2 * C_KV_HEADS * HEAD_DIM) // LANES
    blk = lambda off: pl.BlockSpec((1, n, LANES), lambda i, j, off=off: (i, 0, off + j))
    return pl.pallas_call(
        _attn_d_kernel,
        grid=(b, npair),
        in_specs=[blk(q_off), blk(q_off + npair), blk(q_off + 2 * npair),
                  pl.BlockSpec((1,) + bias.shape[1:], lambda i, j: (j, 0, 0, 0))],
        out_specs=pl.BlockSpec((1, n, LANES), lambda i, j: (i, 0, j)),
        out_shape=jax.ShapeDtypeStruct((b, n, D_W), BF16),
        compiler_params=_params("parallel", "parallel"),
        name="attn_neighborhood",
    )(zc, zc, zc, bias)


def _rw_prep_kernel(z_ref, hp_ref, hn_ref, mup_ref, mun_ref, w0_ref, w2f_ref, w2b_ref, a0_ref,
                    a2f_ref, a2b_ref, g2_ref, r_o, k_o, v_o, cumf_o, cumb_o, af_o, ab_o, g_o):
    tm = z_ref.shape[1]
    first = pl.program_id(1) == 0
    last = pl.program_id(1) == pl.num_programs(1) - 1
    row = lax.broadcasted_iota(jnp.int32, (tm, 1), 0)
    crow = row % WKV_CHUNK

    def edges(cols):
        return (jnp.where(first, 0.0, hp_ref[0, SUBLANES - 1:SUBLANES, cols]),
                jnp.where(last, 0.0, hn_ref[0, 0:1, cols]))

    def mix(z, z_prev, z_next, cols):
        return z + mup_ref[:, cols] * (z_prev - z) + mun_ref[:, cols] * (z_next - z)

    def shifted(cols):
        z = z_ref[0, :, cols]
        edge_prev, edge_next = edges(cols)
        z_prev = jnp.where(row == 0, edge_prev, pltpu.roll(z, 1, 0))
        z_next = jnp.where(row == tm - 1, edge_next, pltpu.roll(z, tm - 1, 0))
        return mix(z, z_prev, z_next, cols)

    def store_shifted(o_ref, cs, cols):
        z = z_ref[0, :, cols]
        edge_prev, edge_next = edges(cols)
        o_ref[0, :, cs] = mix(z, pltpu.roll(z, 1, 0), pltpu.roll(z, tm - 1, 0), cols)
        o_ref[0, 0:1, cs] = mix(z[0:1], edge_prev, z[1:2], cols)
        o_ref[0, tm - 1:tm, cs] = mix(z[tm - 1:tm], z[tm - 2:tm - 1], edge_next, cols)

    def log_decay(x):
        return -RW_DECAY_SCALE * jax.nn.sigmoid(x)

    def chunk_cumsum(x, rev):
        s = 1
        while s < WKV_CHUNK:
            if rev:
                x = x + jnp.where(crow < WKV_CHUNK - s, pltpu.roll(x, tm - s, 0), 0.0)
            else:
                x = x + jnp.where(crow >= s, pltpu.roll(x, s, 0), 0.0)
            s *= 2
        return x

    base = 3 * B_W
    wl = jnp.tanh(shifted(slice(base, base + LANES))).astype(BF16)
    al = shifted(slice(base + LANES, base + 2 * LANES)).astype(BF16)
    gl = jax.nn.sigmoid(shifted(slice(base + 2 * LANES, base + 4 * LANES))).astype(BF16)

    for s0 in range(0, B_W, LANES):
        cs = slice(s0, s0 + LANES)
        store_shifted(r_o, cs, cs)
        store_shifted(k_o, cs, slice(B_W + s0, B_W + s0 + LANES))
        store_shifted(v_o, cs, slice(2 * B_W + s0, 2 * B_W + s0 + LANES))
        cumf_o[0, :, cs] = chunk_cumsum(log_decay(w0_ref[0:1, cs] + _dot(wl, w2f_ref[:, cs])), False)
        cumb_o[0, :, cs] = chunk_cumsum(log_decay(w0_ref[1:2, cs] + _dot(wl, w2b_ref[:, cs])), True)
        af_o[0, :, cs] = jax.nn.sigmoid(a0_ref[0:1, cs] + _dot(al, a2f_ref[:, cs]))
        ab_o[0, :, cs] = jax.nn.sigmoid(a0_ref[1:2, cs] + _dot(al, a2b_ref[:, cs]))
        g_o[0, :, cs] = _dot(gl, g2_ref[:, cs])


def _rw_prep(zb, mu_prev, mu_next, w0, w2, a0, a2, g2, tm=256):
    b, n, cols = zb.shape
    tm = min(tm, n)
    per = tm // SUBLANES
    pad_cols = lambda t: jnp.pad(t, (0, cols - t.shape[0])).reshape(1, cols)
    rows_f = lambda t: jnp.pad(t, ((0, LANES - RW_LORA), (0, 0))).astype(BF16)
    rows_b = lambda t: jnp.pad(t, ((RW_LORA, LANES - 2 * RW_LORA), (0, 0))).astype(BF16)
    g2p = jnp.pad(g2, ((0, 2 * LANES - RW_GATE_LORA), (0, 0))).astype(BF16)
    const = lambda shape: pl.BlockSpec(shape, lambda i, t: (0,) * len(shape))
    halo_prev = pl.BlockSpec((1, SUBLANES, cols), lambda i, t: (i, jnp.maximum(t * per - 1, 0), 0))
    halo_next = pl.BlockSpec((1, SUBLANES, cols), lambda i, t: (i, jnp.minimum((t + 1) * per, n // SUBLANES - 1), 0))
    tok = jax.ShapeDtypeStruct((b, n, B_W), F32)
    return pl.pallas_call(
        _rw_prep_kernel,
        grid=(b, n // tm),
        in_specs=[pl.BlockSpec((1, tm, cols), lambda i, t: (i, t, 0)), halo_prev, halo_next,
                  const((1, cols)), const((1, cols)), const((2, B_W)), const((LANES, B_W)),
                  const((LANES, B_W)), const((2, B_W)), const((LANES, B_W)), const((LANES, B_W)),
                  const((2 * LANES, B_W))],
        out_specs=[pl.BlockSpec((1, tm, B_W), lambda i, t: (i, t, 0))] * 8,
        out_shape=[tok] * 8,
        compiler_params=_params("parallel", "parallel"),
        name="rwkv_prep",
    )(zb, zb, zb, pad_cols(mu_prev), pad_cols(mu_next), w0, rows_f(w2[0]), rows_b(w2[1]),
      a0, rows_f(a2[0]), rows_b(a2[1]), g2p)


def _head_sum(x, lo):
    return jnp.where(lo, jnp.sum(jnp.where(lo, x, 0.0), axis=1, keepdims=True),
                     jnp.sum(jnp.where(lo, 0.0, x), axis=1, keepdims=True))


def _wkv_chunks(chains):
    c = chains[0][0].shape[0]
    ii = lax.broadcasted_iota(jnp.int32, (c, c), 0)
    jj = lax.broadcasted_iota(jnp.int32, (c, c), 1)
    row = lax.broadcasted_iota(jnp.int32, (c, 1), 0)
    eye_f = (ii == jj).astype(F32)
    masks = {False: (jj <= ii, jj < ii), True: (jj >= ii, jj > ii)}
    lo = _lane_lo()
    ki = lax.broadcasted_iota(jnp.int32, (LANES, LANES), 0)
    vj = lax.broadcasted_iota(jnp.int32, (LANES, LANES), 1)
    same_head = (ki < HEAD_DIM) == (vj < HEAD_DIM)
    diag = ki == vj
    r, k, v, cum, rate, st, k_k, k_a, rev = (list(col) for col in zip(*chains))
    twice = lambda xs: [x for x in xs for _ in range(2)]
    merge = lambda xs: [jnp.where(lo, xs[2 * i], xs[2 * i + 1]) for i in range(len(xs) // 2)]
    incl = twice([masks[x][0] for x in rev])
    strict = twice([masks[x][1] for x in rev])

    kk = _each(lambda k, k_k: k * k_k, k, k_k)
    kk = _each(lambda t: t / jnp.maximum(jnp.sqrt(_head_sum(t * t, lo)), 1e-12), kk)
    kd = _each(lambda k, rate, k_a: k * (1.0 + (rate - 1.0) * k_a), k, rate, k_a)
    b = _each(lambda kk, rate: kk * rate, kk, rate)

    tot = _each(lambda cum, x: cum[0:1] if x else cum[c - 1:c], cum, rev)
    excl = _each(lambda cum, x: jnp.where(row == c - 1, 0.0, pltpu.roll(cum, c - 1, 0)) if x
                 else jnp.where(row == 0, 0.0, pltpu.roll(cum, 1, 0)), cum, rev)
    ar = _each(lambda kk, r, cum, ex: jnp.concatenate([-kk * jnp.exp(ex), r * jnp.exp(cum)], axis=0),
               kk, r, cum, excl)
    ar_h = [jnp.where(m, x, 0.0).astype(BF16) for x in ar for m in (lo, ~lo)]
    ar = _each(lambda t: t.astype(BF16), ar)
    v = _each(lambda t: t.astype(BF16), v)
    g_inv = _each(lambda cum: jnp.exp(-cum), cum)
    bg = twice(_each(lambda b, g: (b * g).astype(BF16), b, g_inv))
    kg = twice(_each(lambda kd, g: (kd * g).astype(BF16), kd, g_inv))
    x1 = _each(_dot_nt, ar_h, bg)
    x2 = _each(_dot_nt, ar_h, kg)
    a_ab = _each(lambda x, m: jnp.where(m, x[:c], 0.0), x1, strict)
    a_rb = _each(lambda x, m: jnp.where(m, x[c:], 0.0).astype(BF16), x1, incl)
    a_k = _each(lambda x, ms, mi: jnp.concatenate([jnp.where(ms, x[:c], 0.0), jnp.where(mi, x[c:], 0.0)],
                                                  axis=0).astype(BF16), x2, strict, incl)

    inv = _each(lambda t: eye_f + t, a_ab)
    pw = _each(lambda t: _dot(t, t), a_ab)
    for _ in range(c.bit_length() - 3):
        pw = _each(lambda t: t.astype(BF16), pw)
        both = _each(lambda inv, pw: _dot(jnp.concatenate([inv.astype(BF16), pw], axis=0), pw), inv, pw)
        inv = _each(lambda inv, t: inv + t[:c], inv, both)
        pw = _each(lambda t: t[c:], both)
    inv = _each(lambda inv, pw: (inv + _dot(inv, pw)).astype(BF16), inv, pw)

    akv = merge(_each(_dot, a_k, twice(v)))
    ars = _each(_dot, ar, st)
    rhs = _each(lambda ars, akv: (ars[:c] + akv[:c]).astype(BF16), ars, akv)
    p = _each(lambda t: t.astype(BF16), merge(_each(_dot, inv, twice(rhs))))
    y = _each(lambda ars, arb, akv: ars[c:] + arb + akv[c:], ars, merge(_each(_dot, a_rb, twice(p))), akv)
    g_end = _each(lambda tot, cum: jnp.exp(tot - cum), tot, cum)
    upd = _each(lambda b, kd, g, p, v: _dot_tn(jnp.concatenate([b * g, kd * g], axis=0),
                                               jnp.concatenate([p, v], axis=0)), b, kd, g_end, p, v)
    g_col = _each(lambda tot: jnp.sum(jnp.where(diag, jnp.exp(tot), 0.0), axis=1, keepdims=True), tot)
    new_st = _each(lambda g, st, upd: jnp.where(same_head, g * st + upd, 0.0), g_col, st, upd)
    return list(zip(y, new_st))


def _rw_scan_kernel(rf, kf, vf, cumf, af, rb, kb, vb, cumb, ab, kk_ref, ka_ref, yf, yb, sf, sb):
    tb = rf.shape[1]
    npair = rf.shape[2] // LANES
    c = WKV_CHUNK
    nch = tb // c
    pair = lambda j: slice(j * LANES, (j + 1) * LANES)

    @pl.when(pl.program_id(2) == 0)
    def _():
        sf[...] = jnp.zeros(sf.shape, F32)
        sb[...] = jnp.zeros(sb.shape, F32)

    def body(ci, carry):
        rows_f = pl.ds(pl.multiple_of(ci * c, c), c)
        rows_b = pl.ds(pl.multiple_of((nch - 1 - ci) * c, c), c)
        chains = []
        for j in range(npair):
            par = (kk_ref[:, pair(j)], ka_ref[:, pair(j)])
            chains.append(tuple(ref[0, rows_f, pair(j)] for ref in (rf, kf, vf, cumf, af)) + (sf[j],) + par + (False,))
            chains.append(tuple(ref[0, rows_b, pair(j)] for ref in (rb, kb, vb, cumb, ab)) + (sb[j],) + par + (True,))
        outs = _wkv_chunks(chains)
        for j in range(npair):
            yf[0, rows_f, pair(j)], sf[j] = outs[2 * j]
            yb[0, rows_b, pair(j)], sb[j] = outs[2 * j + 1]
        return carry

    lax.fori_loop(0, nch, body, 0)


def _rw_scan(r, k, v, cumf, cumb, af, ab, k_k, k_a, hb=20, tb=256):
    b, n, width = r.shape
    tb = min(tb, n)
    nt = n // tb
    hw = hb * HEAD_DIM
    fwd = pl.BlockSpec((1, tb, hw), lambda i, g, t: (i, t, g))
    bwd = pl.BlockSpec((1, tb, hw), lambda i, g, t: (i, nt - 1 - t, g))
    par = pl.BlockSpec((1, hw), lambda i, g, t: (0, g))
    out = jax.ShapeDtypeStruct((b, n, width), F32)
    return pl.pallas_call(
        _rw_scan_kernel,
        grid=(b, width // hw, nt),
        in_specs=[fwd] * 5 + [bwd] * 5 + [par, par],
        out_specs=[fwd, bwd],
        out_shape=[out, out],
        scratch_shapes=[pltpu.VMEM((hw // LANES, LANES, LANES), F32)] * 2,
        compiler_params=_params("parallel", "parallel", "arbitrary"),
        name="rwkv_scan",
    )(r, k, v, cumf, af, r, k, v, cumb, ab, k_k.reshape(1, width), k_a.reshape(1, width))


def _rw_post_kernel(yf, yb, r, k, v, af, ab, g, ka_ref, rk_ref, lnw_ref, lnb_ref, o_ref):
    lo = _lane_lo()
    head_sum = lambda x: _head_sum(x, lo)
    for j in range(o_ref.shape[2] // LANES):
        sl = slice(j * LANES, (j + 1) * LANES)
        y = yf[0, :, sl] + yb[0, :, sl]
        dev = y - head_sum(y) * (1.0 / HEAD_DIM)
        var = head_sum(dev * dev) * (1.0 / HEAD_DIM)
        yn = dev * lax.rsqrt(var + RW_GN_EPS) * lnw_ref[:, sl] + lnb_ref[:, sl]
        kt, k_a = k[0, :, sl], ka_ref[:, sl]
        kd = kt * (1.0 + (af[0, :, sl] - 1.0) * k_a) + kt * (1.0 + (ab[0, :, sl] - 1.0) * k_a)
        bonus = head_sum(r[0, :, sl] * kd * rk_ref[:, sl]) * v[0, :, sl]
        o_ref[0, :, sl] = ((yn + bonus) * g[0, :, sl]).astype(o_ref.dtype)


def _rw_post(yf, yb, r, k, v, af, ab, g, k_a, r_k, ln_w, ln_b, tm=256):
    b, n, width = r.shape
    tm = min(tm, n)
    tok = pl.BlockSpec((1, tm, width), lambda i, t: (i, t, 0))
    par = pl.BlockSpec((1, width), lambda i, t: (0, 0))
    return pl.pallas_call(
        _rw_post_kernel,
        grid=(b, n // tm),
        in_specs=[tok] * 8 + [par] * 4,
        out_specs=tok,
        out_shape=jax.ShapeDtypeStruct((b, n, width), BF16),
        compiler_params=_params("parallel", "parallel"),
        name="rwkv_post",
    )(yf, yb, r, k, v, af, ab, g, k_a.reshape(1, width), r_k.reshape(1, width),
      ln_w.reshape(1, width), ln_b.reshape(1, width))


def _rwkv7_bidir(zb, mu_prev, mu_next, w0, w2, a0, a2, g2, k_k, k_a, r_k, ln_w, ln_b):
    r, k, v, cumf, cumb, af, ab, g = _rw_prep(zb, mu_prev, mu_next, w0, w2, a0, a2, g2)
    yf, yb = _rw_scan(r, k, v, cumf, cumb, af, ab, k_k, k_a)
    return _rw_post(yf, yb, r, k, v, af, ab, g, k_a, r_k, ln_w, ln_b)


def _ffn_half_step(x, xn, w1, w3, w2, layer):
    h = _ffn_up(xn, (w1, layer), (w3, layer))
    return _mm_res([(h, (w2, layer))], x, 0.5, tm=256, tn=1024)


def _mix_ab(x, ss, g, next_gain, b, n, cos, sin, w_in, layer, w_out, mu_prev, mu_next, w0, w2, a0, a2, g2,
            k_k, k_a, r_k, ln_w, ln_b):
    qkv = 3 * A_W
    w_b = jnp.pad(w_in[layer, :, qkv:], ((0, 0), (0, RW_PAD_COLS - (w_in.shape[2] - qkv))))
    za = _proj(x, ss, g, w_in[layer, :, :qkv], qkv, tn=qkv).reshape(b, n, qkv)
    zb = _proj(x, ss, g, w_b, RW_PAD_COLS, tn=RW_PAD_COLS // 2).reshape(b, n, RW_PAD_COLS)
    oa = _attn_a(za, cos, sin).reshape(b * n, A_W)
    ob = _rwkv7_bidir(zb, mu_prev, mu_next, w0, w2, a0, a2, g2, k_k, k_a, r_k, ln_w, ln_b)
    return _mm_res([(oa, w_out[:A_W]), (ob.reshape(b * n, B_W), w_out[A_W:])], x, 1.0, tn=x.shape[1],
                   next_gain=next_gain)


def _mix_cd(x, ss, g, next_gain, b, n, cos, sin, w_in, layer, w_out, sink, rpb):
    cols = w_in.shape[2]
    zc = _proj(x, ss, g, (w_in, layer), cols, tn=cols // 2).reshape(b, n, cols)
    oc = _attn_c(zc, sink, cos, sin).reshape(b * n, C_W)
    od = _attn_d(zc, _na_bias_table(rpb, n // GRID_W)).reshape(b * n, D_W)
    return _mm_res([(oc, w_out[:C_W]), (od, w_out[C_W:])], x, 1.0, tn=x.shape[1], next_gain=next_gain)


def kernel(x, p, ffn1_norm, ffn1_w1, ffn1_w3, ffn1_w2, mix_norm, ffn2_norm, ffn2_w1, ffn2_w3, ffn2_w2, ple_norm, ple_w_gate, ple_w_proj, ab_w_in, ab_w_out, rw_mu_prev, rw_mu_next, rw_w0, rw_w2, rw_a0, rw_a2, rw_g2, rw_k_k, rw_k_a, rw_r_k, rw_ln_w, rw_ln_b, cd_w_in, cd_w_out, c_sink, d_rpb, final_norm):
    b, n, d = x.shape
    depth = p.shape[0]
    cos, sin = _rope_tables(n)
    x = x.reshape(b * n, d)
    p = p.reshape(depth, b * n, -1)
    xn = _rmsnorm(x, ffn1_norm[0], BF16)
    for i in range(depth):
        j = i // 2
        x, ss = _ffn_half_step(x, xn, ffn1_w1, ffn1_w3, ffn1_w2, i)
        if i % 2 == 0:
            x, xn = _mix_ab(x, ss, mix_norm[i], ffn2_norm[i], b, n, cos, sin, ab_w_in, j, ab_w_out[j],
                            rw_mu_prev[j], rw_mu_next[j], rw_w0[j], rw_w2[j], rw_a0[j], rw_a2[j], rw_g2[j],
                            rw_k_k[j], rw_k_a[j], rw_r_k[j], rw_ln_w[j], rw_ln_b[j])
        else:
            x, xn = _mix_cd(x, ss, mix_norm[i], ffn2_norm[i], b, n, cos, sin, cd_w_in, j, cd_w_out[j],
                            c_sink[j], d_rpb[j])
        x, ss = _ffn_half_step(x, xn, ffn2_w1, ffn2_w3, ffn2_w2, i)
        ple_args = (x, ss, ple_norm[i], (ple_w_gate, i), (p, i), (ple_w_proj, i))
        if i + 1 < depth:
            x, xn = _ple(*ple_args, ffn1_norm[i + 1], False)
    return _ple(*ple_args, final_norm, True).reshape(b, n, d)
```

```python
import functools
import math

import jax
import jax.numpy as jnp
from jax import lax
from jax.experimental import pallas as pl
from jax.experimental.pallas import tpu as pltpu

F32 = jnp.float32
BF16 = jnp.bfloat16

HEAD_DIM = 64
LANES = 128
SUBLANES = 8
NORM_EPS = 1e-6
ROPE_THETA = 10000.0
GRID_W = 64
A_HEADS = 12
A_W = A_HEADS * HEAD_DIM
A_HALF = 64
A_DILATIONS = (1, 4, 16)
A_QUERY_ROWS = 64
B_HEADS = 20
B_W = B_HEADS * HEAD_DIM
RW_LORA = 64
RW_GATE_LORA = 192
RW_GN_EPS = 64e-5
RW_DECAY_SCALE = math.exp(-0.5)
RW_PAD_COLS = 3 * B_W + 4 * LANES
C_HEADS = 16
C_KV_HEADS = 2
C_GROUP = C_HEADS // C_KV_HEADS
C_W = C_HEADS * HEAD_DIM
C_HALF = 128
D_HEADS = 16
D_W = D_HEADS * HEAD_DIM
NA_KH = 8
NA_KW = 16
MASKED = -1e30
WKV_CHUNK = 64
VMEM_LIMIT = 56 * 1024 * 1024


def _params(*sem):
    return pltpu.CompilerParams(dimension_semantics=sem, vmem_limit_bytes=VMEM_LIMIT)


def _dot(a, b):
    return jnp.dot(a.astype(BF16), b.astype(BF16), preferred_element_type=F32)


def _dot_nt(a, b):
    return lax.dot_general(a.astype(BF16), b.astype(BF16), (((1,), (1,)), ((), ())),
                           preferred_element_type=F32)


def _dot_tn(a, b):
    return lax.dot_general(a.astype(BF16), b.astype(BF16), (((0,), (0,)), ((), ())),
                           preferred_element_type=F32)


def _each(f, *cols):
    return [f(*xs) for xs in zip(*cols)]


def _rmsnorm_kernel(x_ref, g_ref, o_ref):
    x = x_ref[...]
    ms = jnp.mean(x * x, axis=-1, keepdims=True)
    o_ref[...] = (x * lax.rsqrt(ms + NORM_EPS) * g_ref[...]).astype(o_ref.dtype)


def _rmsnorm(x, g, dtype, tm=512):
    m, d = x.shape
    return pl.pallas_call(
        _rmsnorm_kernel,
        grid=(m // tm,),
        in_specs=[pl.BlockSpec((tm, d), lambda i: (i, 0)),
                  pl.BlockSpec((1, d), lambda i: (0, 0))],
        out_specs=pl.BlockSpec((tm, d), lambda i: (i, 0)),
        out_shape=jax.ShapeDtypeStruct((m, d), dtype),
        compiler_params=_params("parallel"),
        name="rmsnorm",
    )(x, g.reshape(1, d))


def _cast_weight(w_ref, wb_ref):
    rows = 256

    @pl.when(pl.program_id(1) == 0)
    def _():
        def body(i, carry):
            sl = pl.ds(pl.multiple_of(i * rows, rows), rows)
            wb_ref[sl, :] = w_ref[sl, :].astype(BF16)
            return carry

        lax.fori_loop(0, w_ref.shape[0] // rows, body, 0)


def _row_spec(tm, k):
    return pl.BlockSpec((tm, k), lambda j, i: (i, 0))


def _col_spec(w, k, tn, **kw):
    if isinstance(w, tuple):
        layer, row_block = w[1], (w[2] if len(w) > 2 else 0)
        return pl.BlockSpec((None, k, tn), lambda j, i: (layer, row_block, j), **kw)
    return pl.BlockSpec((k, tn), lambda j, i: (0, j), **kw)


def _arr(w):
    return w[0] if isinstance(w, tuple) else w


def _tile_spec(tm, tn):
    return pl.BlockSpec((tm, tn), lambda j, i: (i, j))


def _normed(x_ref, ss_ref, g_ref):
    ss = ss_ref[0, :, :1]
    for t in range(1, ss_ref.shape[0]):
        ss = ss + ss_ref[t, :, :1]
    rs = lax.rsqrt(ss * (1.0 / x_ref.shape[1]) + NORM_EPS)
    return (x_ref[...] * rs * g_ref[...]).astype(BF16)


def _ss_spec(nt, tm):
    return pl.BlockSpec((nt, tm, LANES), lambda j, i: (0, i, 0))


def _gain_spec(d):
    return pl.BlockSpec((1, d), lambda j, i: (0, 0))


def _ffn_up_kernel(xn_ref, w1_ref, w3_ref, o_ref, w1b, w3b):
    _cast_weight(w1_ref, w1b)
    _cast_weight(w3_ref, w3b)
    xn = xn_ref[...]
    a = jnp.dot(xn, w1b[...], preferred_element_type=F32)
    b = jnp.dot(xn, w3b[...], preferred_element_type=F32)
    o_ref[...] = (a * jax.nn.sigmoid(a) * b).astype(o_ref.dtype)


def _ffn_up(xn, w1, w3, tm=1024, tn=512):
    m, d = xn.shape
    f = _arr(w1).shape[-1]
    return pl.pallas_call(
        _ffn_up_kernel,
        grid=(f // tn, m // tm),
        in_specs=[_row_spec(tm, d), _col_spec(w1, d, tn), _col_spec(w3, d, tn)],
        out_specs=_tile_spec(tm, tn),
        out_shape=jax.ShapeDtypeStruct((m, f), BF16),
        scratch_shapes=[pltpu.VMEM((d, tn), BF16)] * 2,
        compiler_params=_params("parallel", "arbitrary"),
        name="ffn_up",
    )(xn, _arr(w1), _arr(w3))


def _proj_kernel(x_ref, ss_ref, g_ref, w_ref, o_ref, wb):
    _cast_weight(w_ref, wb)
    o_ref[...] = jnp.dot(_normed(x_ref, ss_ref, g_ref), wb[...], preferred_element_type=F32)


def _proj(x, ss, g, w, ncols, tn, tm=512):
    m, d = x.shape
    return pl.pallas_call(
        _proj_kernel,
        grid=(ncols // tn, m // tm),
        in_specs=[_row_spec(tm, d), _ss_spec(ss.shape[0], tm), _gain_spec(d),
                  _col_spec(w, d, tn, pipeline_mode=pl.Buffered(1))],
        out_specs=_tile_spec(tm, tn),
        out_shape=jax.ShapeDtypeStruct((m, ncols), F32),
        scratch_shapes=[pltpu.VMEM((d, tn), BF16)],
        compiler_params=_params("parallel", "arbitrary"),
        name="proj",
    )(x, ss, g.reshape(1, d), _arr(w))


def _tile_ss(x_new, ss_ref):
    ss_ref[0] = jnp.broadcast_to(jnp.sum(x_new * x_new, axis=1, keepdims=True), ss_ref.shape[1:])


def _ss_out(n, m, tm, tn):
    return (pl.BlockSpec((1, tm, LANES), lambda j, i: (j, i, 0)),
            jax.ShapeDtypeStruct((n // tn, m, LANES), F32))


def _emit_norm(x_new, g_ref, o_ref):
    ms = jnp.mean(x_new * x_new, axis=-1, keepdims=True)
    o_ref[...] = (x_new * lax.rsqrt(ms + NORM_EPS) * g_ref[...]).astype(o_ref.dtype)


def _ple_kernel(x_ref, ss_ref, g_ref, w_ref, p_ref, wp_ref, gn_ref, *refs, final):
    wb, wpb = refs[-2:]
    _cast_weight(w_ref, wb)
    _cast_weight(wp_ref, wpb)
    gate = jax.nn.sigmoid(jnp.dot(_normed(x_ref, ss_ref, g_ref), wb[...], preferred_element_type=F32))
    emb = jnp.dot(p_ref[...].astype(BF16), wpb[...], preferred_element_type=F32)
    x_new = x_ref[...] + emb * gate
    if not final:
        refs[0][...] = x_new
    _emit_norm(x_new, gn_ref, refs[-3])


def _ple(x, ss, g, w_gate, p, w_proj, next_gain, final, tm=512):
    m, d = x.shape
    pd = p[0].shape[-1]
    once = dict(pipeline_mode=pl.Buffered(1))
    rows = _tile_spec(tm, d)
    normed = jax.ShapeDtypeStruct((m, d), F32 if final else BF16)
    return pl.pallas_call(
        functools.partial(_ple_kernel, final=final),
        grid=(1, m // tm),
        in_specs=[_row_spec(tm, d), _ss_spec(ss.shape[0], tm), _gain_spec(d), _col_spec(w_gate, d, d, **once),
                  pl.BlockSpec((None, tm, pd), lambda j, i: (p[1], i, 0)), _col_spec(w_proj, pd, d, **once),
                  _gain_spec(d)],
        out_specs=rows if final else [rows, rows],
        out_shape=normed if final else [jax.ShapeDtypeStruct((m, d), F32), normed],
        scratch_shapes=[pltpu.VMEM((d, d), BF16), pltpu.VMEM((pd, d), BF16)],
        compiler_params=_params("parallel", "arbitrary"),
        name="ple",
    )(x, ss, g.reshape(1, d), _arr(w_gate), p[0], _arr(w_proj), next_gain.reshape(1, d))


def _mm_res_kernel(*refs, n_pairs, scale, with_norm):
    res_ref = refs[2 * n_pairs]
    outs = refs[2 * n_pairs + 1 + with_norm:]
    scratch = outs[2:]
    acc = None
    for t in range(n_pairs):
        _cast_weight(refs[2 * t + 1], scratch[t])
        part = jnp.dot(refs[2 * t][...], scratch[t][...], preferred_element_type=F32)
        acc = part if acc is None else acc + part
    x_new = res_ref[...] + scale * acc
    outs[0][...] = x_new
    if with_norm:
        _emit_norm(x_new, refs[2 * n_pairs + 1], outs[1])
    else:
        _tile_ss(x_new, outs[1])


def _mm_res(pairs, res, scale, tm=512, tn=512, next_gain=None):
    m, n = res.shape
    with_norm = next_gain is not None
    assert tn == n or not with_norm
    in_specs, args, scratch = [], [], []
    once = dict(pipeline_mode=pl.Buffered(1)) if n // tn <= 2 else {}
    for a, w in pairs:
        k = a.shape[1]
        in_specs += [_row_spec(tm, k), _col_spec(w, k, tn, **once)]
        args += [a, _arr(w)]
        scratch.append(pltpu.VMEM((k, tn), BF16))
    in_specs.append(_tile_spec(tm, tn))
    args.append(res)
    if with_norm:
        in_specs.append(_gain_spec(n))
        args.append(next_gain.reshape(1, n))
        second = (_tile_spec(tm, tn), jax.ShapeDtypeStruct((m, n), BF16))
    else:
        second = _ss_out(n, m, tm, tn)
    return pl.pallas_call(
        functools.partial(_mm_res_kernel, n_pairs=len(pairs), scale=scale, with_norm=with_norm),
        grid=(n // tn, m // tm),
        in_specs=in_specs,
        out_specs=[_tile_spec(tm, tn), second[0]],
        out_shape=[jax.ShapeDtypeStruct((m, n), F32), second[1]],
        scratch_shapes=scratch,
        compiler_params=_params("parallel", "arbitrary"),
        name="mm_res",
    )(*args)


def _lane_lo(shape=(1, LANES)):
    return lax.broadcasted_iota(jnp.int32, shape, len(shape) - 1) < HEAD_DIM


def _rope_tables(n):
    half = HEAD_DIM // 2
    inv_freq = ROPE_THETA ** (-jnp.arange(half, dtype=F32) / half)
    ang = jnp.arange(n, dtype=F32)[:, None] * inv_freq[None, :]
    cos, sin = jnp.cos(ang), jnp.sin(ang)
    return (jnp.concatenate([cos, cos, cos, cos], axis=-1),
            jnp.concatenate([-sin, sin, -sin, sin], axis=-1))


def _rope_tile(x, cos, sin):
    lane = lax.broadcasted_iota(jnp.int32, (1, LANES), 1)
    first_half = (lane % HEAD_DIM) < (HEAD_DIM // 2)
    partner = jnp.where(first_half, pltpu.roll(x, LANES - HEAD_DIM // 2, 1),
                        pltpu.roll(x, HEAD_DIM // 2, 1))
    return x * cos + partner * sin


def _attn_a_kernel(q_ref, k_ref, v_ref, cos_ref, sin_ref, o_ref, qs, ks, m_s, l_s, acc_s):
    n = q_ref.shape[1]
    lo = _lane_lo()
    rb = 256

    def rope_body(i, carry):
        sl = pl.ds(pl.multiple_of(i * rb, rb), rb)
        c, s = cos_ref[sl, :], sin_ref[sl, :]
        qs[sl, :] = _rope_tile(q_ref[0, sl, :], c, s) * (HEAD_DIM ** -0.5)
        ks[sl, :] = _rope_tile(k_ref[0, sl, :], c, s)
        return carry

    lax.fori_loop(0, n // rb, rope_body, 0)
    m_s[...] = jnp.full(m_s.shape, MASKED, F32)
    l_s[...] = jnp.zeros(l_s.shape, F32)
    acc_s[...] = jnp.zeros(acc_s.shape, F32)

    group = 16

    for dil in A_DILATIONS:
        m = n // dil
        qb_rows = m if m <= 2 * A_QUERY_ROWS else A_QUERY_ROWS
        span = min(qb_rows + 2 * A_HALF, m)
        nqb = m // qb_rows

        def blocks(items, dil=dil, m=m, qb_rows=qb_rows, span=span):
            q0 = [qb * qb_rows for _, qb in items]
            k0 = [jnp.clip(t - A_HALF, 0, m - span) for t in q0]
            if dil == 1:
                rows_q = [pl.ds(pl.multiple_of(t, qb_rows), qb_rows) for t in q0]
                rows_k = [pl.ds(pl.multiple_of(t, A_HALF), span) for t in k0]
            else:
                rows_q = [pl.ds(r + t * dil, qb_rows, stride=dil) for (r, _), t in zip(items, q0)]
                rows_k = [pl.ds(r + t * dil, span, stride=dil) for (r, _), t in zip(items, k0)]
            unstack = lambda t: jnp.where(lo, t[:qb_rows], t[qb_rows:])
            q2 = [jnp.concatenate([jnp.where(lo, t, 0.0), jnp.where(lo, 0.0, t)], axis=0).astype(BF16)
                  for t in (qs[rq, :] for rq in rows_q)]
            kb = [ks[rk, :].astype(BF16) for rk in rows_k]
            vb = [v_ref[0, rk, :].astype(BF16) for rk in rows_k]
            m_old = [m_s[rq, :] for rq in rows_q]
            l_old = [l_s[rq, :] for rq in rows_q]
            acc_old = [acc_s[rq, :] for rq in rows_q]
            qrow = lax.broadcasted_iota(jnp.int32, (2 * qb_rows, 1), 0) % qb_rows
            kcol = lax.broadcasted_iota(jnp.int32, (1, span), 1)
            mask = _each(lambda a, c: jnp.abs(a + qrow - c - kcol) <= A_HALF, q0, k0)
            s = _each(lambda q, kb, mk: jnp.where(mk, _dot_nt(q, kb), MASKED), q2, kb, mask)
            mn = _each(lambda mo, s: jnp.maximum(jnp.concatenate([mo[:, :1], mo[:, LANES - 1:]], axis=0),
                                                 jnp.max(s, axis=1, keepdims=True)), m_old, s)
            p = _each(lambda s, mn: jnp.exp(s - mn), s, mn)
            psum = _each(lambda p: jnp.sum(p, axis=1, keepdims=True), p)
            pv = _each(_dot, p, vb)
            m_new = _each(unstack, mn)
            alpha = _each(lambda mo, mn: jnp.exp(mo - mn), m_old, m_new)
            for rq, mnew, al, lold, ps, ao, pvi in zip(rows_q, m_new, alpha, l_old, psum, acc_old, pv):
                m_s[rq, :] = mnew
                l_s[rq, :] = al * lold + unstack(ps)
                acc_s[rq, :] = al * ao + unstack(pvi)

        per = math.gcd(group, dil * nqb)

        def group_body(it, carry, blocks=blocks, dil=dil, per=per):
            ts = [it * per + u for u in range(per)]
            blocks([(t & (dil - 1), t >> (dil.bit_length() - 1)) for t in ts])
            return carry

        lax.fori_loop(0, dil * nqb // per, group_body, 0)

    def out_body(i, carry):
        sl = pl.ds(pl.multiple_of(i * rb, rb), rb)
        o_ref[0, sl, :] = (acc_s[sl, :] / l_s[sl, :]).astype(o_ref.dtype)
        return carry

    lax.fori_loop(0, n // rb, out_body, 0)


def _attn_a(za, cos, sin):
    b, n, _ = za.shape
    npair = A_W // LANES
    blk = lambda off: pl.BlockSpec((1, n, LANES), lambda i, j, off=off: (i, 0, off + j))
    tab = pl.BlockSpec((n, LANES), lambda i, j: (0, 0))
    return pl.pallas_call(
        _attn_a_kernel,
        grid=(b, npair),
        in_specs=[blk(0), blk(npair), blk(2 * npair), tab, tab],
        out_specs=pl.BlockSpec((1, n, LANES), lambda i, j: (i, 0, j)),
        out_shape=jax.ShapeDtypeStruct((b, n, A_W), BF16),
        scratch_shapes=[pltpu.VMEM((n, LANES), F32)] * 5,
        compiler_params=_params("parallel", "parallel"),
        name="attn_dilated",
    )(za, za, za, cos, sin)


def _attn_c_kernel(sink_ref, q_ref, k_ref, v_ref, cosq_ref, sinq_ref, cos_ref, sin_ref, o_ref,
                   klo, khi, vlo, vhi):
    n = k_ref.shape[1]
    qb_rows = q_ref.shape[1]
    span = qb_rows + 2 * C_HALF
    hk = pl.program_id(1)
    qb = pl.program_id(2)
    lo = _lane_lo()

    @pl.when(qb == 0)
    def _():
        rb = 256

        def body(i, carry):
            sl = pl.ds(pl.multiple_of(i * rb, rb), rb)
            kx = _rope_tile(k_ref[0, sl, :], cos_ref[sl, :], sin_ref[sl, :])
            kw = pltpu.roll(kx, HEAD_DIM, 1)
            vx = v_ref[0, sl, :]
            vw = pltpu.roll(vx, HEAD_DIM, 1)
            first = hk == 0
            klo[sl, :] = jnp.where(lo, jnp.where(first, kx, kw), 0.0).astype(BF16)
            khi[sl, :] = jnp.where(lo, 0.0, jnp.where(first, kw, kx)).astype(BF16)
            vlo[sl, :] = jnp.where(lo, jnp.where(first, vx, vw), 0.0).astype(BF16)
            vhi[sl, :] = jnp.where(lo, 0.0, jnp.where(first, vw, vx)).astype(BF16)
            return carry

        lax.fori_loop(0, n // rb, body, 0)

    q0 = qb * qb_rows
    k0 = pl.multiple_of(jnp.clip(q0 - C_HALF, 0, n - span), C_HALF)
    rows_k = pl.ds(k0, span)
    kl, kh = klo[rows_k, :], khi[rows_k, :]
    vl, vh = vlo[rows_k, :], vhi[rows_k, :]
    qpos = q0 + lax.broadcasted_iota(jnp.int32, (qb_rows, 1), 0)
    kpos = k0 + lax.broadcasted_iota(jnp.int32, (1, span), 1)
    mask = jnp.abs(qpos - kpos) <= C_HALF
    cq, sq = cosq_ref[...], sinq_ref[...]

    ntile = q_ref.shape[2] // LANES
    qt = [_rope_tile(q_ref[0, :, j * LANES:(j + 1) * LANES], cq, sq) * (HEAD_DIM ** -0.5) for j in range(ntile)]
    qt = [t for t in qt for _ in range(2)]
    kk = [kl, kh] * ntile
    sk = [sink_ref[hk * C_GROUP + h] for h in range(2 * ntile)]
    s = _each(lambda q, k: jnp.where(mask, _dot_nt(q, k), MASKED), qt, kk)
    mx = _each(lambda s, sk: jnp.maximum(jnp.max(s, axis=1, keepdims=True), sk), s, sk)
    p = _each(lambda s, mx: jnp.exp(s - mx), s, mx)
    den = _each(lambda p, sk, mx: jnp.sum(p, axis=1, keepdims=True) + jnp.exp(sk - mx), p, sk, mx)
    for j in range(ntile):
        pv = _dot(p[2 * j], vl) + _dot(p[2 * j + 1], vh)
        o_ref[0, :, j * LANES:(j + 1) * LANES] = (pv / jnp.where(lo, den[2 * j], den[2 * j + 1])).astype(o_ref.dtype)


def _attn_c(zc, sink, cos, sin, qb_rows=128):
    b, n, _ = zc.shape
    gw = C_GROUP * HEAD_DIM
    kcol = C_W // LANES
    full = lambda off: pl.BlockSpec((1, n, LANES), lambda i, h, t, off=off: (i, 0, off))
    tabq = pl.BlockSpec((qb_rows, LANES), lambda i, h, t: (t, 0))
    tab = pl.BlockSpec((n, LANES), lambda i, h, t: (0, 0))
    return pl.pallas_call(
        _attn_c_kernel,
        grid=(b, C_KV_HEADS, n // qb_rows),
        in_specs=[pl.BlockSpec(memory_space=pltpu.SMEM),
                  pl.BlockSpec((1, qb_rows, gw), lambda i, h, t: (i, t, h)),
                  full(kcol), full(kcol + 1), tabq, tabq, tab, tab],
        out_specs=pl.BlockSpec((1, qb_rows, gw), lambda i, h, t: (i, t, h)),
        out_shape=jax.ShapeDtypeStruct((b, n, C_W), BF16),
        scratch_shapes=[pltpu.VMEM((n, LANES), BF16)] * 4,
        compiler_params=_params("parallel", "parallel", "arbitrary"),
        name="attn_gqa_sink",
    )(sink.reshape(-1), zc, zc, zc, cos, sin, cos, sin)


def _na_bias_table(rpb, rows):
    kh = min(NA_KH, rows)
    qc = jnp.arange(GRID_W)[:, None]
    kc = jnp.arange(GRID_W)[None, :]
    qc0 = jnp.clip(qc - NA_KW // 2, 0, GRID_W - NA_KW)
    valid = (kc >= qc0) & (kc < qc0 + NA_KW)
    ci = jnp.clip(kc - qc, -(NA_KW - 1), NA_KW - 1) + NA_KW - 1
    nci = 2 * NA_KW - 1
    pairs = rpb.astype(F32).reshape(D_HEADS // 2, 2, 2 * NA_KH - 1, nci)
    pairs = jnp.concatenate([pairs, jnp.full(pairs.shape[:-1] + (1,), MASKED, F32)], axis=-1)
    by_row = jnp.stack([pairs[:, :, off:off + kh] for off in range(NA_KH)], axis=1)
    cls = jnp.where(valid, ci, nci)
    onehot = (cls[None] == jnp.arange(nci + 1)[:, None, None]).astype(F32)
    bias = jnp.einsum('poekc,cqj->poeqkj', by_row, onehot, precision=lax.Precision.HIGHEST)
    return bias.reshape(D_HEADS // 2, NA_KH, 2 * GRID_W, kh * GRID_W)


def _attn_d_kernel(q_ref, k_ref, v_ref, bias_ref, o_ref):
    n = q_ref.shape[1]
    rows = n // GRID_W
    kh = min(NA_KH, rows)
    lo = _lane_lo()
    group = math.gcd(8, rows)

    def body(rg, carry):
        r = [rg * group + u for u in range(group)]
        r0 = [jnp.clip(t - NA_KH // 2, 0, rows - kh) for t in r]
        rows_q = [pl.ds(pl.multiple_of(t * GRID_W, GRID_W), GRID_W) for t in r]
        rows_k = [pl.ds(pl.multiple_of(t * GRID_W, GRID_W), kh * GRID_W) for t in r0]
        q = [q_ref[0, rq, :] * (HEAD_DIM ** -0.5) for rq in rows_q]
        q2 = _each(lambda q: jnp.concatenate([jnp.where(lo, q, 0.0), jnp.where(lo, 0.0, q)], axis=0), q)
        s = [_dot_nt(q2i, k_ref[0, rk, :]) + bias_ref[0, t0 - t + NA_KH - 1]
             for q2i, rk, t0, t in zip(q2, rows_k, r0, r)]
        p = _each(lambda s: jnp.exp(s - jnp.max(s, axis=1, keepdims=True)), s)
        pv = [_dot(pi, v_ref[0, rk, :]) / jnp.sum(pi, axis=1, keepdims=True) for pi, rk in zip(p, rows_k)]
        for rq, pvi in zip(rows_q, pv):
            o_ref[0, rq, :] = jnp.where(lo, pvi[:GRID_W], pvi[GRID_W:]).astype(o_ref.dtype)
        return carry

    lax.fori_loop(0, rows // group, body, 0)


def _attn_d(zc, bias):
    b, n, _ = zc.shape
    npair = D_W // LANES
    q_off = (C_W + 2 * C_KV_HEADS * HEAD_DIM) // LANES
    blk = lambda off: pl.BlockSpec((1, n, LANES), lambda i, j, off=off: (i, 0, off + j))
    return pl.pallas_call(
        _attn_d_kernel,
        grid=(b, npair),
        in_specs=[blk(q_off), blk(q_off + npair), blk(q_off + 2 * npair),
                  pl.BlockSpec((1,) + bias.shape[1:], lambda i, j: (j, 0, 0, 0))],
        out_specs=pl.BlockSpec((1, n, LANES), lambda i, j: (i, 0, j)),
        out_shape=jax.ShapeDtypeStruct((b, n, D_W), BF16),
        compiler_params=_params("parallel", "parallel"),
        name="attn_neighborhood",
    )(zc, zc, zc, bias)


def _rw_prep_kernel(z_ref, hp_ref, hn_ref, mup_ref, mun_ref, w0_ref, w2f_ref, w2b_ref, a0_ref,
                    a2f_ref, a2b_ref, g2_ref, r_o, k_o, v_o, cumf_o, cumb_o, af_o, ab_o, g_o):
    tm = z_ref.shape[1]
    first = pl.program_id(1) == 0
    last = pl.program_id(1) == pl.num_programs(1) - 1
    row = lax.broadcasted_iota(jnp.int32, (tm, 1), 0)
    crow = row % WKV_CHUNK

    def edges(cols):
        return (jnp.where(first, 0.0, hp_ref[0, SUBLANES - 1:SUBLANES, cols]),
                jnp.where(last, 0.0, hn_ref[0, 0:1, cols]))

    def mix(z, z_prev, z_next, cols):
        return z + mup_ref[:, cols] * (z_prev - z) + mun_ref[:, cols] * (z_next - z)

    def shifted(cols):
        z = z_ref[0, :, cols]
        edge_prev, edge_next = edges(cols)
        z_prev = jnp.where(row == 0, edge_prev, pltpu.roll(z, 1, 0))
        z_next = jnp.where(row == tm - 1, edge_next, pltpu.roll(z, tm - 1, 0))
        return mix(z, z_prev, z_next, cols)

    def store_shifted(o_ref, cs, cols):
        z = z_ref[0, :, cols]
        edge_prev, edge_next = edges(cols)
        o_ref[0, :, cs] = mix(z, pltpu.roll(z, 1, 0), pltpu.roll(z, tm - 1, 0), cols)
        o_ref[0, 0:1, cs] = mix(z[0:1], edge_prev, z[1:2], cols)
        o_ref[0, tm - 1:tm, cs] = mix(z[tm - 1:tm], z[tm - 2:tm - 1], edge_next, cols)

    def log_decay(x):
        return -RW_DECAY_SCALE * jax.nn.sigmoid(x)

    def chunk_cumsum(x, rev):
        s = 1
        while s < WKV_CHUNK:
            if rev:
                x = x + jnp.where(crow < WKV_CHUNK - s, pltpu.roll(x, tm - s, 0), 0.0)
            else:
                x = x + jnp.where(crow >= s, pltpu.roll(x, s, 0), 0.0)
            s *= 2
        return x

    base = 3 * B_W
    wl = jnp.tanh(shifted(slice(base, base + LANES))).astype(BF16)
    al = shifted(slice(base + LANES, base + 2 * LANES)).astype(BF16)
    gl = jax.nn.sigmoid(shifted(slice(base + 2 * LANES, base + 4 * LANES))).astype(BF16)

    for s0 in range(0, B_W, LANES):
        cs = slice(s0, s0 + LANES)
        store_shifted(r_o, cs, cs)
        store_shifted(k_o, cs, slice(B_W + s0, B_W + s0 + LANES))
        store_shifted(v_o, cs, slice(2 * B_W + s0, 2 * B_W + s0 + LANES))
        cumf_o[0, :, cs] = chunk_cumsum(log_decay(w0_ref[0:1, cs] + _dot(wl, w2f_ref[:, cs])), False)
        cumb_o[0, :, cs] = chunk_cumsum(log_decay(w0_ref[1:2, cs] + _dot(wl, w2b_ref[:, cs])), True)
        af_o[0, :, cs] = jax.nn.sigmoid(a0_ref[0:1, cs] + _dot(al, a2f_ref[:, cs]))
        ab_o[0, :, cs] = jax.nn.sigmoid(a0_ref[1:2, cs] + _dot(al, a2b_ref[:, cs]))
        g_o[0, :, cs] = _dot(gl, g2_ref[:, cs])


def _rw_prep(zb, mu_prev, mu_next, w0, w2, a0, a2, g2, tm=256):
    b, n, cols = zb.shape
    tm = min(tm, n)
    per = tm // SUBLANES
    pad_cols = lambda t: jnp.pad(t, (0, cols - t.shape[0])).reshape(1, cols)
    rows_f = lambda t: jnp.pad(t, ((0, LANES - RW_LORA), (0, 0))).astype(BF16)
    rows_b = lambda t: jnp.pad(t, ((RW_LORA, LANES - 2 * RW_LORA), (0, 0))).astype(BF16)
    g2p = jnp.pad(g2, ((0, 2 * LANES - RW_GATE_LORA), (0, 0))).astype(BF16)
    const = lambda shape: pl.BlockSpec(shape, lambda i, t: (0,) * len(shape))
    halo_prev = pl.BlockSpec((1, SUBLANES, cols), lambda i, t: (i, jnp.maximum(t * per - 1, 0), 0))
    halo_next = pl.BlockSpec((1, SUBLANES, cols), lambda i, t: (i, jnp.minimum((t + 1) * per, n // SUBLANES - 1), 0))
    tok = jax.ShapeDtypeStruct((b, n, B_W), F32)
    return pl.pallas_call(
        _rw_prep_kernel,
        grid=(b, n // tm),
        in_specs=[pl.BlockSpec((1, tm, cols), lambda i, t: (i, t, 0)), halo_prev, halo_next,
                  const((1, cols)), const((1, cols)), const((2, B_W)), const((LANES, B_W)),
                  const((LANES, B_W)), const((2, B_W)), const((LANES, B_W)), const((LANES, B_W)),
                  const((2 * LANES, B_W))],
        out_specs=[pl.BlockSpec((1, tm, B_W), lambda i, t: (i, t, 0))] * 8,
        out_shape=[tok] * 8,
        compiler_params=_params("parallel", "parallel"),
        name="rwkv_prep",
    )(zb, zb, zb, pad_cols(mu_prev), pad_cols(mu_next), w0, rows_f(w2[0]), rows_b(w2[1]),
      a0, rows_f(a2[0]), rows_b(a2[1]), g2p)


def _head_sum(x, lo):
    return jnp.where(lo, jnp.sum(jnp.where(lo, x, 0.0), axis=1, keepdims=True),
                     jnp.sum(jnp.where(lo, 0.0, x), axis=1, keepdims=True))


def _wkv_chunks(chains):
    c = chains[0][0].shape[0]
    ii = lax.broadcasted_iota(jnp.int32, (c, c), 0)
    jj = lax.broadcasted_iota(jnp.int32, (c, c), 1)
    row = lax.broadcasted_iota(jnp.int32, (c, 1), 0)
    eye_f = (ii == jj).astype(F32)
    masks = {False: (jj <= ii, jj < ii), True: (jj >= ii, jj > ii)}
    lo = _lane_lo()
    ki = lax.broadcasted_iota(jnp.int32, (LANES, LANES), 0)
    vj = lax.broadcasted_iota(jnp.int32, (LANES, LANES), 1)
    same_head = (ki < HEAD_DIM) == (vj < HEAD_DIM)
    diag = ki == vj
    r, k, v, cum, rate, st, k_k, k_a, rev = (list(col) for col in zip(*chains))
    twice = lambda xs: [x for x in xs for _ in range(2)]
    merge = lambda xs: [jnp.where(lo, xs[2 * i], xs[2 * i + 1]) for i in range(len(xs) // 2)]
    incl = twice([masks[x][0] for x in rev])
    strict = twice([masks[x][1] for x in rev])

    kk = _each(lambda k, k_k: k * k_k, k, k_k)
    kk = _each(lambda t: t / jnp.maximum(jnp.sqrt(_head_sum(t * t, lo)), 1e-12), kk)
    kd = _each(lambda k, rate, k_a: k * (1.0 + (rate - 1.0) * k_a), k, rate, k_a)
    b = _each(lambda kk, rate: kk * rate, kk, rate)

    tot = _each(lambda cum, x: cum[0:1] if x else cum[c - 1:c], cum, rev)
    excl = _each(lambda cum, x: jnp.where(row == c - 1, 0.0, pltpu.roll(cum, c - 1, 0)) if x
                 else jnp.where(row == 0, 0.0, pltpu.roll(cum, 1, 0)), cum, rev)
    ar = _each(lambda kk, r, cum, ex: jnp.concatenate([-kk * jnp.exp(ex), r * jnp.exp(cum)], axis=0),
               kk, r, cum, excl)
    ar_h = [jnp.where(m, x, 0.0).astype(BF16) for x in ar for m in (lo, ~lo)]
    ar = _each(lambda t: t.astype(BF16), ar)
    v = _each(lambda t: t.astype(BF16), v)
    g_inv = _each(lambda cum: jnp.exp(-cum), cum)
    bg = twice(_each(lambda b, g: (b * g).astype(BF16), b, g_inv))
    kg = twice(_each(lambda kd, g: (kd * g).astype(BF16), kd, g_inv))
    x1 = _each(_dot_nt, ar_h, bg)
    x2 = _each(_dot_nt, ar_h, kg)
    a_ab = _each(lambda x, m: jnp.where(m, x[:c], 0.0), x1, strict)
    a_rb = _each(lambda x, m: jnp.where(m, x[c:], 0.0).astype(BF16), x1, incl)
    a_k = _each(lambda x, ms, mi: jnp.concatenate([jnp.where(ms, x[:c], 0.0), jnp.where(mi, x[c:], 0.0)],
                                                  axis=0).astype(BF16), x2, strict, incl)

    inv = _each(lambda t: eye_f + t, a_ab)
    pw = _each(lambda t: _dot(t, t), a_ab)
    for _ in range(c.bit_length() - 3):
        pw = _each(lambda t: t.astype(BF16), pw)
        both = _each(lambda inv, pw: _dot(jnp.concatenate([inv.astype(BF16), pw], axis=0), pw), inv, pw)
        inv = _each(lambda inv, t: inv + t[:c], inv, both)
        pw = _each(lambda t: t[c:], both)
    inv = _each(lambda inv, pw: (inv + _dot(inv, pw)).astype(BF16), inv, pw)

    akv = merge(_each(_dot, a_k, twice(v)))
    ars = _each(_dot, ar, st)
    rhs = _each(lambda ars, akv: (ars[:c] + akv[:c]).astype(BF16), ars, akv)
    p = _each(lambda t: t.astype(BF16), merge(_each(_dot, inv, twice(rhs))))
    y = _each(lambda ars, arb, akv: ars[c:] + arb + akv[c:], ars, merge(_each(_dot, a_rb, twice(p))), akv)
    g_end = _each(lambda tot, cum: jnp.exp(tot - cum), tot, cum)
    upd = _each(lambda b, kd, g, p, v: _dot_tn(jnp.concatenate([b * g, kd * g], axis=0),
                                               jnp.concatenate([p, v], axis=0)), b, kd, g_end, p, v)
    g_col = _each(lambda tot: jnp.sum(jnp.where(diag, jnp.exp(tot), 0.0), axis=1, keepdims=True), tot)
    new_st = _each(lambda g, st, upd: jnp.where(same_head, g * st + upd, 0.0), g_col, st, upd)
    return list(zip(y, new_st))


def _rw_scan_kernel(rf, kf, vf, cumf, af, rb, kb, vb, cumb, ab, kk_ref, ka_ref, yf, yb, sf, sb):
    tb = rf.shape[1]
    npair = rf.shape[2] // LANES
    c = WKV_CHUNK
    nch = tb // c
    pair = lambda j: slice(j * LANES, (j + 1) * LANES)

    @pl.when(pl.program_id(2) == 0)
    def _():
        sf[...] = jnp.zeros(sf.shape, F32)
        sb[...] = jnp.zeros(sb.shape, F32)

    def body(ci, carry):
        rows_f = pl.ds(pl.multiple_of(ci * c, c), c)
        rows_b = pl.ds(pl.multiple_of((nch - 1 - ci) * c, c), c)
        chains = []
        for j in range(npair):
            par = (kk_ref[:, pair(j)], ka_ref[:, pair(j)])
            chains.append(tuple(ref[0, rows_f, pair(j)] for ref in (rf, kf, vf, cumf, af)) + (sf[j],) + par + (False,))
            chains.append(tuple(ref[0, rows_b, pair(j)] for ref in (rb, kb, vb, cumb, ab)) + (sb[j],) + par + (True,))
        outs = _wkv_chunks(chains)
        for j in range(npair):
            yf[0, rows_f, pair(j)], sf[j] = outs[2 * j]
            yb[0, rows_b, pair(j)], sb[j] = outs[2 * j + 1]
        return carry

    lax.fori_loop(0, nch, body, 0)


def _rw_scan(r, k, v, cumf, cumb, af, ab, k_k, k_a, hb=20, tb=256):
    b, n, width = r.shape
    tb = min(tb, n)
    nt = n // tb
    hw = hb * HEAD_DIM
    fwd = pl.BlockSpec((1, tb, hw), lambda i, g, t: (i, t, g))
    bwd = pl.BlockSpec((1, tb, hw), lambda i, g, t: (i, nt - 1 - t, g))
    par = pl.BlockSpec((1, hw), lambda i, g, t: (0, g))
    out = jax.ShapeDtypeStruct((b, n, width), F32)
    return pl.pallas_call(
        _rw_scan_kernel,
        grid=(b, width // hw, nt),
        in_specs=[fwd] * 5 + [bwd] * 5 + [par, par],
        out_specs=[fwd, bwd],
        out_shape=[out, out],
        scratch_shapes=[pltpu.VMEM((hw // LANES, LANES, LANES), F32)] * 2,
        compiler_params=_params("parallel", "parallel", "arbitrary"),
        name="rwkv_scan",
    )(r, k, v, cumf, af, r, k, v, cumb, ab, k_k.reshape(1, width), k_a.reshape(1, width))


def _rw_post_kernel(yf, yb, r, k, v, af, ab, g, ka_ref, rk_ref, lnw_ref, lnb_ref, o_ref):
    lo = _lane_lo()
    head_sum = lambda x: _head_sum(x, lo)
    for j in range(o_ref.shape[2] // LANES):
        sl = slice(j * LANES, (j + 1) * LANES)
        y = yf[0, :, sl] + yb[0, :, sl]
        dev = y - head_sum(y) * (1.0 / HEAD_DIM)
        var = head_sum(dev * dev) * (1.0 / HEAD_DIM)
        yn = dev * lax.rsqrt(var + RW_GN_EPS) * lnw_ref[:, sl] + lnb_ref[:, sl]
        kt, k_a = k[0, :, sl], ka_ref[:, sl]
        kd = kt * (1.0 + (af[0, :, sl] - 1.0) * k_a) + kt * (1.0 + (ab[0, :, sl] - 1.0) * k_a)
        bonus = head_sum(r[0, :, sl] * kd * rk_ref[:, sl]) * v[0, :, sl]
        o_ref[0, :, sl] = ((yn + bonus) * g[0, :, sl]).astype(o_ref.dtype)


def _rw_post(yf, yb, r, k, v, af, ab, g, k_a, r_k, ln_w, ln_b, tm=256):
    b, n, width = r.shape
    tm = min(tm, n)
    tok = pl.BlockSpec((1, tm, width), lambda i, t: (i, t, 0))
    par = pl.BlockSpec((1, width), lambda i, t: (0, 0))
    return pl.pallas_call(
        _rw_post_kernel,
        grid=(b, n // tm),
        in_specs=[tok] * 8 + [par] * 4,
        out_specs=tok,
        out_shape=jax.ShapeDtypeStruct((b, n, width), BF16),
        compiler_params=_params("parallel", "parallel"),
        name="rwkv_post",
    )(yf, yb, r, k, v, af, ab, g, k_a.reshape(1, width), r_k.reshape(1, width),
      ln_w.reshape(1, width), ln_b.reshape(1, width))


def _rwkv7_bidir(zb, mu_prev, mu_next, w0, w2, a0, a2, g2, k_k, k_a, r_k, ln_w, ln_b):
    r, k, v, cumf, cumb, af, ab, g = _rw_prep(zb, mu_prev, mu_next, w0, w2, a0, a2, g2)
    yf, yb = _rw_scan(r, k, v, cumf, cumb, af, ab, k_k, k_a)
    return _rw_post(yf, yb, r, k, v, af, ab, g, k_a, r_k, ln_w, ln_b)


def _ffn_half_step(x, xn, w1, w3, w2, layer):
    h = _ffn_up(xn, (w1, layer), (w3, layer))
    return _mm_res([(h, (w2, layer))], x, 0.5, tm=256, tn=1024)


def _mix_ab(x, ss, g, next_gain, b, n, cos, sin, w_in, layer, w_out, mu_prev, mu_next, w0, w2, a0, a2, g2,
            k_k, k_a, r_k, ln_w, ln_b):
    qkv = 3 * A_W
    w_b = jnp.pad(w_in[layer, :, qkv:], ((0, 0), (0, RW_PAD_COLS - (w_in.shape[2] - qkv))))
    za = _proj(x, ss, g, w_in[layer, :, :qkv], qkv, tn=qkv).reshape(b, n, qkv)
    zb = _proj(x, ss, g, w_b, RW_PAD_COLS, tn=RW_PAD_COLS // 2).reshape(b, n, RW_PAD_COLS)
    oa = _attn_a(za, cos, sin).reshape(b * n, A_W)
    ob = _rwkv7_bidir(zb, mu_prev, mu_next, w0, w2, a0, a2, g2, k_k, k_a, r_k, ln_w, ln_b)
    return _mm_res([(oa, w_out[:A_W]), (ob.reshape(b * n, B_W), w_out[A_W:])], x, 1.0, tn=x.shape[1],
                   next_gain=next_gain)


def _mix_cd(x, ss, g, next_gain, b, n, cos, sin, w_in, layer, w_out, sink, rpb):
    cols = w_in.shape[2]
    zc = _proj(x, ss, g, (w_in, layer), cols, tn=cols // 2).reshape(b, n, cols)
    oc = _attn_c(zc, sink, cos, sin).reshape(b * n, C_W)
    od = _attn_d(zc, _na_bias_table(rpb, n // GRID_W)).reshape(b * n, D_W)
    return _mm_res([(oc, (w_out, layer, 0)), (od, (w_out, layer, 1))], x, 1.0, tn=x.shape[1],
                   next_gain=next_gain)


def kernel(x, p, ffn1_norm, ffn1_w1, ffn1_w3, ffn1_w2, mix_norm, ffn2_norm, ffn2_w1, ffn2_w3, ffn2_w2, ple_norm, ple_w_gate, ple_w_proj, ab_w_in, ab_w_out, rw_mu_prev, rw_mu_next, rw_w0, rw_w2, rw_a0, rw_a2, rw_g2, rw_k_k, rw_k_a, rw_r_k, rw_ln_w, rw_ln_b, cd_w_in, cd_w_out, c_sink, d_rpb, final_norm):
    b, n, d = x.shape
    depth = p.shape[0]
    cos, sin = _rope_tables(n)
    x = x.reshape(b * n, d)
    p = p.reshape(depth, b * n, -1)
    xn = _rmsnorm(x, ffn1_norm[0], BF16)
    for i in range(depth):
        j = i // 2
        x, ss = _ffn_half_step(x, xn, ffn1_w1, ffn1_w3, ffn1_w2, i)
        if i % 2 == 0:
            x, xn = _mix_ab(x, ss, mix_norm[i], ffn2_norm[i], b, n, cos, sin, ab_w_in, j, ab_w_out[j],
                            rw_mu_prev[j], rw_mu_next[j], rw_w0[j], rw_w2[j], rw_a0[j], rw_a2[j], rw_g2[j],
                            rw_k_k[j], rw_k_a[j], rw_r_k[j], rw_ln_w[j], rw_ln_b[j])
        else:
            x, xn = _mix_cd(x, ss, mix_norm[i], ffn2_norm[i], b, n, cos, sin, cd_w_in, j, cd_w_out,
                            c_sink[j], d_rpb[j])
        x, ss = _ffn_half_step(x, xn, ffn2_w1, ffn2_w3, ffn2_w2, i)
        ple_args = (x, ss, ple_norm[i], (ple_w_gate, i), (p, i), (ple_w_proj, i))
        if i + 1 < depth:
            x, xn = _ple(*ple_args, ffn1_norm[i + 1], False)
    return _ple(*ple_args, final_norm, True).reshape(b, n, d)
```

```python
import functools
import math

import jax
import jax.numpy as jnp
from jax import lax
from jax.experimental import pallas as pl
from jax.experimental.pallas import tpu as pltpu

F32 = jnp.float32
BF16 = jnp.bfloat16

HEAD_DIM = 64
LANES = 128
SUBLANES = 8
NORM_EPS = 1e-6
ROPE_THETA = 10000.0
GRID_W = 64
A_HEADS = 12
A_W = A_HEADS * HEAD_DIM
A_HALF = 64
A_DILATIONS = (1, 4, 16)
A_QUERY_ROWS = 64
B_HEADS = 20
B_W = B_HEADS * HEAD_DIM
RW_LORA = 64
RW_GATE_LORA = 192
RW_GN_EPS = 64e-5
RW_DECAY_SCALE = math.exp(-0.5)
RW_PAD_COLS = 3 * B_W + 4 * LANES
C_HEADS = 16
C_KV_HEADS = 2
C_GROUP = C_HEADS // C_KV_HEADS
C_W = C_HEADS * HEAD_DIM
C_HALF = 128
D_HEADS = 16
D_W = D_HEADS * HEAD_DIM
NA_KH = 8
NA_KW = 16
MASKED = -1e30
WKV_CHUNK = 64
VMEM_LIMIT = 56 * 1024 * 1024


def _params(*sem):
    return pltpu.CompilerParams(dimension_semantics=sem, vmem_limit_bytes=VMEM_LIMIT)


def _dot(a, b):
    return jnp.dot(a.astype(BF16), b.astype(BF16), preferred_element_type=F32)


def _dot_nt(a, b):
    return lax.dot_general(a.astype(BF16), b.astype(BF16), (((1,), (1,)), ((), ())),
                           preferred_element_type=F32)


def _dot_tn(a, b):
    return lax.dot_general(a.astype(BF16), b.astype(BF16), (((0,), (0,)), ((), ())),
                           preferred_element_type=F32)


def _each(f, *cols):
    return [f(*xs) for xs in zip(*cols)]


def _rmsnorm_kernel(x_ref, g_ref, o_ref):
    x = x_ref[...]
    ms = jnp.mean(x * x, axis=-1, keepdims=True)
    o_ref[...] = (x * lax.rsqrt(ms + NORM_EPS) * g_ref[...]).astype(o_ref.dtype)


def _rmsnorm(x, g, dtype, tm=512):
    m, d = x.shape
    return pl.pallas_call(
        _rmsnorm_kernel,
        grid=(m // tm,),
        in_specs=[pl.BlockSpec((tm, d), lambda i: (i, 0)),
                  pl.BlockSpec((1, d), lambda i: (0, 0))],
        out_specs=pl.BlockSpec((tm, d), lambda i: (i, 0)),
        out_shape=jax.ShapeDtypeStruct((m, d), dtype),
        compiler_params=_params("parallel"),
        name="rmsnorm",
    )(x, g.reshape(1, d))


def _cast_weight(w_ref, wb_ref):
    rows = 256

    @pl.when(pl.program_id(1) == 0)
    def _():
        def body(i, carry):
            sl = pl.ds(pl.multiple_of(i * rows, rows), rows)
            wb_ref[sl, :] = w_ref[sl, :].astype(BF16)
            return carry

        lax.fori_loop(0, w_ref.shape[0] // rows, body, 0)


def _row_spec(tm, k, **kw):
    return pl.BlockSpec((tm, k), lambda j, i: (i, 0), **kw)


def _col_spec(w, k, tn, **kw):
    if isinstance(w, tuple):
        layer, row_block = w[1], (w[2] if len(w) > 2 else 0)
        return pl.BlockSpec((None, k, tn), lambda j, i: (layer, row_block, j), **kw)
    return pl.BlockSpec((k, tn), lambda j, i: (0, j), **kw)


def _arr(w):
    return w[0] if isinstance(w, tuple) else w


def _tile_spec(tm, tn):
    return pl.BlockSpec((tm, tn), lambda j, i: (i, j))


def _normed(x_ref, ss_ref, g_ref):
    ss = ss_ref[0, :, :1]
    for t in range(1, ss_ref.shape[0]):
        ss = ss + ss_ref[t, :, :1]
    rs = lax.rsqrt(ss * (1.0 / x_ref.shape[1]) + NORM_EPS)
    return (x_ref[...] * rs * g_ref[...]).astype(BF16)


def _ss_spec(nt, tm):
    return pl.BlockSpec((nt, tm, LANES), lambda j, i: (0, i, 0))


def _gain_spec(d):
    return pl.BlockSpec((1, d), lambda j, i: (0, 0))


def _ffn_up_kernel(xn_ref, w1_ref, w3_ref, o_ref, w1b, w3b):
    _cast_weight(w1_ref, w1b)
    _cast_weight(w3_ref, w3b)
    xn = xn_ref[...]
    a = jnp.dot(xn, w1b[...], preferred_element_type=F32)
    b = jnp.dot(xn, w3b[...], preferred_element_type=F32)
    o_ref[...] = (a * jax.nn.sigmoid(a) * b).astype(o_ref.dtype)


def _ffn_up(xn, w1, w3, tm=1024, tn=512):
    m, d = xn.shape
    f = _arr(w1).shape[-1]
    return pl.pallas_call(
        _ffn_up_kernel,
        grid=(f // tn, m // tm),
        in_specs=[_row_spec(tm, d), _col_spec(w1, d, tn), _col_spec(w3, d, tn)],
        out_specs=_tile_spec(tm, tn),
        out_shape=jax.ShapeDtypeStruct((m, f), BF16),
        scratch_shapes=[pltpu.VMEM((d, tn), BF16)] * 2,
        compiler_params=_params("parallel", "arbitrary"),
        name="ffn_up",
    )(xn, _arr(w1), _arr(w3))


def _proj_kernel(x_ref, ss_ref, g_ref, w_ref, o_ref, wb):
    _cast_weight(w_ref, wb)
    o_ref[...] = jnp.dot(_normed(x_ref, ss_ref, g_ref), wb[...], preferred_element_type=F32)


def _proj(x, ss, g, w, ncols, tn, tm=512):
    m, d = x.shape
    return pl.pallas_call(
        _proj_kernel,
        grid=(ncols // tn, m // tm),
        in_specs=[_row_spec(tm, d), _ss_spec(ss.shape[0], tm), _gain_spec(d),
                  _col_spec(w, d, tn, pipeline_mode=pl.Buffered(1))],
        out_specs=_tile_spec(tm, tn),
        out_shape=jax.ShapeDtypeStruct((m, ncols), F32),
        scratch_shapes=[pltpu.VMEM((d, tn), BF16)],
        compiler_params=_params("parallel", "arbitrary"),
        name="proj",
    )(x, ss, g.reshape(1, d), _arr(w))


def _tile_ss(x_new, ss_ref):
    ss_ref[0] = jnp.broadcast_to(jnp.sum(x_new * x_new, axis=1, keepdims=True), ss_ref.shape[1:])


def _ss_out(n, m, tm, tn):
    return (pl.BlockSpec((1, tm, LANES), lambda j, i: (j, i, 0)),
            jax.ShapeDtypeStruct((n // tn, m, LANES), F32))


def _emit_norm(x_new, g_ref, o_ref):
    ms = jnp.mean(x_new * x_new, axis=-1, keepdims=True)
    o_ref[...] = (x_new * lax.rsqrt(ms + NORM_EPS) * g_ref[...]).astype(o_ref.dtype)


def _ple_kernel(x_ref, ss_ref, g_ref, w_ref, p_ref, wp_ref, gn_ref, *refs, final):
    wb, wpb = refs[-2:]
    _cast_weight(w_ref, wb)
    _cast_weight(wp_ref, wpb)
    gate = jax.nn.sigmoid(jnp.dot(_normed(x_ref, ss_ref, g_ref), wb[...], preferred_element_type=F32))
    emb = jnp.dot(p_ref[...].astype(BF16), wpb[...], preferred_element_type=F32)
    x_new = x_ref[...] + emb * gate
    if not final:
        refs[0][...] = x_new
    _emit_norm(x_new, gn_ref, refs[-3])


def _ple(x, ss, g, w_gate, p, w_proj, next_gain, final, tm=512):
    m, d = x.shape
    pd = p[0].shape[-1]
    once = dict(pipeline_mode=pl.Buffered(1))
    rows = _tile_spec(tm, d)
    normed = jax.ShapeDtypeStruct((m, d), F32 if final else BF16)
    return pl.pallas_call(
        functools.partial(_ple_kernel, final=final),
        grid=(1, m // tm),
        in_specs=[_row_spec(tm, d), _ss_spec(ss.shape[0], tm), _gain_spec(d), _col_spec(w_gate, d, d, **once),
                  pl.BlockSpec((None, tm, pd), lambda j, i: (p[1], i, 0)), _col_spec(w_proj, pd, d, **once),
                  _gain_spec(d)],
        out_specs=rows if final else [rows, rows],
        out_shape=normed if final else [jax.ShapeDtypeStruct((m, d), F32), normed],
        scratch_shapes=[pltpu.VMEM((d, d), BF16), pltpu.VMEM((pd, d), BF16)],
        compiler_params=_params("parallel", "arbitrary"),
        name="ple",
    )(x, ss, g.reshape(1, d), _arr(w_gate), p[0], _arr(w_proj), next_gain.reshape(1, d))


def _mm_res_kernel(*refs, n_pairs, scale, with_norm):
    res_ref = refs[2 * n_pairs]
    outs = refs[2 * n_pairs + 1 + with_norm:]
    scratch = outs[2:]
    acc = None
    for t in range(n_pairs):
        _cast_weight(refs[2 * t + 1], scratch[t])
        part = jnp.dot(refs[2 * t][...], scratch[t][...], preferred_element_type=F32)
        acc = part if acc is None else acc + part
    x_new = res_ref[...] + scale * acc
    outs[0][...] = x_new
    if with_norm:
        _emit_norm(x_new, refs[2 * n_pairs + 1], outs[1])
    else:
        _tile_ss(x_new, outs[1])


def _mm_res(pairs, res, scale, tm=512, tn=512, next_gain=None):
    m, n = res.shape
    with_norm = next_gain is not None
    assert tn == n or not with_norm
    in_specs, args, scratch = [], [], []
    once = dict(pipeline_mode=pl.Buffered(1)) if n // tn <= 2 else {}
    for a, w in pairs:
        k = a.shape[1]
        in_specs += [_row_spec(tm, k), _col_spec(w, k, tn, **once)]
        args += [a, _arr(w)]
        scratch.append(pltpu.VMEM((k, tn), BF16))
    in_specs.append(_tile_spec(tm, tn))
    args.append(res)
    if with_norm:
        in_specs.append(_gain_spec(n))
        args.append(next_gain.reshape(1, n))
        second = (_tile_spec(tm, tn), jax.ShapeDtypeStruct((m, n), BF16))
    else:
        second = _ss_out(n, m, tm, tn)
    return pl.pallas_call(
        functools.partial(_mm_res_kernel, n_pairs=len(pairs), scale=scale, with_norm=with_norm),
        grid=(n // tn, m // tm),
        in_specs=in_specs,
        out_specs=[_tile_spec(tm, tn), second[0]],
        out_shape=[jax.ShapeDtypeStruct((m, n), F32), second[1]],
        scratch_shapes=scratch,
        compiler_params=_params("parallel", "arbitrary"),
        name="mm_res",
    )(*args)


def _lane_lo(shape=(1, LANES)):
    return lax.broadcasted_iota(jnp.int32, shape, len(shape) - 1) < HEAD_DIM


def _rope_tables(n):
    half = HEAD_DIM // 2
    inv_freq = ROPE_THETA ** (-jnp.arange(half, dtype=F32) / half)
    ang = jnp.arange(n, dtype=F32)[:, None] * inv_freq[None, :]
    cos, sin = jnp.cos(ang), jnp.sin(ang)
    return (jnp.concatenate([cos, cos, cos, cos], axis=-1),
            jnp.concatenate([-sin, sin, -sin, sin], axis=-1))


def _rope_tile(x, cos, sin):
    lane = lax.broadcasted_iota(jnp.int32, (1, LANES), 1)
    first_half = (lane % HEAD_DIM) < (HEAD_DIM // 2)
    partner = jnp.where(first_half, pltpu.roll(x, LANES - HEAD_DIM // 2, 1),
                        pltpu.roll(x, HEAD_DIM // 2, 1))
    return x * cos + partner * sin


def _attn_a_kernel(q_ref, k_ref, v_ref, cos_ref, sin_ref, o_ref, qs, ks, m_s, l_s, acc_s):
    n = q_ref.shape[1]
    lo = _lane_lo()
    rb = 256

    def rope_body(i, carry):
        sl = pl.ds(pl.multiple_of(i * rb, rb), rb)
        c, s = cos_ref[sl, :], sin_ref[sl, :]
        qs[sl, :] = _rope_tile(q_ref[0, sl, :], c, s) * (HEAD_DIM ** -0.5)
        ks[sl, :] = _rope_tile(k_ref[0, sl, :], c, s)
        return carry

    lax.fori_loop(0, n // rb, rope_body, 0)

    group = 32

    for dil in sorted(A_DILATIONS, reverse=True):
        first = dil == max(A_DILATIONS)
        m = n // dil
        qb_rows = m if m <= 2 * A_QUERY_ROWS else A_QUERY_ROWS
        span = min(qb_rows + 2 * A_HALF, m)
        nqb = m // qb_rows

        def blocks(items, dil=dil, m=m, qb_rows=qb_rows, span=span, first=first):
            q0 = [qb * qb_rows for _, qb in items]
            k0 = [jnp.clip(t - A_HALF, 0, m - span) for t in q0]
            if dil == 1:
                rows_q = [pl.ds(pl.multiple_of(t, qb_rows), qb_rows) for t in q0]
                rows_k = [pl.ds(pl.multiple_of(t, A_HALF), span) for t in k0]
            else:
                rows_q = [pl.ds(r + t * dil, qb_rows, stride=dil) for (r, _), t in zip(items, q0)]
                rows_k = [pl.ds(r + t * dil, span, stride=dil) for (r, _), t in zip(items, k0)]
            unstack = lambda t: jnp.where(lo, t[:qb_rows], t[qb_rows:])
            q2 = [jnp.concatenate([jnp.where(lo, t, 0.0), jnp.where(lo, 0.0, t)], axis=0).astype(BF16)
                  for t in (qs[rq, :] for rq in rows_q)]
            kb = [ks[rk, :].astype(BF16) for rk in rows_k]
            vb = [v_ref[0, rk, :].astype(BF16) for rk in rows_k]
            qrow = lax.broadcasted_iota(jnp.int32, (2 * qb_rows, 1), 0) % qb_rows
            kcol = lax.broadcasted_iota(jnp.int32, (1, span), 1)
            mask = _each(lambda a, c: jnp.abs(a + qrow - c - kcol) <= A_HALF, q0, k0)
            s = _each(lambda q, kb, mk: jnp.where(mk, _dot_nt(q, kb), MASKED), q2, kb, mask)
            mn = _each(lambda s: jnp.max(s, axis=1, keepdims=True), s)
            if not first:
                m_old = [m_s[rq, :] for rq in rows_q]
                l_old = [l_s[rq, :] for rq in rows_q]
                acc_old = [acc_s[rq, :] for rq in rows_q]
                mn = _each(lambda mo, t: jnp.maximum(jnp.concatenate([mo[:, :1], mo[:, LANES - 1:]], axis=0), t),
                           m_old, mn)
            p = _each(lambda s, mn: jnp.exp(s - mn), s, mn)
            psum = _each(lambda p: unstack(jnp.sum(p, axis=1, keepdims=True)), p)
            pv = _each(lambda p, vb: unstack(_dot(p, vb)), p, vb)
            m_new = _each(unstack, mn)
            if not first:
                alpha = _each(lambda mo, mn: jnp.exp(mo - mn), m_old, m_new)
                psum = _each(lambda al, lold, ps: al * lold + ps, alpha, l_old, psum)
                pv = _each(lambda al, ao, pvi: al * ao + pvi, alpha, acc_old, pv)
            for rq, mnew, ps, pvi in zip(rows_q, m_new, psum, pv):
                m_s[rq, :] = mnew
                l_s[rq, :] = jnp.broadcast_to(ps, mnew.shape)
                acc_s[rq, :] = pvi

        per = math.gcd(group, dil * nqb)

        def group_body(it, carry, blocks=blocks, dil=dil, per=per):
            ts = [it * per + u for u in range(per)]
            blocks([(t & (dil - 1), t >> (dil.bit_length() - 1)) for t in ts])
            return carry

        lax.fori_loop(0, dil * nqb // per, group_body, 0)

    def out_body(i, carry):
        sl = pl.ds(pl.multiple_of(i * rb, rb), rb)
        o_ref[0, sl, :] = (acc_s[sl, :] / l_s[sl, :]).astype(o_ref.dtype)
        return carry

    lax.fori_loop(0, n // rb, out_body, 0)


def _attn_a(za, cos, sin):
    b, n, _ = za.shape
    npair = A_W // LANES
    blk = lambda off: pl.BlockSpec((1, n, LANES), lambda i, j, off=off: (i, 0, off + j))
    tab = pl.BlockSpec((n, LANES), lambda i, j: (0, 0))
    return pl.pallas_call(
        _attn_a_kernel,
        grid=(b, npair),
        in_specs=[blk(0), blk(npair), blk(2 * npair), tab, tab],
        out_specs=pl.BlockSpec((1, n, LANES), lambda i, j: (i, 0, j)),
        out_shape=jax.ShapeDtypeStruct((b, n, A_W), BF16),
        scratch_shapes=[pltpu.VMEM((n, LANES), F32)] * 5,
        compiler_params=_params("parallel", "parallel"),
        name="attn_dilated",
    )(za, za, za, cos, sin)


def _attn_c_kernel(sink_ref, q_ref, k_ref, v_ref, cosq_ref, sinq_ref, cos_ref, sin_ref, o_ref,
                   klo, khi, vlo, vhi):
    n = k_ref.shape[1]
    qb_rows = q_ref.shape[1]
    span = qb_rows + 2 * C_HALF
    hk = pl.program_id(1)
    qb = pl.program_id(2)
    lo = _lane_lo()

    @pl.when(qb == 0)
    def _():
        rb = 256

        def body(i, carry):
            sl = pl.ds(pl.multiple_of(i * rb, rb), rb)
            kx = _rope_tile(k_ref[0, sl, :], cos_ref[sl, :], sin_ref[sl, :])
            kw = pltpu.roll(kx, HEAD_DIM, 1)
            vx = v_ref[0, sl, :]
            vw = pltpu.roll(vx, HEAD_DIM, 1)
            first = hk == 0
            klo[sl, :] = jnp.where(lo, jnp.where(first, kx, kw), 0.0).astype(BF16)
            khi[sl, :] = jnp.where(lo, 0.0, jnp.where(first, kw, kx)).astype(BF16)
            vlo[sl, :] = jnp.where(lo, jnp.where(first, vx, vw), 0.0).astype(BF16)
            vhi[sl, :] = jnp.where(lo, 0.0, jnp.where(first, vw, vx)).astype(BF16)
            return carry

        lax.fori_loop(0, n // rb, body, 0)

    q0 = qb * qb_rows
    k0 = pl.multiple_of(jnp.clip(q0 - C_HALF, 0, n - span), C_HALF)
    rows_k = pl.ds(k0, span)
    kl, kh = klo[rows_k, :], khi[rows_k, :]
    vl, vh = vlo[rows_k, :], vhi[rows_k, :]
    qpos = q0 + lax.broadcasted_iota(jnp.int32, (qb_rows, 1), 0)
    kpos = k0 + lax.broadcasted_iota(jnp.int32, (1, span), 1)
    mask = jnp.abs(qpos - kpos) <= C_HALF
    cq, sq = cosq_ref[...], sinq_ref[...]

    ntile = q_ref.shape[2] // LANES
    qt = [_rope_tile(q_ref[0, :, j * LANES:(j + 1) * LANES], cq, sq) * (HEAD_DIM ** -0.5) for j in range(ntile)]
    qt = [t for t in qt for _ in range(2)]
    kk = [kl, kh] * ntile
    sk = [sink_ref[hk * C_GROUP + h] for h in range(2 * ntile)]
    s = _each(lambda q, k: jnp.where(mask, _dot_nt(q, k), MASKED), qt, kk)
    mx = _each(lambda s, sk: jnp.maximum(jnp.max(s, axis=1, keepdims=True), sk), s, sk)
    p = _each(lambda s, mx: jnp.exp(s - mx), s, mx)
    den = _each(lambda p, sk, mx: jnp.sum(p, axis=1, keepdims=True) + jnp.exp(sk - mx), p, sk, mx)
    for j in range(ntile):
        pv = _dot(p[2 * j], vl) + _dot(p[2 * j + 1], vh)
        o_ref[0, :, j * LANES:(j + 1) * LANES] = (pv / jnp.where(lo, den[2 * j], den[2 * j + 1])).astype(o_ref.dtype)


def _attn_c(zc, sink, cos, sin, qb_rows=128):
    b, n, _ = zc.shape
    gw = C_GROUP * HEAD_DIM
    kcol = C_W // LANES
    full = lambda off: pl.BlockSpec((1, n, LANES), lambda i, h, t, off=off: (i, 0, off))
    tabq = pl.BlockSpec((qb_rows, LANES), lambda i, h, t: (t, 0))
    tab = pl.BlockSpec((n, LANES), lambda i, h, t: (0, 0))
    return pl.pallas_call(
        _attn_c_kernel,
        grid=(b, C_KV_HEADS, n // qb_rows),
        in_specs=[pl.BlockSpec(memory_space=pltpu.SMEM),
                  pl.BlockSpec((1, qb_rows, gw), lambda i, h, t: (i, t, h)),
                  full(kcol), full(kcol + 1), tabq, tabq, tab, tab],
        out_specs=pl.BlockSpec((1, qb_rows, gw), lambda i, h, t: (i, t, h)),
        out_shape=jax.ShapeDtypeStruct((b, n, C_W), BF16),
        scratch_shapes=[pltpu.VMEM((n, LANES), BF16)] * 4,
        compiler_params=_params("parallel", "parallel", "arbitrary"),
        name="attn_gqa_sink",
    )(sink.reshape(-1), zc, zc, zc, cos, sin, cos, sin)


def _na_bias_table(rpb, rows):
    kh = min(NA_KH, rows)
    qc = jnp.arange(GRID_W)[:, None]
    kc = jnp.arange(GRID_W)[None, :]
    qc0 = jnp.clip(qc - NA_KW // 2, 0, GRID_W - NA_KW)
    valid = (kc >= qc0) & (kc < qc0 + NA_KW)
    ci = jnp.clip(kc - qc, -(NA_KW - 1), NA_KW - 1) + NA_KW - 1
    nci = 2 * NA_KW - 1
    pairs = rpb.astype(F32).reshape(D_HEADS // 2, 2, 2 * NA_KH - 1, nci)
    pairs = jnp.concatenate([pairs, jnp.full(pairs.shape[:-1] + (1,), MASKED, F32)], axis=-1)
    by_row = jnp.stack([pairs[:, :, off:off + kh] for off in range(NA_KH)], axis=1)
    cls = jnp.where(valid, ci, nci)
    onehot = (cls[None] == jnp.arange(nci + 1)[:, None, None]).astype(F32)
    bias = jnp.einsum('poekc,cqj->poeqkj', by_row, onehot, precision=lax.Precision.HIGHEST)
    return bias.reshape(D_HEADS // 2, NA_KH, 2 * GRID_W, kh * GRID_W)


def _attn_d_kernel(q_ref, k_ref, v_ref, bias_ref, o_ref):
    n = q_ref.shape[1]
    rows = n // GRID_W
    kh = min(NA_KH, rows)
    lo = _lane_lo()
    group = math.gcd(16, rows)

    def body(rg, carry):
        r = [rg * group + u for u in range(group)]
        r0 = [jnp.clip(t - NA_KH // 2, 0, rows - kh) for t in r]
        rows_q = [pl.ds(pl.multiple_of(t * GRID_W, GRID_W), GRID_W) for t in r]
        rows_k = [pl.ds(pl.multiple_of(t * GRID_W, GRID_W), kh * GRID_W) for t in r0]
        q = [q_ref[0, rq, :] * (HEAD_DIM ** -0.5) for rq in rows_q]
        q2 = _each(lambda q: jnp.concatenate([jnp.where(lo, q, 0.0), jnp.where(lo, 0.0, q)], axis=0), q)
        s = [_dot_nt(q2i, k_ref[0, rk, :]) + bias_ref[0, t0 - t + NA_KH - 1]
             for q2i, rk, t0, t in zip(q2, rows_k, r0, r)]
        p = _each(lambda s: jnp.exp(s - jnp.max(s, axis=1, keepdims=True)), s)
        pv = [_dot(pi, v_ref[0, rk, :]) / jnp.sum(pi, axis=1, keepdims=True) for pi, rk in zip(p, rows_k)]
        for rq, pvi in zip(rows_q, pv):
            o_ref[0, rq, :] = jnp.where(lo, pvi[:GRID_W], pvi[GRID_W:]).astype(o_ref.dtype)
        return carry

    lax.fori_loop(0, rows // group, body, 0)


def _attn_d(zc, bias):
    b, n, _ = zc.shape
    npair = D_W // LANES
    q_off = (C_W + 2 * C_KV_HEADS * HEAD_DIM) // LANES
    blk = lambda off: pl.BlockSpec((1, n, LANES), lambda i, j, off=off: (i, 0, off + j))
    return pl.pallas_call(
        _attn_d_kernel,
        grid=(b, npair),
        in_specs=[blk(q_off), blk(q_off + npair), blk(q_off + 2 * npair),
                  pl.BlockSpec((1,) + bias.shape[1:], lambda i, j: (j, 0, 0, 0))],
        out_specs=pl.BlockSpec((1, n, LANES), lambda i, j: (i, 0, j)),
        out_shape=jax.ShapeDtypeStruct((b, n, D_W), BF16),
        compiler_params=_params("parallel", "parallel"),
        name="attn_neighborhood",
    )(zc, zc, zc, bias)


def _rw_prep_kernel(z_ref, hp_ref, hn_ref, mup_ref, mun_ref, w0_ref, w2f_ref, w2b_ref, a0_ref,
                    a2f_ref, a2b_ref, g2_ref, r_o, k_o, v_o, cumf_o, cumb_o, af_o, ab_o, g_o):
    tm = z_ref.shape[1]
    first = pl.program_id(1) == 0
    last = pl.program_id(1) == pl.num_programs(1) - 1
    row = lax.broadcasted_iota(jnp.int32, (tm, 1), 0)
    crow = row % WKV_CHUNK

    def edges(cols):
        return (jnp.where(first, 0.0, hp_ref[0, SUBLANES - 1:SUBLANES, cols]),
                jnp.where(last, 0.0, hn_ref[0, 0:1, cols]))

    def mix(z, z_prev, z_next, cols):
        return z + mup_ref[:, cols] * (z_prev - z) + mun_ref[:, cols] * (z_next - z)

    def shifted(cols):
        z = z_ref[0, :, cols]
        edge_prev, edge_next = edges(cols)
        z_prev = jnp.where(row == 0, edge_prev, pltpu.roll(z, 1, 0))
        z_next = jnp.where(row == tm - 1, edge_next, pltpu.roll(z, tm - 1, 0))
        return mix(z, z_prev, z_next, cols)

    def store_shifted(o_ref, cs, cols):
        z = z_ref[0, :, cols]
        edge_prev, edge_next = edges(cols)
        o_ref[0, :, cs] = mix(z, pltpu.roll(z, 1, 0), pltpu.roll(z, tm - 1, 0), cols)
        o_ref[0, 0:1, cs] = mix(z[0:1], edge_prev, z[1:2], cols)
        o_ref[0, tm - 1:tm, cs] = mix(z[tm - 1:tm], z[tm - 2:tm - 1], edge_next, cols)

    def log_decay(x):
        return -RW_DECAY_SCALE * jax.nn.sigmoid(x)

    def chunk_cumsum(x, rev):
        s = 1
        while s < WKV_CHUNK:
            if rev:
                x = x + jnp.where(crow < WKV_CHUNK - s, pltpu.roll(x, tm - s, 0), 0.0)
            else:
                x = x + jnp.where(crow >= s, pltpu.roll(x, s, 0), 0.0)
            s *= 2
        return x

    base = 3 * B_W
    wl = jnp.tanh(shifted(slice(base, base + LANES))).astype(BF16)
    al = shifted(slice(base + LANES, base + 2 * LANES)).astype(BF16)
    gl = jax.nn.sigmoid(shifted(slice(base + 2 * LANES, base + 4 * LANES))).astype(BF16)

    for s0 in range(0, B_W, LANES):
        cs = slice(s0, s0 + LANES)
        store_shifted(r_o, cs, cs)
        store_shifted(k_o, cs, slice(B_W + s0, B_W + s0 + LANES))
        store_shifted(v_o, cs, slice(2 * B_W + s0, 2 * B_W + s0 + LANES))
        cumf_o[0, :, cs] = chunk_cumsum(log_decay(w0_ref[0:1, cs] + _dot(wl, w2f_ref[:, cs])), False)
        cumb_o[0, :, cs] = chunk_cumsum(log_decay(w0_ref[1:2, cs] + _dot(wl, w2b_ref[:, cs])), True)
        af_o[0, :, cs] = jax.nn.sigmoid(a0_ref[0:1, cs] + _dot(al, a2f_ref[:, cs]))
        ab_o[0, :, cs] = jax.nn.sigmoid(a0_ref[1:2, cs] + _dot(al, a2b_ref[:, cs]))
        g_o[0, :, cs] = _dot(gl, g2_ref[:, cs])


def _rw_prep(zb, mu_prev, mu_next, w0, w2, a0, a2, g2, tm=256):
    b, n, cols = zb.shape
    tm = min(tm, n)
    per = tm // SUBLANES
    pad_cols = lambda t: jnp.pad(t, (0, cols - t.shape[0])).reshape(1, cols)
    rows_f = lambda t: jnp.pad(t, ((0, LANES - RW_LORA), (0, 0))).astype(BF16)
    rows_b = lambda t: jnp.pad(t, ((RW_LORA, LANES - 2 * RW_LORA), (0, 0))).astype(BF16)
    g2p = jnp.pad(g2, ((0, 2 * LANES - RW_GATE_LORA), (0, 0))).astype(BF16)
    const = lambda shape: pl.BlockSpec(shape, lambda i, t: (0,) * len(shape))
    halo_prev = pl.BlockSpec((1, SUBLANES, cols), lambda i, t: (i, jnp.maximum(t * per - 1, 0), 0))
    halo_next = pl.BlockSpec((1, SUBLANES, cols), lambda i, t: (i, jnp.minimum((t + 1) * per, n // SUBLANES - 1), 0))
    tok = jax.ShapeDtypeStruct((b, n, B_W), F32)
    return pl.pallas_call(
        _rw_prep_kernel,
        grid=(b, n // tm),
        in_specs=[pl.BlockSpec((1, tm, cols), lambda i, t: (i, t, 0)), halo_prev, halo_next,
                  const((1, cols)), const((1, cols)), const((2, B_W)), const((LANES, B_W)),
                  const((LANES, B_W)), const((2, B_W)), const((LANES, B_W)), const((LANES, B_W)),
                  const((2 * LANES, B_W))],
        out_specs=[pl.BlockSpec((1, tm, B_W), lambda i, t: (i, t, 0))] * 8,
        out_shape=[tok] * 8,
        compiler_params=_params("parallel", "parallel"),
        name="rwkv_prep",
    )(zb, zb, zb, pad_cols(mu_prev), pad_cols(mu_next), w0, rows_f(w2[0]), rows_b(w2[1]),
      a0, rows_f(a2[0]), rows_b(a2[1]), g2p)


def _head_sum(x, lo):
    return jnp.where(lo, jnp.sum(jnp.where(lo, x, 0.0), axis=1, keepdims=True),
                     jnp.sum(jnp.where(lo, 0.0, x), axis=1, keepdims=True))


def _wkv_chunks(chains):
    c = chains[0][0].shape[0]
    ii = lax.broadcasted_iota(jnp.int32, (c, c), 0)
    jj = lax.broadcasted_iota(jnp.int32, (c, c), 1)
    row = lax.broadcasted_iota(jnp.int32, (c, 1), 0)
    eye_f = (ii == jj).astype(F32)
    masks = {False: (jj <= ii, jj < ii), True: (jj >= ii, jj > ii)}
    lo = _lane_lo()
    ki = lax.broadcasted_iota(jnp.int32, (LANES, LANES), 0)
    vj = lax.broadcasted_iota(jnp.int32, (LANES, LANES), 1)
    same_head = (ki < HEAD_DIM) == (vj < HEAD_DIM)
    diag = ki == vj
    r, k, v, cum, rate, st, k_k, k_a, rev = (list(col) for col in zip(*chains))
    twice = lambda xs: [x for x in xs for _ in range(2)]
    merge = lambda xs: [jnp.where(lo, xs[2 * i], xs[2 * i + 1]) for i in range(len(xs) // 2)]
    incl = twice([masks[x][0] for x in rev])
    strict = twice([masks[x][1] for x in rev])

    kk = _each(lambda k, k_k: k * k_k, k, k_k)
    kk = _each(lambda t: t / jnp.maximum(jnp.sqrt(_head_sum(t * t, lo)), 1e-12), kk)
    kd = _each(lambda k, rate, k_a: k * (1.0 + (rate - 1.0) * k_a), k, rate, k_a)
    b = _each(lambda kk, rate: kk * rate, kk, rate)

    tot = _each(lambda cum, x: cum[0:1] if x else cum[c - 1:c], cum, rev)
    excl = _each(lambda cum, x: jnp.where(row == c - 1, 0.0, pltpu.roll(cum, c - 1, 0)) if x
                 else jnp.where(row == 0, 0.0, pltpu.roll(cum, 1, 0)), cum, rev)
    ar = _each(lambda kk, r, cum, ex: jnp.concatenate([-kk * jnp.exp(ex), r * jnp.exp(cum)], axis=0),
               kk, r, cum, excl)
    ar_h = [jnp.where(m, x, 0.0).astype(BF16) for x in ar for m in (lo, ~lo)]
    ar = _each(lambda t: t.astype(BF16), ar)
    v = _each(lambda t: t.astype(BF16), v)
    g_inv = _each(lambda cum: jnp.exp(-cum), cum)
    bg = twice(_each(lambda b, g: (b * g).astype(BF16), b, g_inv))
    kg = twice(_each(lambda kd, g: (kd * g).astype(BF16), kd, g_inv))
    x1 = _each(_dot_nt, ar_h, bg)
    x2 = _each(_dot_nt, ar_h, kg)
    a_ab = _each(lambda x, m: jnp.where(m, x[:c], 0.0), x1, strict)
    a_rb = _each(lambda x, m: jnp.where(m, x[c:], 0.0).astype(BF16), x1, incl)
    a_k = _each(lambda x, ms, mi: jnp.concatenate([jnp.where(ms, x[:c], 0.0), jnp.where(mi, x[c:], 0.0)],
                                                  axis=0).astype(BF16), x2, strict, incl)

    inv = _each(lambda t: eye_f + t, a_ab)
    pw = _each(lambda t: _dot(t, t), a_ab)
    for _ in range(c.bit_length() - 3):
        pw = _each(lambda t: t.astype(BF16), pw)
        both = _each(lambda inv, pw: _dot(jnp.concatenate([inv.astype(BF16), pw], axis=0), pw), inv, pw)
        inv = _each(lambda inv, t: inv + t[:c], inv, both)
        pw = _each(lambda t: t[c:], both)
    inv = _each(lambda inv, pw: (inv + _dot(inv, pw)).astype(BF16), inv, pw)

    akv = merge(_each(_dot, a_k, twice(v)))
    ars = _each(_dot, ar, st)
    rhs = _each(lambda ars, akv: (ars[:c] + akv[:c]).astype(BF16), ars, akv)
    p = _each(lambda t: t.astype(BF16), merge(_each(_dot, inv, twice(rhs))))
    y = _each(lambda ars, arb, akv: ars[c:] + arb + akv[c:], ars, merge(_each(_dot, a_rb, twice(p))), akv)
    g_end = _each(lambda tot, cum: jnp.exp(tot - cum), tot, cum)
    upd = _each(lambda b, kd, g, p, v: _dot_tn(jnp.concatenate([b * g, kd * g], axis=0),
                                               jnp.concatenate([p, v], axis=0)), b, kd, g_end, p, v)
    g_col = _each(lambda tot: jnp.sum(jnp.where(diag, jnp.exp(tot), 0.0), axis=1, keepdims=True), tot)
    new_st = _each(lambda g, st, upd: jnp.where(same_head, g * st + upd, 0.0), g_col, st, upd)
    return list(zip(y, new_st))


def _rw_scan_kernel(rf, kf, vf, cumf, af, rb, kb, vb, cumb, ab, kk_ref, ka_ref, yf, yb, sf, sb):
    tb = rf.shape[1]
    npair = rf.shape[2] // LANES
    c = WKV_CHUNK
    nch = tb // c
    pair = lambda j: slice(j * LANES, (j + 1) * LANES)

    @pl.when(pl.program_id(2) == 0)
    def _():
        sf[...] = jnp.zeros(sf.shape, F32)
        sb[...] = jnp.zeros(sb.shape, F32)

    def body(ci, carry):
        rows_f = pl.ds(pl.multiple_of(ci * c, c), c)
        rows_b = pl.ds(pl.multiple_of((nch - 1 - ci) * c, c), c)
        chains = []
        for j in range(npair):
            par = (kk_ref[:, pair(j)], ka_ref[:, pair(j)])
            chains.append(tuple(ref[0, rows_f, pair(j)] for ref in (rf, kf, vf, cumf, af)) + (sf[j],) + par + (False,))
            chains.append(tuple(ref[0, rows_b, pair(j)] for ref in (rb, kb, vb, cumb, ab)) + (sb[j],) + par + (True,))
        outs = _wkv_chunks(chains)
        for j in range(npair):
            yf[0, rows_f, pair(j)], sf[j] = outs[2 * j]
            yb[0, rows_b, pair(j)], sb[j] = outs[2 * j + 1]
        return carry

    lax.fori_loop(0, nch, body, 0)


def _rw_scan(r, k, v, cumf, cumb, af, ab, k_k, k_a, hb=20, tb=256):
    b, n, width = r.shape
    tb = min(tb, n)
    nt = n // tb
    hw = hb * HEAD_DIM
    fwd = pl.BlockSpec((1, tb, hw), lambda i, g, t: (i, t, g))
    bwd = pl.BlockSpec((1, tb, hw), lambda i, g, t: (i, nt - 1 - t, g))
    par = pl.BlockSpec((1, hw), lambda i, g, t: (0, g))
    out = jax.ShapeDtypeStruct((b, n, width), F32)
    return pl.pallas_call(
        _rw_scan_kernel,
        grid=(b, width // hw, nt),
        in_specs=[fwd] * 5 + [bwd] * 5 + [par, par],
        out_specs=[fwd, bwd],
        out_shape=[out, out],
        scratch_shapes=[pltpu.VMEM((hw // LANES, LANES, LANES), F32)] * 2,
        compiler_params=_params("parallel", "parallel", "arbitrary"),
        name="rwkv_scan",
    )(r, k, v, cumf, af, r, k, v, cumb, ab, k_k.reshape(1, width), k_a.reshape(1, width))


def _rw_post_kernel(yf, yb, r, k, v, af, ab, g, ka_ref, rk_ref, lnw_ref, lnb_ref, o_ref):
    lo = _lane_lo()
    head_sum = lambda x: _head_sum(x, lo)
    for j in range(o_ref.shape[2] // LANES):
        sl = slice(j * LANES, (j + 1) * LANES)
        y = yf[0, :, sl] + yb[0, :, sl]
        dev = y - head_sum(y) * (1.0 / HEAD_DIM)
        var = head_sum(dev * dev) * (1.0 / HEAD_DIM)
        yn = dev * lax.rsqrt(var + RW_GN_EPS) * lnw_ref[:, sl] + lnb_ref[:, sl]
        kt, k_a = k[0, :, sl], ka_ref[:, sl]
        kd = kt * (1.0 + (af[0, :, sl] - 1.0) * k_a) + kt * (1.0 + (ab[0, :, sl] - 1.0) * k_a)
        bonus = head_sum(r[0, :, sl] * kd * rk_ref[:, sl]) * v[0, :, sl]
        o_ref[0, :, sl] = ((yn + bonus) * g[0, :, sl]).astype(o_ref.dtype)


def _rw_post(yf, yb, r, k, v, af, ab, g, k_a, r_k, ln_w, ln_b, tm=256):
    b, n, width = r.shape
    tm = min(tm, n)
    tok = pl.BlockSpec((1, tm, width), lambda i, t: (i, t, 0))
    par = pl.BlockSpec((1, width), lambda i, t: (0, 0))
    return pl.pallas_call(
        _rw_post_kernel,
        grid=(b, n // tm),
        in_specs=[tok] * 8 + [par] * 4,
        out_specs=tok,
        out_shape=jax.ShapeDtypeStruct((b, n, width), BF16),
        compiler_params=_params("parallel", "parallel"),
        name="rwkv_post",
    )(yf, yb, r, k, v, af, ab, g, k_a.reshape(1, width), r_k.reshape(1, width),
      ln_w.reshape(1, width), ln_b.reshape(1, width))


def _rwkv7_bidir(zb, mu_prev, mu_next, w0, w2, a0, a2, g2, k_k, k_a, r_k, ln_w, ln_b):
    r, k, v, cumf, cumb, af, ab, g = _rw_prep(zb, mu_prev, mu_next, w0, w2, a0, a2, g2)
    yf, yb = _rw_scan(r, k, v, cumf, cumb, af, ab, k_k, k_a)
    return _rw_post(yf, yb, r, k, v, af, ab, g, k_a, r_k, ln_w, ln_b)


def _ffn_half_step(x, xn, w1, w3, w2, layer):
    h = _ffn_up(xn, (w1, layer), (w3, layer))
    return _mm_res([(h, (w2, layer))], x, 0.5, tm=256, tn=1024)


def _mix_ab(x, ss, g, next_gain, b, n, cos, sin, w_in, layer, w_out, mu_prev, mu_next, w0, w2, a0, a2, g2,
            k_k, k_a, r_k, ln_w, ln_b):
    qkv = 3 * A_W
    w_b = jnp.pad(w_in[layer, :, qkv:], ((0, 0), (0, RW_PAD_COLS - (w_in.shape[2] - qkv))))
    za = _proj(x, ss, g, w_in[layer, :, :qkv], qkv, tn=qkv).reshape(b, n, qkv)
    zb = _proj(x, ss, g, w_b, RW_PAD_COLS, tn=RW_PAD_COLS // 2).reshape(b, n, RW_PAD_COLS)
    oa = _attn_a(za, cos, sin).reshape(b * n, A_W)
    ob = _rwkv7_bidir(zb, mu_prev, mu_next, w0, w2, a0, a2, g2, k_k, k_a, r_k, ln_w, ln_b)
    return _mm_res([(oa, w_out[:A_W]), (ob.reshape(b * n, B_W), w_out[A_W:])], x, 1.0, tn=x.shape[1],
                   next_gain=next_gain)


def _mix_cd(x, ss, g, next_gain, b, n, cos, sin, w_in, layer, w_out, sink, rpb):
    cols = w_in.shape[2]
    zc = _proj(x, ss, g, (w_in, layer), cols, tn=cols // 2).reshape(b, n, cols)
    oc = _attn_c(zc, sink, cos, sin).reshape(b * n, C_W)
    od = _attn_d(zc, _na_bias_table(rpb, n // GRID_W)).reshape(b * n, D_W)
    return _mm_res([(oc, (w_out, layer, 0)), (od, (w_out, layer, 1))], x, 1.0, tn=x.shape[1],
                   next_gain=next_gain)


def kernel(x, p, ffn1_norm, ffn1_w1, ffn1_w3, ffn1_w2, mix_norm, ffn2_norm, ffn2_w1, ffn2_w3, ffn2_w2, ple_norm, ple_w_gate, ple_w_proj, ab_w_in, ab_w_out, rw_mu_prev, rw_mu_next, rw_w0, rw_w2, rw_a0, rw_a2, rw_g2, rw_k_k, rw_k_a, rw_r_k, rw_ln_w, rw_ln_b, cd_w_in, cd_w_out, c_sink, d_rpb, final_norm):
    b, n, d = x.shape
    depth = p.shape[0]
    cos, sin = _rope_tables(n)
    x = x.reshape(b * n, d)
    p = p.reshape(depth, b * n, -1)
    xn = _rmsnorm(x, ffn1_norm[0], BF16)
    for i in range(depth):
        j = i // 2
        x, ss = _ffn_half_step(x, xn, ffn1_w1, ffn1_w3, ffn1_w2, i)
        if i % 2 == 0:
            x, xn = _mix_ab(x, ss, mix_norm[i], ffn2_norm[i], b, n, cos, sin, ab_w_in, j, ab_w_out[j],
                            rw_mu_prev[j], rw_mu_next[j], rw_w0[j], rw_w2[j], rw_a0[j], rw_a2[j], rw_g2[j],
                            rw_k_k[j], rw_k_a[j], rw_r_k[j], rw_ln_w[j], rw_ln_b[j])
        else:
            x, xn = _mix_cd(x, ss, mix_norm[i], ffn2_norm[i], b, n, cos, sin, cd_w_in, j, cd_w_out,
                            c_sink[j], d_rpb[j])
        x, ss = _ffn_half_step(x, xn, ffn2_w1, ffn2_w3, ffn2_w2, i)
        ple_args = (x, ss, ple_norm[i], (ple_w_gate, i), (p, i), (ple_w_proj, i))
        if i + 1 < depth:
            x, xn = _ple(*ple_args, ffn1_norm[i + 1], False)
    return _ple(*ple_args, final_norm, True).reshape(b, n, d)
```
